```python
import math
import jax, jax.numpy as jnp
from jax import lax
import numpy as np

D_MODEL = 1024
BATCH = 4
SEQ = 4096
DEPTH = 4

GRID_W = 64
CTX_LEN = 256
HEAD_DIM = 128
FNET_GROUPS = 4
FNET_GROUP_DIM = 64
FNET_WIDTH = FNET_GROUPS * FNET_GROUP_DIM
GQA_Q_HEADS = 6
GQA_KV_HEADS = 2
GQA_GROUP = GQA_Q_HEADS // GQA_KV_HEADS
AB_IN_WIDTH = FNET_WIDTH + (GQA_Q_HEADS + 2 * GQA_KV_HEADS) * HEAD_DIM
AB_OUT_WIDTH = FNET_WIDTH + GQA_Q_HEADS * HEAD_DIM
ROPE_THETA = 10000.0
Q_BLOCK = 128
NA_HEADS = 8
NA_WIDTH = NA_HEADS * HEAD_DIM
NA_KH = 8
NA_KW = 16
NA_ROWS_PER_BLOCK = Q_BLOCK // GRID_W
NEG_INF = -1e30
N_EXPERTS = 64
TOP_K = 8
EXPERT_DIM = 256
SHARED_DIM = 256
ROUTE_SCALE = 2.5
MOE_BLOCK = 128
DN_ALPHA = (2 * DEPTH) ** 0.25
DN_BETA = (8 * DEPTH) ** -0.25
N_EVEN = (DEPTH + 1) // 2
N_ODD = DEPTH // 2
LN_EPS = 1e-6
RMS_EPS = 1e-6

kernel_name = "hybrid_fourier_gqa_natten_moe_deepnorm"


def layer_norm(x, g, b):
    xf = x.astype(jnp.float32)
    mu = xf.mean(-1, keepdims=True)
    xc = xf - mu
    var = (xc * xc).mean(-1, keepdims=True)
    return (xc * lax.rsqrt(var + LN_EPS) * g + b).astype(x.dtype)


def rms_norm(x, g):
    xf = x.astype(jnp.float32)
    return (xf * lax.rsqrt((xf * xf).mean(-1, keepdims=True) + RMS_EPS) * g).astype(x.dtype)


def axial_rope(x):
    n_tok = x.shape[1]
    t = jnp.arange(n_tok)
    half = HEAD_DIM // 2
    nf = half // 2
    inv = ROPE_THETA ** (-(2.0 / half) * jnp.arange(nf, dtype=jnp.float32))

    def rot(v, pos):
        ang = pos.astype(jnp.float32)[:, None] * inv
        cos = jnp.cos(ang)[:, None, :]
        sin = jnp.sin(ang)[:, None, :]
        v1, v2 = v[..., :nf], v[..., nf:]
        return jnp.concatenate([v1 * cos - v2 * sin, v1 * sin + v2 * cos], axis=-1)

    xf = x.astype(jnp.float32)
    out = jnp.concatenate([rot(xf[..., :half], t // GRID_W), rot(xf[..., half:], t % GRID_W)], axis=-1)
    return out.astype(x.dtype)


def gqa_attend(q, k, v):
    s = jnp.einsum("bqkgd,bnkd->bkgqn", q, k).astype(jnp.float32) * HEAD_DIM ** -0.5
    p = jax.nn.softmax(s, axis=-1).astype(v.dtype)
    return jnp.einsum("bkgqn,bnkd->bqkgd", p, v)


def fourier_mix(f, w_fnet):
    bsz, n_tok, _ = f.shape
    fg = f.reshape(bsz, n_tok, FNET_GROUPS, FNET_GROUP_DIM).astype(jnp.float32)
    fr = jnp.fft.fft2(fg, axes=(1, 3), norm="ortho").real.astype(f.dtype)
    return jnp.einsum("blgc,gcd->blgd", fr, w_fnet).reshape(bsz, n_tok, FNET_WIDTH)


def mixer_ab(h_lat, h_ctx, w_in, w_fnet, q_gain, k_gain, w_out, want_ctx):
    bsz, n_lat, _ = h_lat.shape
    n_ctx = h_ctx.shape[1]
    nq = GQA_Q_HEADS * HEAD_DIM
    nkv = GQA_KV_HEADS * HEAD_DIM
    cuts = [FNET_WIDTH, FNET_WIDTH + nq, FNET_WIDTH + nq + nkv]
    f, q, k, v = jnp.split(h_lat @ w_in, cuts, axis=-1)
    q = axial_rope(rms_norm(q.reshape(bsz, n_lat, GQA_Q_HEADS, HEAD_DIM), q_gain))
    k = axial_rope(rms_norm(k.reshape(bsz, n_lat, GQA_KV_HEADS, HEAD_DIM), k_gain))
    v = v.reshape(bsz, n_lat, GQA_KV_HEADS, HEAD_DIM)
    if want_ctx:
        fc, qc, kc, vc = jnp.split(h_ctx @ w_in, cuts, axis=-1)
    else:
        kc, vc = jnp.split(h_ctx @ w_in[:, cuts[1]:], [nkv], axis=-1)
    kc = rms_norm(kc.reshape(bsz, n_ctx, GQA_KV_HEADS, HEAD_DIM), k_gain)
    vc = vc.reshape(bsz, n_ctx, GQA_KV_HEADS, HEAD_DIM)
    k_all = jnp.concatenate([k, kc], axis=1)
    v_all = jnp.concatenate([v, vc], axis=1)
    nb = n_lat // Q_BLOCK
    qb = q.reshape(bsz, nb, Q_BLOCK, GQA_KV_HEADS, GQA_GROUP, HEAD_DIM).swapaxes(0, 1)
    o = lax.map(lambda qi: gqa_attend(qi, k_all, v_all), qb)
    o = o.swapaxes(0, 1).reshape(bsz, n_lat, nq)
    y_lat = jnp.concatenate([fourier_mix(f, w_fnet), o], axis=-1) @ w_out
    if not want_ctx:
        return y_lat, None
    qc = rms_norm(qc.reshape(bsz, n_ctx, GQA_Q_HEADS, HEAD_DIM), q_gain)
    qc = qc.reshape(bsz, n_ctx, GQA_KV_HEADS, GQA_GROUP, HEAD_DIM)
    oc = gqa_attend(qc, kc, vc).reshape(bsz, n_ctx, nq)
    y_ctx = jnp.concatenate([fourier_mix(fc, w_fnet), oc], axis=-1) @ w_out
    return y_lat, y_ctx


def mixer_na(h_lat, h_ctx, w_in, rpb, w_out, want_ctx):
    bsz, n_lat, _ = h_lat.shape
    n_ctx = h_ctx.shape[1]
    rows = n_lat // GRID_W
    kh = min(NA_KH, rows)
    kw = min(NA_KW, GRID_W)
    qr = NA_ROWS_PER_BLOCK
    nbr = min(qr + kh - 1, rows)
    nblk = rows // qr
    scale = HEAD_DIM ** -0.5
    q, k, v = jnp.split(h_lat @ w_in, 3, axis=-1)
    if want_ctx:
        qc, kc, vc = jnp.split(h_ctx @ w_in, 3, axis=-1)
    else:
        kc, vc = jnp.split(h_ctx @ w_in[:, NA_WIDTH:], 2, axis=-1)
    kc = kc.reshape(bsz, n_ctx, NA_HEADS, HEAD_DIM)
    vc = vc.reshape(bsz, n_ctx, NA_HEADS, HEAD_DIM)
    k_grid = k.reshape(bsz, rows, GRID_W, NA_HEADS, HEAD_DIM)
    v_grid = v.reshape(bsz, rows, GRID_W, NA_HEADS, HEAD_DIM)
    q_blocks = q.reshape(bsz, nblk, qr * GRID_W, NA_HEADS, HEAD_DIM).swapaxes(0, 1)
    col = jnp.arange(GRID_W)
    col_start = jnp.clip(col - kw // 2, 0, GRID_W - kw)
    in_col = (col[None, :] >= col_start[:, None]) & (col[None, :] < col_start[:, None] + kw)
    dc = jnp.clip(col[None, :] - col[:, None] + NA_KW - 1, 0, 2 * NA_KW - 2)

    def block(args):
        i, qi = args
        qrow = i * qr + jnp.arange(qr)
        rstart = jnp.clip(qrow - kh // 2, 0, rows - kh)
        bs = jnp.minimum(rstart[0], rows - nbr)
        krow = bs + jnp.arange(nbr)
        kb = lax.dynamic_slice_in_dim(k_grid, bs, nbr, axis=1).reshape(bsz, nbr * GRID_W, NA_HEADS, HEAD_DIM)
        vb = lax.dynamic_slice_in_dim(v_grid, bs, nbr, axis=1).reshape(bsz, nbr * GRID_W, NA_HEADS, HEAD_DIM)
        in_row = (krow[None, :] >= rstart[:, None]) & (krow[None, :] < rstart[:, None] + kh)
        dr = jnp.clip(krow[None, :] - qrow[:, None] + NA_KH - 1, 0, 2 * NA_KH - 2)
        mask = (in_row[:, None, :, None] & in_col[None, :, None, :]).reshape(qr * GRID_W, nbr * GRID_W)
        bias = rpb[:, dr[:, None, :, None], dc[None, :, None, :]].reshape(NA_HEADS, qr * GRID_W, nbr * GRID_W)
        s_loc = jnp.einsum("bqhd,bnhd->bhqn", qi, kb).astype(jnp.float32) * scale + bias.astype(jnp.float32)
        s_loc = jnp.where(mask, s_loc, NEG_INF)
        s_ctx = jnp.einsum("bqhd,bnhd->bhqn", qi, kc).astype(jnp.float32) * scale
        p = jax.nn.softmax(jnp.concatenate([s_loc, s_ctx], axis=-1), axis=-1).astype(vb.dtype)
        n_loc = nbr * GRID_W
        return (jnp.einsum("bhqn,bnhd->bqhd", p[..., :n_loc], vb)
                + jnp.einsum("bhqn,bnhd->bqhd", p[..., n_loc:], vc))

    o = lax.map(block, (jnp.arange(nblk), q_blocks))
    y_lat = o.swapaxes(0, 1).reshape(bsz, n_lat, NA_WIDTH) @ w_out
    if not want_ctx:
        return y_lat, None
    qc = qc.reshape(bsz, n_ctx, NA_HEADS, 1, HEAD_DIM)
    oc = gqa_attend(qc, kc, vc).reshape(bsz, n_ctx, NA_WIDTH)
    return y_lat, oc @ w_out


def moe_ffn(h, w_router, e_bias, w_gate, w_up, w_down, s_gate, s_up, s_down):
    n_tok, d = h.shape
    scores = jax.nn.sigmoid((h @ w_router).astype(jnp.float32))
    _, idx = lax.top_k(scores + e_bias.astype(jnp.float32), TOP_K)
    wts = jnp.take_along_axis(scores, idx, axis=-1)
    wts = wts / wts.sum(-1, keepdims=True) * ROUTE_SCALE
    n_pairs = n_tok * TOP_K
    flat_e = idx.reshape(-1)
    order = jnp.argsort(flat_e)
    sorted_e = flat_e[order]
    sorted_tok = (order // TOP_K).astype(jnp.int32)
    sorted_w = wts.reshape(-1)[order]
    counts = jnp.bincount(flat_e, length=N_EXPERTS)
    padded = (counts + MOE_BLOCK - 1) // MOE_BLOCK * MOE_BLOCK
    pad_end = jnp.cumsum(padded)
    pad_start = pad_end - padded
    grp_start = jnp.cumsum(counts) - counts
    dest = pad_start[sorted_e] + jnp.arange(n_pairs) - grp_start[sorted_e]
    n_blocks = (n_pairs + N_EXPERTS * (MOE_BLOCK - 1)) // MOE_BLOCK + 1
    n_slots = n_blocks * MOE_BLOCK
    slot_tok = jnp.full((n_slots,), n_tok, jnp.int32).at[dest].set(sorted_tok)
    slot_w = jnp.zeros((n_slots,), jnp.float32).at[dest].set(sorted_w)
    block_e = jnp.minimum(jnp.searchsorted(pad_end, jnp.arange(n_blocks) * MOE_BLOCK, side="right"),
                          N_EXPERTS - 1)
    h_pad = jnp.concatenate([h, jnp.zeros((1, d), h.dtype)], axis=0)

    def expert_block(args):
        tok, e = args
        xb = h_pad[tok]
        a = jax.nn.silu(xb @ w_gate[e]) * (xb @ w_up[e])
        return a @ w_down[e]

    yb = lax.map(expert_block, (slot_tok.reshape(n_blocks, MOE_BLOCK), block_e))
    y = yb.reshape(n_slots, d) * slot_w[:, None].astype(h.dtype)
    routed = jnp.zeros((n_tok + 1, d), h.dtype).at[slot_tok].add(y)[:n_tok]
    shared = (jax.nn.silu(h @ s_gate) * (h @ s_up)) @ s_down
    return routed + shared


def setup_inputs(seed: int = 0) -> dict:
    key = jax.random.key(seed)
    ks = iter(jax.random.split(key, 32))
    D = D_MODEL

    def nrm(shape, scale):
        return jax.random.normal(next(ks), shape, jnp.float32) * scale

    return {
        "x": nrm((BATCH, SEQ, D), 1.0),
        "c": nrm((BATCH, D), 1.0),
        "ctx": nrm((BATCH, CTX_LEN, D), 1.0),
        "c_ctx": nrm((D,), 1.0),
        "w_mod": nrm((DEPTH, D, 6 * D), 0.5 * D ** -0.5),
        "b_mod": nrm((DEPTH, 6 * D), 0.02),
        "ln1_g": 1.0 + nrm((DEPTH, D), 0.05),
        "ln1_b": nrm((DEPTH, D), 0.02),
        "ln2_g": 1.0 + nrm((DEPTH, D), 0.05),
        "ln2_b": nrm((DEPTH, D), 0.02),
        "ab_w_in": nrm((N_EVEN, D, AB_IN_WIDTH), D ** -0.5),
        "ab_w_fnet": nrm((N_EVEN, FNET_GROUPS, FNET_GROUP_DIM, FNET_GROUP_DIM), FNET_GROUP_DIM ** -0.5),
        "ab_q_norm": 1.0 + nrm((N_EVEN, HEAD_DIM), 0.05),
        "ab_k_norm": 1.0 + nrm((N_EVEN, HEAD_DIM), 0.05),
        "ab_w_out": nrm((N_EVEN, AB_OUT_WIDTH, D), AB_OUT_WIDTH ** -0.5 * DN_BETA),
        "na_w_in": nrm((N_ODD, D, 3 * NA_WIDTH), D ** -0.5),
        "na_rpb": nrm((N_ODD, NA_HEADS, 2 * NA_KH - 1, 2 * NA_KW - 1), 0.5),
        "na_w_out": nrm((N_ODD, NA_WIDTH, D), NA_WIDTH ** -0.5 * DN_BETA),
        "moe_w_router": nrm((DEPTH, D, N_EXPERTS), D ** -0.5),
        "moe_bias": nrm((DEPTH, N_EXPERTS), 0.01),
        "moe_w_gate": nrm((DEPTH, N_EXPERTS, D, EXPERT_DIM), D ** -0.5),
        "moe_w_up": nrm((DEPTH, N_EXPERTS, D, EXPERT_DIM), D ** -0.5),
        "moe_w_down": nrm((DEPTH, N_EXPERTS, EXPERT_DIM, D), EXPERT_DIM ** -0.5 * DN_BETA),
        "sh_w_gate": nrm((DEPTH, D, SHARED_DIM), D ** -0.5),
        "sh_w_up": nrm((DEPTH, D, SHARED_DIM), D ** -0.5),
        "sh_w_down": nrm((DEPTH, SHARED_DIM, D), SHARED_DIM ** -0.5 * DN_BETA),
    }


def reference(x, c, ctx, c_ctx, w_mod, b_mod, ln1_g, ln1_b, ln2_g, ln2_b,
              ab_w_in, ab_w_fnet, ab_q_norm, ab_k_norm, ab_w_out,
              na_w_in, na_rpb, na_w_out,
              moe_w_router, moe_bias, moe_w_gate, moe_w_up, moe_w_down,
              sh_w_gate, sh_w_up, sh_w_down):
    bsz, n_lat, d = x.shape
    n_ctx = ctx.shape[1]
    for l in range(DEPTH):
        want_ctx = l < DEPTH - 1
        j = l // 2
        mod = jax.nn.silu(c) @ w_mod[l] + b_mod[l]
        mod_c = jax.nn.silu(c_ctx) @ w_mod[l] + b_mod[l]
        sh1, sc1, g1, sh2, sc2, g2 = jnp.split(mod[:, None, :], 6, axis=-1)
        csh1, csc1, cg1, csh2, csc2, cg2 = jnp.split(mod_c, 6, axis=-1)
        h = x * (1.0 + sc1) + sh1
        hc = ctx * (1.0 + csc1) + csh1
        if l % 2 == 0:
            y, yc = mixer_ab(h, hc, ab_w_in[j], ab_w_fnet[j], ab_q_norm[j], ab_k_norm[j], ab_w_out[j], want_ctx)
        else:
            y, yc = mixer_na(h, hc, na_w_in[j], na_rpb[j], na_w_out[j], want_ctx)
        x = layer_norm(DN_ALPHA * x + g1 * y, ln1_g[l], ln1_b[l])
        h2 = (x * (1.0 + sc2) + sh2).reshape(bsz * n_lat, d)
        moe_args = (moe_w_router[l], moe_bias[l], moe_w_gate[l], moe_w_up[l], moe_w_down[l],
                    sh_w_gate[l], sh_w_up[l], sh_w_down[l])
        if want_ctx:
            ctx = layer_norm(DN_ALPHA * ctx + cg1 * yc, ln1_g[l], ln1_b[l])
            hc2 = (ctx * (1.0 + csc2) + csh2).reshape(bsz * n_ctx, d)
            ff = moe_ffn(jnp.concatenate([h2, hc2], axis=0), *moe_args)
            ff_lat = ff[:bsz * n_lat].reshape(bsz, n_lat, d)
            ff_ctx = ff[bsz * n_lat:].reshape(bsz, n_ctx, d)
            ctx = layer_norm(DN_ALPHA * ctx + cg2 * ff_ctx, ln2_g[l], ln2_b[l])
        else:
            ff_lat = moe_ffn(h2, *moe_args).reshape(bsz, n_lat, d)
        x = layer_norm(DN_ALPHA * x + g2 * ff_lat, ln2_g[l], ln2_b[l])
    return x
```

```python
import functools
import math

import numpy as np
import jax
import jax.numpy as jnp
from jax import lax
from jax.experimental import pallas as pl
from jax.experimental.pallas import tpu as pltpu

GRID_W = 64
HEAD_DIM = 128
FNET_GROUPS = 4
FNET_GROUP_DIM = 64
FNET_WIDTH = FNET_GROUPS * FNET_GROUP_DIM
GQA_Q_HEADS = 6
GQA_KV_HEADS = 2
GQA_GROUP = GQA_Q_HEADS // GQA_KV_HEADS
ROPE_THETA = 10000.0
NA_HEADS = 8
NA_WIDTH = NA_HEADS * HEAD_DIM
NA_KH = 8
NA_KW = 16
NA_Q_ROWS = 2
NEG_INF = -1e30
N_EXPERTS = 64
TOP_K = 8
EXPERT_DIM = 256
ROUTE_SCALE = 2.5
LN_EPS = 1e-6
RMS_EPS = 1e-6
ATTN_SCALE = HEAD_DIM ** -0.5

V7X_LANES = 128
V7X_SUBLANES = 8
V7X_VMEM_LIMIT_BYTES = 56 * 1024 * 1024

TOKEN_TILE = 256
MOD_ROWS = 8

F32 = jnp.float32
BF16 = jnp.bfloat16


def _params(*sem):
    return pltpu.CompilerParams(dimension_semantics=sem, vmem_limit_bytes=V7X_VMEM_LIMIT_BYTES)


def _bdot(a, b):
    return jnp.dot(a, b, preferred_element_type=F32)


def _bdot_t(a, b):
    return lax.dot_general(a, b, (((1,), (1,)), ((), ())), preferred_element_type=F32)


def _split(x):
    hi = x.astype(BF16)
    lo = (x - hi.astype(F32)).astype(BF16)
    return hi, lo


def _dot3(a, b):
    ah, al = _split(a)
    bh, bl = _split(b)
    return _bdot(ah, bh) + (_bdot(ah, bl) + _bdot(al, bh))


def _silu(x):
    return x * jax.nn.sigmoid(x)


def _layer_norm(z, g, b):
    mu = jnp.mean(z, axis=-1, keepdims=True)
    zc = z - mu
    var = jnp.mean(zc * zc, axis=-1, keepdims=True)
    return zc * lax.rsqrt(var + LN_EPS) * g + b


class _Stream:
    def __init__(self, bsz, n_lat, n_ctx, d):
        assert n_lat % TOKEN_TILE == 0 and n_ctx % TOKEN_TILE == 0
        assert bsz < MOD_ROWS
        self.bsz, self.n_lat, self.n_ctx, self.d = bsz, n_lat, n_ctx, d
        self.nt = n_lat + n_ctx
        self.t = bsz * self.nt
        self.tiles_per_sample = self.nt // TOKEN_TILE
        self.lat_tiles = n_lat // TOKEN_TILE
        self.n_tiles = self.t // TOKEN_TILE

    def mod_row(self, tile):
        return jnp.where(tile % self.tiles_per_sample < self.lat_tiles, tile // self.tiles_per_sample, self.bsz)

    def mod_spec(self, layer):
        return pl.BlockSpec((None, None, 6, self.d), lambda t: (layer, self.mod_row(t), 0, 0))

    def tok_spec(self, width):
        return pl.BlockSpec((TOKEN_TILE, width), lambda t: (t, 0))


def _full_spec(shape):
    nd = len(shape)
    return pl.BlockSpec(shape, lambda *_: (0,) * nd)


def _mod_kernel(cc_ref, w_ref, b_ref, o_ref):
    o_ref[...] = _dot3(_silu(cc_ref[...]), w_ref[...]) + b_ref[...]


def _mod_call(cc, w_mod, b_mod):
    depth, d, n = w_mod.shape
    tn = n // 4
    return pl.pallas_call(
        _mod_kernel,
        grid=(depth, n // tn),
        in_specs=[pl.BlockSpec((MOD_ROWS, d), lambda l, j: (0, 0)),
                  pl.BlockSpec((None, d, tn), lambda l, j: (l, 0, j)),
                  pl.BlockSpec((None, 1, tn), lambda l, j: (l, 0, j))],
        out_specs=pl.BlockSpec((None, MOD_ROWS, tn), lambda l, j: (l, 0, j)),
        out_shape=jax.ShapeDtypeStruct((depth, MOD_ROWS, n), F32),
        compiler_params=_params("arbitrary", "arbitrary"),
        name="mod",
    )(cc, w_mod, b_mod.reshape(depth, 1, n))


def _modulate_kernel(x_ref, mod_ref, h_ref):
    h_ref[...] = (x_ref[...] * (1.0 + mod_ref[1:2, :]) + mod_ref[0:1, :]).astype(BF16)


def _modulate_call(st, x, mod, layer):
    return pl.pallas_call(
        _modulate_kernel,
        grid=(st.n_tiles,),
        in_specs=[st.tok_spec(st.d), st.mod_spec(layer)],
        out_specs=st.tok_spec(st.d),
        out_shape=jax.ShapeDtypeStruct((st.t, st.d), BF16),
        compiler_params=_params("arbitrary"),
        name="modulate",
    )(x, mod)


def _rope_tables(n_lat, n_ctx):
    half = HEAD_DIM // 2
    nf = half // 2
    t = np.arange(n_lat)
    inv = ROPE_THETA ** (-(2.0 / half) * np.arange(nf, dtype=np.float64))
    ang_r = (t // GRID_W)[:, None] * inv
    ang_c = (t % GRID_W)[:, None] * inv
    zeros = np.zeros_like(ang_r)
    cos = np.concatenate([np.cos(ang_r), np.cos(ang_r), np.cos(ang_c), np.cos(ang_c)], axis=1)
    sin_fwd = np.concatenate([-np.sin(ang_r), zeros, -np.sin(ang_c), zeros], axis=1)
    sin_bwd = np.concatenate([zeros, np.sin(ang_r), zeros, np.sin(ang_c)], axis=1)
    pad = lambda a, v: np.concatenate([a, np.full((n_ctx, HEAD_DIM), v)], axis=0).astype(np.float32)
    return jnp.asarray(pad(cos, 1.0)), jnp.asarray(pad(sin_fwd, 0.0)), jnp.asarray(pad(sin_bwd, 0.0))


def _ab_in_kernel(h_ref, w_ref, qg_ref, kg_ref, cos_ref, sf_ref, sb_ref, f_ref, q_ref, k_ref, v_ref):
    acc = _bdot(h_ref[...], w_ref[...])
    cos, sf, sb = cos_ref[...], sf_ref[...], sb_ref[...]
    nf = HEAD_DIM // 4

    def norm_rope(xh, gain):
        ms = jnp.mean(xh * xh, axis=-1, keepdims=True)
        y = xh * lax.rsqrt(ms + RMS_EPS) * gain
        return y * cos + pltpu.roll(y, HEAD_DIM - nf, 1) * sf + pltpu.roll(y, nf, 1) * sb

    f_ref[...] = acc[:, :FNET_WIDTH]
    q0 = FNET_WIDTH
    k0 = q0 + GQA_Q_HEADS * HEAD_DIM
    v0 = k0 + GQA_KV_HEADS * HEAD_DIM
    for h in range(GQA_Q_HEADS):
        xh = acc[:, q0 + h * HEAD_DIM:q0 + (h + 1) * HEAD_DIM]
        q_ref[:, h * HEAD_DIM:(h + 1) * HEAD_DIM] = (norm_rope(xh, qg_ref[...]) * ATTN_SCALE).astype(BF16)
    for h in range(GQA_KV_HEADS):
        xh = acc[:, k0 + h * HEAD_DIM:k0 + (h + 1) * HEAD_DIM]
        k_ref[:, h * HEAD_DIM:(h + 1) * HEAD_DIM] = norm_rope(xh, kg_ref[...]).astype(BF16)
    v_ref[...] = acc[:, v0:].astype(BF16)


def _ab_in_call(st, h, w_in, q_gain, k_gain, rope):
    nq = GQA_Q_HEADS * HEAD_DIM
    nkv = GQA_KV_HEADS * HEAD_DIM
    pos_spec = pl.BlockSpec((TOKEN_TILE, HEAD_DIM), lambda t: (t % st.tiles_per_sample, 0))
    return pl.pallas_call(
        _ab_in_kernel,
        grid=(st.n_tiles,),
        in_specs=[st.tok_spec(st.d), _full_spec(w_in.shape), _full_spec((1, HEAD_DIM)), _full_spec((1, HEAD_DIM)),
                  pos_spec, pos_spec, pos_spec],
        out_specs=[st.tok_spec(FNET_WIDTH), st.tok_spec(nq), st.tok_spec(nkv), st.tok_spec(nkv)],
        out_shape=[jax.ShapeDtypeStruct((st.t, FNET_WIDTH), F32), jax.ShapeDtypeStruct((st.t, nq), BF16),
                   jax.ShapeDtypeStruct((st.t, nkv), BF16), jax.ShapeDtypeStruct((st.t, nkv), BF16)],
        compiler_params=_params("arbitrary"),
        name="ab_in",
    )(h, w_in, q_gain.reshape(1, HEAD_DIM), k_gain.reshape(1, HEAD_DIM), *rope)


def _attend(q, keys_values, biases):
    scores = []
    for (k, _), bias in zip(keys_values, biases):
        s = _bdot_t(q, k)
        scores.append(s if bias is None else s + bias)
    m = scores[0].max(axis=-1, keepdims=True)
    for s in scores[1:]:
        m = jnp.maximum(m, s.max(axis=-1, keepdims=True))
    num = None
    den = None
    for s, (_, v) in zip(scores, keys_values):
        p = jnp.exp(s - m)
        pv = _bdot(p.astype(BF16), v)
        ps = p.sum(axis=-1, keepdims=True)
        num = pv if num is None else num + pv
        den = ps if den is None else den + ps
    return num / den


def _gqa_kernel(q_ref, k_ref, v_ref, o_ref, *, n_lat, lat_tiles):
    def run(k, v):
        for h in range(GQA_GROUP):
            sl = slice(h * HEAD_DIM, (h + 1) * HEAD_DIM)
            o_ref[:, sl] = _attend(q_ref[:, sl], [(k, v)], [None]).astype(BF16)

    is_lat = pl.program_id(2) < lat_tiles

    @pl.when(is_lat)
    def _():
        run(k_ref[...], v_ref[...])

    @pl.when(jnp.logical_not(is_lat))
    def _():
        run(k_ref[n_lat:, :], v_ref[n_lat:, :])


def _gqa_call(st, q, k, v):
    gw = GQA_GROUP * HEAD_DIM
    q3 = q.reshape(st.bsz, st.nt, GQA_Q_HEADS * HEAD_DIM)
    k3 = k.reshape(st.bsz, st.nt, GQA_KV_HEADS * HEAD_DIM)
    v3 = v.reshape(st.bsz, st.nt, GQA_KV_HEADS * HEAD_DIM)
    q_spec = pl.BlockSpec((None, TOKEN_TILE, gw), lambda b, g, i: (b, i, g))
    kv_spec = pl.BlockSpec((None, st.nt, HEAD_DIM), lambda b, g, i: (b, 0, g))
    o = pl.pallas_call(
        functools.partial(_gqa_kernel, n_lat=st.n_lat, lat_tiles=st.lat_tiles),
        grid=(st.bsz, GQA_KV_HEADS, st.tiles_per_sample),
        in_specs=[q_spec, kv_spec, kv_spec],
        out_specs=q_spec,
        out_shape=jax.ShapeDtypeStruct(q3.shape, BF16),
        compiler_params=_params("arbitrary", "arbitrary", "arbitrary"),
        name="gqa",
    )(q3, k3, v3)
    return o.reshape(st.t, GQA_Q_HEADS * HEAD_DIM)


def _fft_split(n):
    l1 = 1 << ((n.bit_length() - 1 + 1) // 2)
    assert n % l1 == 0 and n == l1 * (n // l1)
    return l1, n // l1


def _fft_tables(n):
    l1, l2 = _fft_split(n)
    a = np.arange(l1, dtype=np.float64)
    ang1 = 2.0 * np.pi * np.outer(a, a) / l1
    stage1 = np.concatenate([np.cos(ang1), -np.sin(ang1)], axis=0)
    b = np.arange(l2, dtype=np.float64)
    ang_t = 2.0 * np.pi * np.outer(b, a) / n
    tw_cos = np.cos(ang_t)[:, :, None]
    tw_sin = np.sin(ang_t)[:, :, None]
    ang2 = 2.0 * np.pi * np.outer(b, b) / l2
    c2, s2 = np.cos(ang2), np.sin(ang2)
    stage2 = np.block([[c2, s2], [-s2, c2]])
    f32 = lambda x: jnp.asarray(x.astype(np.float32))
    return f32(stage1), f32(tw_cos), f32(tw_sin), f32(stage2)


FNET_GROUPS_PER_SLAB = V7X_LANES // FNET_GROUP_DIM
FNET_SLABS = FNET_WIDTH // V7X_LANES


def _fnet_channel_tables(n_positions):
    c = np.arange(FNET_GROUP_DIM, dtype=np.float64)
    ang = 2.0 * np.pi * np.outer(c, c) / FNET_GROUP_DIM
    eye = np.eye(FNET_GROUPS_PER_SLAB)
    scale = 1.0 / math.sqrt(n_positions * FNET_GROUP_DIM)
    m = np.concatenate([np.kron(eye, np.cos(ang)), np.kron(eye, np.sin(ang))], axis=0) * scale
    return jnp.asarray(m.astype(np.float32))


def _fnet_part(f_ref, o_ref, a_ref, row0, n, s1_ref, tc_ref, ts_ref, s2_ref, ch_ref, wf_ref):
    l1, l2 = _fft_split(n)
    stage1 = s1_ref[...]
    stage2 = s2_ref[...]
    chan = ch_ref[...]
    wf = wf_ref[...]

    def first(j, carry):
        xs = f_ref[pl.ds(row0 + j, l1, stride=l2), :]
        a = _dot3(stage1, xs)
        ar, ai = a[:l1], a[l1:]
        tc, ts = tc_ref[j], ts_ref[j]
        a_ref[0, pl.ds(pl.multiple_of(j * l1, l1), l1), :] = ar * tc + ai * ts
        a_ref[1, pl.ds(pl.multiple_of(j * l1, l1), l1), :] = ai * tc - ar * ts
        return carry

    lax.fori_loop(0, l2, first, 0)

    def second(j, carry):
        br = a_ref[0, pl.ds(j, l2, stride=l1), :]
        bi = a_ref[1, pl.ds(j, l2, stride=l1), :]
        p = _dot3(stage2, jnp.concatenate([br, bi], axis=0))
        re = _dot3(jnp.concatenate([p[:l2], p[l2:]], axis=1), chan)
        o_ref[pl.ds(row0 + j, l2, stride=l1), :] = _bdot(re.astype(BF16), wf)
        return carry

    lax.fori_loop(0, l1, second, 0)


def _fnet_kernel(f_ref, s1l, tcl, tsl, s2l, chl, s1c, tcc, tsc, s2c, chc, wf_ref, o_ref, a_ref, *, n_lat, n_ctx):
    _fnet_part(f_ref, o_ref, a_ref, 0, n_lat, s1l, tcl, tsl, s2l, chl, wf_ref)
    _fnet_part(f_ref, o_ref, a_ref, n_lat, n_ctx, s1c, tcc, tsc, s2c, chc, wf_ref)


def _fnet_call(st, f, w_fnet):
    gps = FNET_GROUPS_PER_SLAB
    eye = jnp.eye(gps, dtype=F32)
    wg = w_fnet.reshape(FNET_SLABS, gps, FNET_GROUP_DIM, FNET_GROUP_DIM)
    wf = (eye[None, :, None, :, None] * wg[:, :, :, None, :]).reshape(FNET_SLABS, V7X_LANES, V7X_LANES).astype(BF16)
    consts = (*_fft_tables(st.n_lat), _fnet_channel_tables(st.n_lat),
              *_fft_tables(st.n_ctx), _fnet_channel_tables(st.n_ctx))
    f3 = f.reshape(st.bsz, st.nt, FNET_WIDTH)
    blk = pl.BlockSpec((None, st.nt, V7X_LANES), lambda b, s: (b, 0, s))
    o = pl.pallas_call(
        functools.partial(_fnet_kernel, n_lat=st.n_lat, n_ctx=st.n_ctx),
        grid=(st.bsz, FNET_SLABS),
        in_specs=[blk] + [_full_spec(c.shape) for c in consts]
        + [pl.BlockSpec((None, V7X_LANES, V7X_LANES), lambda b, s: (s, 0, 0))],
        out_specs=blk,
        out_shape=jax.ShapeDtypeStruct(f3.shape, F32),
        scratch_shapes=[pltpu.VMEM((2, st.n_lat, V7X_LANES), F32)],
        compiler_params=_params("arbitrary", "arbitrary"),
        name="fnet",
    )(f3, *consts, wf)
    return o.reshape(st.t, FNET_WIDTH)


def _route(h2, wr, e_bias):
    scores = jax.nn.sigmoid(_dot3(h2, wr))
    sel = scores + e_bias
    lane = lax.broadcasted_iota(jnp.int32, sel.shape, 1)
    chosen = jnp.zeros(sel.shape, jnp.bool_)
    for _ in range(TOP_K):
        best = sel.max(axis=-1, keepdims=True)
        first = jnp.where(sel == best, lane, N_EXPERTS).min(axis=-1, keepdims=True)
        hit = lane == first
        chosen = jnp.logical_or(chosen, hit)
        sel = jnp.where(hit, -jnp.inf, sel)
    w = jnp.where(chosen, scores, 0.0)
    return w / w.sum(axis=-1, keepdims=True) * ROUTE_SCALE


def _out_ln_kernel(*refs, n_in, alpha):
    a_refs = refs[:n_in]
    w_refs = refs[n_in:2 * n_in]
    x_ref, mod_ref, g_ref, b_ref, wr_ref, eb_ref, x1_ref, h2_ref, cw_ref = refs[2 * n_in:]
    y = None
    for a_ref, w_ref in zip(a_refs, w_refs):
        part = _bdot(a_ref[...].astype(BF16), w_ref[...])
        y = part if y is None else y + part
    z = alpha * x_ref[...] + mod_ref[2:3, :] * y
    x1 = _layer_norm(z, g_ref[...], b_ref[...])
    x1_ref[...] = x1
    h2 = x1 * (1.0 + mod_ref[4:5, :]) + mod_ref[3:4, :]
    h2_ref[...] = h2.astype(BF16)
    cw_ref[...] = _route(h2, wr_ref[...], eb_ref[...])


def _out_ln_call(st, acts, w_out, x, mod, layer, ln_g, ln_b, w_router, e_bias, alpha):
    ws, r0 = [], 0
    for a in acts:
        ws.append(w_out[r0:r0 + a.shape[1]].astype(BF16))
        r0 += a.shape[1]
    assert r0 == w_out.shape[0]
    row = lambda v: v.reshape(1, -1)
    return pl.pallas_call(
        functools.partial(_out_ln_kernel, n_in=len(acts), alpha=alpha),
        grid=(st.n_tiles,),
        in_specs=[st.tok_spec(a.shape[1]) for a in acts] + [_full_spec(w.shape) for w in ws]
        + [st.tok_spec(st.d), st.mod_spec(layer), _full_spec((1, st.d)), _full_spec((1, st.d)),
           _full_spec(w_router.shape), _full_spec((1, N_EXPERTS))],
        out_specs=[st.tok_spec(st.d), st.tok_spec(st.d), st.tok_spec(N_EXPERTS)],
        out_shape=[jax.ShapeDtypeStruct((st.t, st.d), F32), jax.ShapeDtypeStruct((st.t, st.d), BF16),
                   jax.ShapeDtypeStruct((st.t, N_EXPERTS), F32)],
        compiler_params=_params("arbitrary"),
        name="out_ln",
    )(*acts, *ws, x, mod, row(ln_g), row(ln_b), w_router, row(e_bias))


MOE_TOKEN_TILE = 1024


def _moe_dense_kernel(h_ref, cw_ref, wg_ref, wu_ref, wd_ref, o_ref):
    e = pl.program_id(1)

    @pl.when(e == 0)
    def _():
        o_ref[...] = jnp.zeros_like(o_ref)

    h = h_ref[...]
    a = _silu(_bdot(h, wg_ref[...].astype(BF16))) * _bdot(h, wu_ref[...].astype(BF16))
    cw = cw_ref[...]
    lane = lax.broadcasted_iota(jnp.int32, cw.shape, 1)
    w = jnp.where(lane == e, cw, 0.0).sum(axis=-1, keepdims=True)
    o_ref[...] += _bdot(a.astype(BF16), wd_ref[...].astype(BF16)) * w


def _moe_dense_call(st, h2, cw, w_gate, w_up, w_down):
    tm = MOE_TOKEN_TILE
    assert st.t % tm == 0
    return pl.pallas_call(
        _moe_dense_kernel,
        grid=(st.t // tm, N_EXPERTS),
        in_specs=[pl.BlockSpec((tm, st.d), lambda t, e: (t, 0)),
                  pl.BlockSpec((tm, N_EXPERTS), lambda t, e: (t, 0)),
                  pl.BlockSpec((None, st.d, EXPERT_DIM), lambda t, e: (e, 0, 0)),
                  pl.BlockSpec((None, st.d, EXPERT_DIM), lambda t, e: (e, 0, 0)),
                  pl.BlockSpec((None, EXPERT_DIM, st.d), lambda t, e: (e, 0, 0))],
        out_specs=pl.BlockSpec((tm, st.d), lambda t, e: (t, 0)),
        out_shape=jax.ShapeDtypeStruct((st.t, st.d), F32),
        compiler_params=_params("arbitrary", "arbitrary"),
        name="moe_dense",
    )(h2, cw, w_gate, w_up, w_down)


def _ffn_ln_kernel(r_ref, h2_ref, x1_ref, sg_ref, su_ref, sd_ref, mod_ref, modn_ref, g_ref, b_ref, x2_ref, hn_ref,
                   *, alpha):
    h2 = h2_ref[...]
    a = _silu(_bdot(h2, sg_ref[...])) * _bdot(h2, su_ref[...])
    ff = r_ref[...] + _bdot(a.astype(BF16), sd_ref[...])
    z = alpha * x1_ref[...] + mod_ref[5:6, :] * ff
    x2 = _layer_norm(z, g_ref[...], b_ref[...])
    x2_ref[...] = x2
    hn_ref[...] = (x2 * (1.0 + modn_ref[1:2, :]) + modn_ref[0:1, :]).astype(BF16)


def _ffn_ln_call(st, routed, h2, x1, s_gate, s_up, s_down, mod, layer, next_layer, ln_g, ln_b, alpha):
    row = lambda v: v.reshape(1, -1)
    sg, su, sd = s_gate.astype(BF16), s_up.astype(BF16), s_down.astype(BF16)
    return pl.pallas_call(
        functools.partial(_ffn_ln_kernel, alpha=alpha),
        grid=(st.n_tiles,),
        in_specs=[st.tok_spec(st.d), st.tok_spec(st.d), st.tok_spec(st.d),
                  _full_spec(sg.shape), _full_spec(su.shape), _full_spec(sd.shape),
                  st.mod_spec(layer), st.mod_spec(next_layer), _full_spec((1, st.d)), _full_spec((1, st.d))],
        out_specs=[st.tok_spec(st.d), st.tok_spec(st.d)],
        out_shape=[jax.ShapeDtypeStruct((st.t, st.d), F32), jax.ShapeDtypeStruct((st.t, st.d), BF16)],
        compiler_params=_params("arbitrary"),
        name="ffn_ln",
    )(routed, h2, x1, sg, su, sd, mod, mod, row(ln_g), row(ln_b))


def _na_in_kernel(h_ref, w_ref, o_ref):
    scale = jnp.where(pl.program_id(1) == 0, ATTN_SCALE, 1.0)
    o_ref[...] = (_bdot(h_ref[...], w_ref[...]) * scale).astype(BF16)


def _na_in_call(st, h, w_in):
    return pl.pallas_call(
        _na_in_kernel,
        grid=(st.n_tiles, 3),
        in_specs=[pl.BlockSpec((TOKEN_TILE, st.d), lambda t, j: (t, 0)),
                  pl.BlockSpec((st.d, NA_WIDTH), lambda t, j: (0, j))],
        out_specs=pl.BlockSpec((TOKEN_TILE, NA_WIDTH), lambda t, j: (t, j)),
        out_shape=jax.ShapeDtypeStruct((st.t, 3 * NA_WIDTH), BF16),
        compiler_params=_params("arbitrary", "arbitrary"),
        name="na_in",
    )(h, w_in)


def _na_geometry(n_lat):
    rows = n_lat // GRID_W
    kh, kw, qr = min(NA_KH, rows), min(NA_KW, GRID_W), NA_Q_ROWS
    nbr = min(qr + kh - 1, rows)
    col = np.arange(GRID_W)
    col_start = np.clip(col - kw // 2, 0, GRID_W - kw)
    in_col = (col[None, :] >= col_start[:, None]) & (col[None, :] < col_start[:, None] + kw)
    dc = np.clip(col[None, :] - col[:, None] + NA_KW - 1, 0, 2 * NA_KW - 2)
    starts, variant_of, variants = [], [], {}
    for i in range(rows // qr):
        qrow = i * qr + np.arange(qr)
        rstart = np.clip(qrow - kh // 2, 0, rows - kh)
        bs = min(int(rstart[0]), rows - nbr)
        krow = bs + np.arange(nbr)
        in_row = (krow[None, :] >= rstart[:, None]) & (krow[None, :] < rstart[:, None] + kh)
        dr = np.clip(krow[None, :] - qrow[:, None] + NA_KH - 1, 0, 2 * NA_KH - 2)
        key = (in_row.tobytes(), dr.tobytes())
        if key not in variants:
            mask = (in_row[:, None, :, None] & in_col[None, :, None, :]).reshape(qr * GRID_W, nbr * GRID_W)
            variants[key] = (len(variants), dr, mask)
        starts.append(bs)
        variant_of.append(variants[key][0])
    ordered = sorted(variants.values(), key=lambda v: v[0])
    return nbr, np.asarray(starts, np.int32), np.asarray(variant_of, np.int32), [(v[1], v[2]) for v in ordered], dc


def _na_bias_tables(rpb, n_lat):
    nbr, starts, variant_of, variants, dc = _na_geometry(n_lat)
    tables = []
    for dr, mask in variants:
        bias = rpb[:, dr[:, None, :, None], dc[None, :, None, :]].reshape(NA_HEADS, NA_Q_ROWS * GRID_W, nbr * GRID_W)
        tables.append(jnp.where(jnp.asarray(mask)[None], bias, NEG_INF))
    return nbr, starts, variant_of, jnp.stack(tables)


NA_Q_TILE = NA_Q_ROWS * GRID_W


def _na_kernel(start_ref, var_ref, q_ref, k_ref, v_ref, bias_ref, o_ref, *, n_lat, band):
    i = pl.program_id(1)
    is_lat = i < n_lat // NA_Q_TILE

    @pl.when(is_lat)
    def _():
        off = pl.multiple_of(start_ref[i] * GRID_W, GRID_W)
        for h in range(NA_HEADS):
            sl = slice(h * HEAD_DIM, (h + 1) * HEAD_DIM)
            local = (k_ref[pl.ds(off, band), sl], v_ref[pl.ds(off, band), sl])
            ctx = (k_ref[n_lat:, sl], v_ref[n_lat:, sl])
            o_ref[:, sl] = _attend(q_ref[:, sl], [local, ctx], [bias_ref[h], None]).astype(BF16)

    @pl.when(jnp.logical_not(is_lat))
    def _():
        for h in range(NA_HEADS):
            sl = slice(h * HEAD_DIM, (h + 1) * HEAD_DIM)
            o_ref[:, sl] = _attend(q_ref[:, sl], [(k_ref[n_lat:, sl], v_ref[n_lat:, sl])], [None]).astype(BF16)


def _na_call(st, qkv, rpb):
    nbr, starts, variant_of, bias = _na_bias_tables(rpb, st.n_lat)
    band = nbr * GRID_W
    n_q = st.nt // NA_Q_TILE
    pad = n_q - starts.shape[0]
    starts = jnp.asarray(np.concatenate([starts, np.zeros(pad, np.int32)]))
    variant_of = jnp.asarray(np.concatenate([variant_of, np.zeros(pad, np.int32)]))
    qkv3 = qkv.reshape(st.bsz, st.nt, 3 * NA_WIDTH)
    grid_spec = pltpu.PrefetchScalarGridSpec(
        num_scalar_prefetch=2,
        grid=(st.bsz, n_q),
        in_specs=[pl.BlockSpec((None, NA_Q_TILE, NA_WIDTH), lambda b, i, s, v: (b, i, 0)),
                  pl.BlockSpec((None, st.nt, NA_WIDTH), lambda b, i, s, v: (b, 0, 1)),
                  pl.BlockSpec((None, st.nt, NA_WIDTH), lambda b, i, s, v: (b, 0, 2)),
                  pl.BlockSpec((None, NA_HEADS, NA_Q_TILE, band), lambda b, i, s, v: (v[i], 0, 0, 0))],
        out_specs=pl.BlockSpec((None, NA_Q_TILE, NA_WIDTH), lambda b, i, s, v: (b, i, 0)),
    )
    o = pl.pallas_call(
        functools.partial(_na_kernel, n_lat=st.n_lat, band=band),
        grid_spec=grid_spec,
        out_shape=jax.ShapeDtypeStruct((st.bsz, st.nt, NA_WIDTH), BF16),
        compiler_params=_params("arbitrary", "arbitrary"),
        name="na_attn",
    )(starts, variant_of, qkv3, qkv3, qkv3, bias)
    return o.reshape(st.t, NA_WIDTH)


def kernel(x, c, ctx, c_ctx, w_mod, b_mod, ln1_g, ln1_b, ln2_g, ln2_b, ab_w_in, ab_w_fnet, ab_q_norm, ab_k_norm,
           ab_w_out, na_w_in, na_rpb, na_w_out, moe_w_router, moe_bias, moe_w_gate, moe_w_up, moe_w_down,
           sh_w_gate, sh_w_up, sh_w_down):
    bsz, n_lat, d = x.shape
    n_ctx = ctx.shape[1]
    depth = w_mod.shape[0]
    st = _Stream(bsz, n_lat, n_ctx, d)
    alpha = (2 * depth) ** 0.25

    cc = jnp.concatenate([c, c_ctx[None, :], jnp.zeros((MOD_ROWS - bsz - 1, d), F32)], axis=0)
    mod = _mod_call(cc, w_mod, b_mod).reshape(depth, MOD_ROWS, 6, d)
    xs = jnp.concatenate([x, ctx], axis=1).reshape(st.t, d)
    h = _modulate_call(st, xs, mod, 0)
    rope = _rope_tables(n_lat, n_ctx)

    for l in range(depth):
        j = l // 2
        if l % 2 == 0:
            f, q, k, v = _ab_in_call(st, h, ab_w_in[j].astype(BF16), ab_q_norm[j], ab_k_norm[j], rope)
            acts = [_fnet_call(st, f, ab_w_fnet[j]), _gqa_call(st, q, k, v)]
            w_out = ab_w_out[j]
        else:
            qkv = _na_in_call(st, h, na_w_in[j].astype(BF16))
            acts = [_na_call(st, qkv, na_rpb[j])]
            w_out = na_w_out[j]
        x1, h2, cw = _out_ln_call(st, acts, w_out, xs, mod, l, ln1_g[l], ln1_b[l], moe_w_router[l], moe_bias[l], alpha)
        routed = _moe_dense_call(st, h2, cw, moe_w_gate[l], moe_w_up[l], moe_w_down[l])
        xs, h = _ffn_ln_call(st, routed, h2, x1, sh_w_gate[l], sh_w_up[l], sh_w_down[l], mod, l,
                             min(l + 1, depth - 1), ln2_g[l], ln2_b[l], alpha)
    return xs.reshape(bsz, st.nt, d)[:, :n_lat]
```

```python
import functools
import math

import numpy as np
import jax
import jax.numpy as jnp
from jax import lax
from jax.experimental import pallas as pl
from jax.experimental.pallas import tpu as pltpu

GRID_W = 64
HEAD_DIM = 128
FNET_GROUPS = 4
FNET_GROUP_DIM = 64
FNET_WIDTH = FNET_GROUPS * FNET_GROUP_DIM
GQA_Q_HEADS = 6
GQA_KV_HEADS = 2
GQA_GROUP = GQA_Q_HEADS // GQA_KV_HEADS
ROPE_THETA = 10000.0
NA_HEADS = 8
NA_WIDTH = NA_HEADS * HEAD_DIM
NA_KH = 8
NA_KW = 16
NA_Q_ROWS = 2
NEG_INF = -1e30
N_EXPERTS = 64
TOP_K = 8
EXPERT_DIM = 256
ROUTE_SCALE = 2.5
LN_EPS = 1e-6
RMS_EPS = 1e-6
ATTN_SCALE = HEAD_DIM ** -0.5

V7X_LANES = 128
V7X_SUBLANES = 8
V7X_VMEM_LIMIT_BYTES = 56 * 1024 * 1024

TOKEN_TILE = 256
MOD_ROWS = 8

F32 = jnp.float32
BF16 = jnp.bfloat16


def _params(*sem):
    return pltpu.CompilerParams(dimension_semantics=sem, vmem_limit_bytes=V7X_VMEM_LIMIT_BYTES)


def _bdot(a, b):
    return jnp.dot(a, b, preferred_element_type=F32)


def _bdot_t(a, b):
    return lax.dot_general(a, b, (((1,), (1,)), ((), ())), preferred_element_type=F32)


def _split(x):
    hi = x.astype(BF16)
    lo = (x - hi.astype(F32)).astype(BF16)
    return hi, lo


def _dot3(a, b):
    ah, al = _split(a)
    bh, bl = _split(b)
    return _bdot(ah, bh) + (_bdot(ah, bl) + _bdot(al, bh))


def _silu(x):
    return x * jax.nn.sigmoid(x)


def _layer_norm(z, g, b):
    mu = jnp.mean(z, axis=-1, keepdims=True)
    zc = z - mu
    var = jnp.mean(zc * zc, axis=-1, keepdims=True)
    return zc * lax.rsqrt(var + LN_EPS) * g + b


class _Stream:
    def __init__(self, bsz, n_lat, n_ctx, d):
        assert n_lat % TOKEN_TILE == 0 and n_ctx % TOKEN_TILE == 0
        assert bsz < MOD_ROWS
        self.bsz, self.n_lat, self.n_ctx, self.d = bsz, n_lat, n_ctx, d
        self.nt = n_lat + n_ctx
        self.t = bsz * self.nt
        self.tiles_per_sample = self.nt // TOKEN_TILE
        self.lat_tiles = n_lat // TOKEN_TILE
        self.n_tiles = self.t // TOKEN_TILE

    def mod_row(self, tile):
        return jnp.where(tile % self.tiles_per_sample < self.lat_tiles, tile // self.tiles_per_sample, self.bsz)

    def mod_spec(self, layer):
        return pl.BlockSpec((None, None, 6, self.d), lambda t: (layer, self.mod_row(t), 0, 0))

    def tok_spec(self, width):
        return pl.BlockSpec((TOKEN_TILE, width), lambda t: (t, 0))


def _full_spec(shape):
    nd = len(shape)
    return pl.BlockSpec(shape, lambda *_: (0,) * nd)


def _mod_kernel(cc_ref, w_ref, b_ref, o_ref):
    o_ref[...] = _dot3(_silu(cc_ref[...]), w_ref[...]) + b_ref[...]


def _mod_call(cc, w_mod, b_mod):
    depth, d, n = w_mod.shape
    tn = n // 4
    return pl.pallas_call(
        _mod_kernel,
        grid=(depth, n // tn),
        in_specs=[pl.BlockSpec((MOD_ROWS, d), lambda l, j: (0, 0)),
                  pl.BlockSpec((None, d, tn), lambda l, j: (l, 0, j)),
                  pl.BlockSpec((None, 1, tn), lambda l, j: (l, 0, j))],
        out_specs=pl.BlockSpec((None, MOD_ROWS, tn), lambda l, j: (l, 0, j)),
        out_shape=jax.ShapeDtypeStruct((depth, MOD_ROWS, n), F32),
        compiler_params=_params("arbitrary", "arbitrary"),
        name="mod",
    )(cc, w_mod, b_mod.reshape(depth, 1, n))


def _modulate_kernel(x_ref, mod_ref, h_ref):
    h_ref[...] = (x_ref[...] * (1.0 + mod_ref[1:2, :]) + mod_ref[0:1, :]).astype(BF16)


def _modulate_call(st, x, mod, layer):
    return pl.pallas_call(
        _modulate_kernel,
        grid=(st.n_tiles,),
        in_specs=[st.tok_spec(st.d), st.mod_spec(layer)],
        out_specs=st.tok_spec(st.d),
        out_shape=jax.ShapeDtypeStruct((st.t, st.d), BF16),
        compiler_params=_params("arbitrary"),
        name="modulate",
    )(x, mod)


def _rope_tables(n_lat, n_ctx):
    half = HEAD_DIM // 2
    nf = half // 2
    t = np.arange(n_lat)
    inv = ROPE_THETA ** (-(2.0 / half) * np.arange(nf, dtype=np.float64))
    ang_r = (t // GRID_W)[:, None] * inv
    ang_c = (t % GRID_W)[:, None] * inv
    zeros = np.zeros_like(ang_r)
    cos = np.concatenate([np.cos(ang_r), np.cos(ang_r), np.cos(ang_c), np.cos(ang_c)], axis=1)
    sin_fwd = np.concatenate([-np.sin(ang_r), zeros, -np.sin(ang_c), zeros], axis=1)
    sin_bwd = np.concatenate([zeros, np.sin(ang_r), zeros, np.sin(ang_c)], axis=1)
    pad = lambda a, v: np.concatenate([a, np.full((n_ctx, HEAD_DIM), v)], axis=0).astype(np.float32)
    return jnp.asarray(pad(cos, 1.0)), jnp.asarray(pad(sin_fwd, 0.0)), jnp.asarray(pad(sin_bwd, 0.0))


def _ab_in_kernel(h_ref, w_ref, qg_ref, kg_ref, cos_ref, sf_ref, sb_ref, f_ref, q_ref, k_ref, v_ref):
    acc = _bdot(h_ref[...], w_ref[...])
    cos, sf, sb = cos_ref[...], sf_ref[...], sb_ref[...]
    nf = HEAD_DIM // 4

    def norm_rope(xh, gain):
        ms = jnp.mean(xh * xh, axis=-1, keepdims=True)
        y = xh * lax.rsqrt(ms + RMS_EPS) * gain
        return y * cos + pltpu.roll(y, HEAD_DIM - nf, 1) * sf + pltpu.roll(y, nf, 1) * sb

    f_ref[...] = acc[:, :FNET_WIDTH]
    q0 = FNET_WIDTH
    k0 = q0 + GQA_Q_HEADS * HEAD_DIM
    v0 = k0 + GQA_KV_HEADS * HEAD_DIM
    for h in range(GQA_Q_HEADS):
        xh = acc[:, q0 + h * HEAD_DIM:q0 + (h + 1) * HEAD_DIM]
        q_ref[:, h * HEAD_DIM:(h + 1) * HEAD_DIM] = (norm_rope(xh, qg_ref[...]) * ATTN_SCALE).astype(BF16)
    for h in range(GQA_KV_HEADS):
        xh = acc[:, k0 + h * HEAD_DIM:k0 + (h + 1) * HEAD_DIM]
        k_ref[:, h * HEAD_DIM:(h + 1) * HEAD_DIM] = norm_rope(xh, kg_ref[...]).astype(BF16)
    v_ref[...] = acc[:, v0:].astype(BF16)


def _ab_in_call(st, h, w_in, q_gain, k_gain, rope):
    nq = GQA_Q_HEADS * HEAD_DIM
    nkv = GQA_KV_HEADS * HEAD_DIM
    pos_spec = pl.BlockSpec((TOKEN_TILE, HEAD_DIM), lambda t: (t % st.tiles_per_sample, 0))
    return pl.pallas_call(
        _ab_in_kernel,
        grid=(st.n_tiles,),
        in_specs=[st.tok_spec(st.d), _full_spec(w_in.shape), _full_spec((1, HEAD_DIM)), _full_spec((1, HEAD_DIM)),
                  pos_spec, pos_spec, pos_spec],
        out_specs=[st.tok_spec(FNET_WIDTH), st.tok_spec(nq), st.tok_spec(nkv), st.tok_spec(nkv)],
        out_shape=[jax.ShapeDtypeStruct((st.t, FNET_WIDTH), F32), jax.ShapeDtypeStruct((st.t, nq), BF16),
                   jax.ShapeDtypeStruct((st.t, nkv), BF16), jax.ShapeDtypeStruct((st.t, nkv), BF16)],
        compiler_params=_params("arbitrary"),
        name="ab_in",
    )(h, w_in, q_gain.reshape(1, HEAD_DIM), k_gain.reshape(1, HEAD_DIM), *rope)


def _attend(q, keys_values, biases):
    scores = []
    for (k, _), bias in zip(keys_values, biases):
        s = _bdot_t(q, k)
        scores.append(s if bias is None else s + bias)
    m = scores[0].max(axis=-1, keepdims=True)
    for s in scores[1:]:
        m = jnp.maximum(m, s.max(axis=-1, keepdims=True))
    num = None
    den = None
    for s, (_, v) in zip(scores, keys_values):
        p = jnp.exp(s - m)
        pv = _bdot(p.astype(BF16), v)
        ps = p.sum(axis=-1, keepdims=True)
        num = pv if num is None else num + pv
        den = ps if den is None else den + ps
    return num / den


def _gqa_kernel(q_ref, k_ref, v_ref, o_ref, *, n_lat, lat_tiles):
    def run(k, v):
        for h in range(GQA_GROUP):
            sl = slice(h * HEAD_DIM, (h + 1) * HEAD_DIM)
            o_ref[:, sl] = _attend(q_ref[:, sl], [(k, v)], [None]).astype(BF16)

    is_lat = pl.program_id(2) < lat_tiles

    @pl.when(is_lat)
    def _():
        run(k_ref[...], v_ref[...])

    @pl.when(jnp.logical_not(is_lat))
    def _():
        run(k_ref[n_lat:, :], v_ref[n_lat:, :])


def _gqa_call(st, q, k, v):
    gw = GQA_GROUP * HEAD_DIM
    q3 = q.reshape(st.bsz, st.nt, GQA_Q_HEADS * HEAD_DIM)
    k3 = k.reshape(st.bsz, st.nt, GQA_KV_HEADS * HEAD_DIM)
    v3 = v.reshape(st.bsz, st.nt, GQA_KV_HEADS * HEAD_DIM)
    q_spec = pl.BlockSpec((None, TOKEN_TILE, gw), lambda b, g, i: (b, i, g))
    kv_spec = pl.BlockSpec((None, st.nt, HEAD_DIM), lambda b, g, i: (b, 0, g))
    o = pl.pallas_call(
        functools.partial(_gqa_kernel, n_lat=st.n_lat, lat_tiles=st.lat_tiles),
        grid=(st.bsz, GQA_KV_HEADS, st.tiles_per_sample),
        in_specs=[q_spec, kv_spec, kv_spec],
        out_specs=q_spec,
        out_shape=jax.ShapeDtypeStruct(q3.shape, BF16),
        compiler_params=_params("arbitrary", "arbitrary", "arbitrary"),
        name="gqa",
    )(q3, k3, v3)
    return o.reshape(st.t, GQA_Q_HEADS * HEAD_DIM)


def _fft_split(n):
    l1 = 1 << ((n.bit_length() - 1 + 1) // 2)
    assert n % l1 == 0 and n == l1 * (n // l1)
    return l1, n // l1


def _fft_tables(n):
    l1, l2 = _fft_split(n)
    a = np.arange(l1, dtype=np.float64)
    ang1 = 2.0 * np.pi * np.outer(a, a) / l1
    stage1 = np.concatenate([np.cos(ang1), -np.sin(ang1)], axis=0)
    b = np.arange(l2, dtype=np.float64)
    ang_t = 2.0 * np.pi * np.outer(b, a) / n
    tw_cos = np.cos(ang_t)[:, :, None]
    tw_sin = np.sin(ang_t)[:, :, None]
    ang2 = 2.0 * np.pi * np.outer(b, b) / l2
    c2, s2 = np.cos(ang2), np.sin(ang2)
    stage2 = np.block([[c2, s2], [-s2, c2]])
    f32 = lambda x: jnp.asarray(x.astype(np.float32))
    return f32(stage1), f32(tw_cos), f32(tw_sin), f32(stage2)


FNET_GROUPS_PER_SLAB = V7X_LANES // FNET_GROUP_DIM
FNET_SLABS = FNET_WIDTH // V7X_LANES


def _fnet_channel_tables(n_positions):
    c = np.arange(FNET_GROUP_DIM, dtype=np.float64)
    ang = 2.0 * np.pi * np.outer(c, c) / FNET_GROUP_DIM
    eye = np.eye(FNET_GROUPS_PER_SLAB)
    scale = 1.0 / math.sqrt(n_positions * FNET_GROUP_DIM)
    m = np.concatenate([np.kron(eye, np.cos(ang)), np.kron(eye, np.sin(ang))], axis=0) * scale
    return jnp.asarray(m.astype(np.float32))


def _fnet_part(f_ref, o_ref, a_ref, row0, n, s1_ref, tc_ref, ts_ref, s2_ref, ch_ref, wf_ref):
    l1, l2 = _fft_split(n)
    stage1 = s1_ref[...]
    stage2 = s2_ref[...]
    chan = ch_ref[...]
    wf = wf_ref[...]

    def first(j, carry):
        xs = f_ref[pl.ds(row0 + j, l1, stride=l2), :]
        a = _dot3(stage1, xs)
        ar, ai = a[:l1], a[l1:]
        tc, ts = tc_ref[j], ts_ref[j]
        a_ref[0, pl.ds(pl.multiple_of(j * l1, l1), l1), :] = ar * tc + ai * ts
        a_ref[1, pl.ds(pl.multiple_of(j * l1, l1), l1), :] = ai * tc - ar * ts
        return carry

    lax.fori_loop(0, l2, first, 0)

    def second(j, carry):
        br = a_ref[0, pl.ds(j, l2, stride=l1), :]
        bi = a_ref[1, pl.ds(j, l2, stride=l1), :]
        p = _dot3(stage2, jnp.concatenate([br, bi], axis=0))
        re = _dot3(jnp.concatenate([p[:l2], p[l2:]], axis=1), chan)
        o_ref[pl.ds(row0 + j, l2, stride=l1), :] = _bdot(re.astype(BF16), wf)
        return carry

    lax.fori_loop(0, l1, second, 0)


def _fnet_kernel(f_ref, s1l, tcl, tsl, s2l, chl, s1c, tcc, tsc, s2c, chc, wf_ref, o_ref, a_ref, *, n_lat, n_ctx):
    _fnet_part(f_ref, o_ref, a_ref, 0, n_lat, s1l, tcl, tsl, s2l, chl, wf_ref)
    _fnet_part(f_ref, o_ref, a_ref, n_lat, n_ctx, s1c, tcc, tsc, s2c, chc, wf_ref)


def _fnet_call(st, f, w_fnet):
    gps = FNET_GROUPS_PER_SLAB
    eye = jnp.eye(gps, dtype=F32)
    wg = w_fnet.reshape(FNET_SLABS, gps, FNET_GROUP_DIM, FNET_GROUP_DIM)
    wf = (eye[None, :, None, :, None] * wg[:, :, :, None, :]).reshape(FNET_SLABS, V7X_LANES, V7X_LANES).astype(BF16)
    consts = (*_fft_tables(st.n_lat), _fnet_channel_tables(st.n_lat),
              *_fft_tables(st.n_ctx), _fnet_channel_tables(st.n_ctx))
    f3 = f.reshape(st.bsz, st.nt, FNET_WIDTH)
    blk = pl.BlockSpec((None, st.nt, V7X_LANES), lambda b, s: (b, 0, s))
    o = pl.pallas_call(
        functools.partial(_fnet_kernel, n_lat=st.n_lat, n_ctx=st.n_ctx),
        grid=(st.bsz, FNET_SLABS),
        in_specs=[blk] + [_full_spec(c.shape) for c in consts]
        + [pl.BlockSpec((None, V7X_LANES, V7X_LANES), lambda b, s: (s, 0, 0))],
        out_specs=blk,
        out_shape=jax.ShapeDtypeStruct(f3.shape, F32),
        scratch_shapes=[pltpu.VMEM((2, st.n_lat, V7X_LANES), F32)],
        compiler_params=_params("arbitrary", "arbitrary"),
        name="fnet",
    )(f3, *consts, wf)
    return o.reshape(st.t, FNET_WIDTH)


def _route(h2, wr, e_bias):
    scores = jax.nn.sigmoid(_dot3(h2, wr))
    sel = scores + e_bias
    lane = lax.broadcasted_iota(jnp.int32, sel.shape, 1)
    chosen = jnp.zeros(sel.shape, jnp.bool_)
    for _ in range(TOP_K):
        best = sel.max(axis=-1, keepdims=True)
        first = jnp.where(sel == best, lane, N_EXPERTS).min(axis=-1, keepdims=True)
        hit = lane == first
        chosen = jnp.logical_or(chosen, hit)
        sel = jnp.where(hit, -jnp.inf, sel)
    w = jnp.where(chosen, scores, 0.0)
    return w / w.sum(axis=-1, keepdims=True) * ROUTE_SCALE


def _out_ln_kernel(*refs, n_in, alpha):
    a_refs = refs[:n_in]
    w_refs = refs[n_in:2 * n_in]
    x_ref, mod_ref, g_ref, b_ref, wr_ref, eb_ref, x1_ref, h2_ref, cw_ref = refs[2 * n_in:]
    y = None
    for a_ref, w_ref in zip(a_refs, w_refs):
        part = _bdot(a_ref[...].astype(BF16), w_ref[...])
        y = part if y is None else y + part
    z = alpha * x_ref[...] + mod_ref[2:3, :] * y
    x1 = _layer_norm(z, g_ref[...], b_ref[...])
    x1_ref[...] = x1
    h2 = x1 * (1.0 + mod_ref[4:5, :]) + mod_ref[3:4, :]
    h2_ref[...] = h2.astype(BF16)
    cw_ref[...] = _route(h2, wr_ref[...], eb_ref[...])


def _out_ln_call(st, acts, w_out, x, mod, layer, ln_g, ln_b, w_router, e_bias, alpha):
    ws, r0 = [], 0
    for a in acts:
        ws.append(w_out[r0:r0 + a.shape[1]].astype(BF16))
        r0 += a.shape[1]
    assert r0 == w_out.shape[0]
    row = lambda v: v.reshape(1, -1)
    return pl.pallas_call(
        functools.partial(_out_ln_kernel, n_in=len(acts), alpha=alpha),
        grid=(st.n_tiles,),
        in_specs=[st.tok_spec(a.shape[1]) for a in acts] + [_full_spec(w.shape) for w in ws]
        + [st.tok_spec(st.d), st.mod_spec(layer), _full_spec((1, st.d)), _full_spec((1, st.d)),
           _full_spec(w_router.shape), _full_spec((1, N_EXPERTS))],
        out_specs=[st.tok_spec(st.d), st.tok_spec(st.d), st.tok_spec(N_EXPERTS)],
        out_shape=[jax.ShapeDtypeStruct((st.t, st.d), F32), jax.ShapeDtypeStruct((st.t, st.d), BF16),
                   jax.ShapeDtypeStruct((st.t, N_EXPERTS), F32)],
        compiler_params=_params("arbitrary"),
        name="out_ln",
    )(*acts, *ws, x, mod, row(ln_g), row(ln_b), w_router, row(e_bias))


MOE_TOKEN_TILE = 1024


def _moe_dense_kernel(h_ref, cw_ref, wg_ref, wu_ref, wd_ref, o_ref):
    e = pl.program_id(1)

    @pl.when(e == 0)
    def _():
        o_ref[...] = jnp.zeros_like(o_ref)

    h = h_ref[...]
    a = _silu(_bdot(h, wg_ref[...].astype(BF16))) * _bdot(h, wu_ref[...].astype(BF16))
    cw = cw_ref[...]
    lane = lax.broadcasted_iota(jnp.int32, cw.shape, 1)
    w = jnp.where(lane == e, cw, 0.0).sum(axis=-1, keepdims=True)
    o_ref[...] += _bdot(a.astype(BF16), wd_ref[...].astype(BF16)) * w


def _moe_dense_call(st, h2, cw, w_gate, w_up, w_down, layer):
    tm = MOE_TOKEN_TILE
    assert st.t % tm == 0
    return pl.pallas_call(
        _moe_dense_kernel,
        grid=(st.t // tm, N_EXPERTS),
        in_specs=[pl.BlockSpec((tm, st.d), lambda t, e: (t, 0)),
                  pl.BlockSpec((tm, N_EXPERTS), lambda t, e: (t, 0)),
                  pl.BlockSpec((None, None, st.d, EXPERT_DIM), lambda t, e: (layer, e, 0, 0)),
                  pl.BlockSpec((None, None, st.d, EXPERT_DIM), lambda t, e: (layer, e, 0, 0)),
                  pl.BlockSpec((None, None, EXPERT_DIM, st.d), lambda t, e: (layer, e, 0, 0))],
        out_specs=pl.BlockSpec((tm, st.d), lambda t, e: (t, 0)),
        out_shape=jax.ShapeDtypeStruct((st.t, st.d), F32),
        compiler_params=_params("arbitrary", "arbitrary"),
        name="moe_dense",
    )(h2, cw, w_gate, w_up, w_down)


def _ffn_ln_kernel(r_ref, h2_ref, x1_ref, sg_ref, su_ref, sd_ref, mod_ref, modn_ref, g_ref, b_ref, x2_ref, hn_ref,
                   *, alpha):
    h2 = h2_ref[...]
    a = _silu(_bdot(h2, sg_ref[...])) * _bdot(h2, su_ref[...])
    ff = r_ref[...] + _bdot(a.astype(BF16), sd_ref[...])
    z = alpha * x1_ref[...] + mod_ref[5:6, :] * ff
    x2 = _layer_norm(z, g_ref[...], b_ref[...])
    x2_ref[...] = x2
    hn_ref[...] = (x2 * (1.0 + modn_ref[1:2, :]) + modn_ref[0:1, :]).astype(BF16)


def _ffn_ln_call(st, routed, h2, x1, s_gate, s_up, s_down, mod, layer, next_layer, ln_g, ln_b, alpha):
    row = lambda v: v.reshape(1, -1)
    sg, su, sd = s_gate.astype(BF16), s_up.astype(BF16), s_down.astype(BF16)
    return pl.pallas_call(
        functools.partial(_ffn_ln_kernel, alpha=alpha),
        grid=(st.n_tiles,),
        in_specs=[st.tok_spec(st.d), st.tok_spec(st.d), st.tok_spec(st.d),
                  _full_spec(sg.shape), _full_spec(su.shape), _full_spec(sd.shape),
                  st.mod_spec(layer), st.mod_spec(next_layer), _full_spec((1, st.d)), _full_spec((1, st.d))],
        out_specs=[st.tok_spec(st.d), st.tok_spec(st.d)],
        out_shape=[jax.ShapeDtypeStruct((st.t, st.d), F32), jax.ShapeDtypeStruct((st.t, st.d), BF16)],
        compiler_params=_params("arbitrary"),
        name="ffn_ln",
    )(routed, h2, x1, sg, su, sd, mod, mod, row(ln_g), row(ln_b))


def _na_in_kernel(h_ref, w_ref, o_ref):
    scale = jnp.where(pl.program_id(1) == 0, ATTN_SCALE, 1.0)
    o_ref[...] = (_bdot(h_ref[...], w_ref[...]) * scale).astype(BF16)


def _na_in_call(st, h, w_in):
    return pl.pallas_call(
        _na_in_kernel,
        grid=(st.n_tiles, 3),
        in_specs=[pl.BlockSpec((TOKEN_TILE, st.d), lambda t, j: (t, 0)),
                  pl.BlockSpec((st.d, NA_WIDTH), lambda t, j: (0, j))],
        out_specs=pl.BlockSpec((TOKEN_TILE, NA_WIDTH), lambda t, j: (t, j)),
        out_shape=jax.ShapeDtypeStruct((st.t, 3 * NA_WIDTH), BF16),
        compiler_params=_params("arbitrary", "arbitrary"),
        name="na_in",
    )(h, w_in)


def _na_geometry(n_lat):
    rows = n_lat // GRID_W
    kh, kw, qr = min(NA_KH, rows), min(NA_KW, GRID_W), NA_Q_ROWS
    nbr = min(qr + kh - 1, rows)
    col = np.arange(GRID_W)
    col_start = np.clip(col - kw // 2, 0, GRID_W - kw)
    in_col = (col[None, :] >= col_start[:, None]) & (col[None, :] < col_start[:, None] + kw)
    dc = np.clip(col[None, :] - col[:, None] + NA_KW - 1, 0, 2 * NA_KW - 2)
    starts, variant_of, variants = [], [], {}
    for i in range(rows // qr):
        qrow = i * qr + np.arange(qr)
        rstart = np.clip(qrow - kh // 2, 0, rows - kh)
        bs = min(int(rstart[0]), rows - nbr)
        krow = bs + np.arange(nbr)
        in_row = (krow[None, :] >= rstart[:, None]) & (krow[None, :] < rstart[:, None] + kh)
        dr = np.clip(krow[None, :] - qrow[:, None] + NA_KH - 1, 0, 2 * NA_KH - 2)
        key = (in_row.tobytes(), dr.tobytes())
        if key not in variants:
            mask = (in_row[:, None, :, None] & in_col[None, :, None, :]).reshape(qr * GRID_W, nbr * GRID_W)
            variants[key] = (len(variants), dr, mask)
        starts.append(bs)
        variant_of.append(variants[key][0])
    ordered = sorted(variants.values(), key=lambda v: v[0])
    return nbr, np.asarray(starts, np.int32), np.asarray(variant_of, np.int32), [(v[1], v[2]) for v in ordered], dc


def _na_bias_tables(rpb, n_lat):
    nbr, starts, variant_of, variants, dc = _na_geometry(n_lat)
    n_dr, n_dc = 2 * NA_KH - 1, 2 * NA_KW - 1
    col_sel = jnp.asarray((dc[:, :, None] == np.arange(n_dc)).astype(np.float32))
    by_col = jnp.einsum("hrc,wuc->hrwu", rpb, col_sel, precision=lax.Precision.HIGHEST)
    tables = []
    for dr, mask in variants:
        row_sel = jnp.asarray((dr[:, :, None] == np.arange(n_dr)).astype(np.float32))
        bias = jnp.einsum("ajr,hrwu->hawju", row_sel, by_col, precision=lax.Precision.HIGHEST)
        bias = bias.reshape(NA_HEADS, NA_Q_ROWS * GRID_W, nbr * GRID_W)
        tables.append(jnp.where(jnp.asarray(mask)[None], bias, NEG_INF))
    return nbr, starts, variant_of, jnp.stack(tables)


NA_Q_TILE = NA_Q_ROWS * GRID_W


def _na_kernel(start_ref, var_ref, q_ref, k_ref, v_ref, bias_ref, o_ref, *, n_lat, band):
    i = pl.program_id(1)
    is_lat = i < n_lat // NA_Q_TILE

    @pl.when(is_lat)
    def _():
        off = pl.multiple_of(start_ref[i] * GRID_W, GRID_W)
        for h in range(NA_HEADS):
            sl = slice(h * HEAD_DIM, (h + 1) * HEAD_DIM)
            local = (k_ref[pl.ds(off, band), sl], v_ref[pl.ds(off, band), sl])
            ctx = (k_ref[n_lat:, sl], v_ref[n_lat:, sl])
            o_ref[:, sl] = _attend(q_ref[:, sl], [local, ctx], [bias_ref[h], None]).astype(BF16)

    @pl.when(jnp.logical_not(is_lat))
    def _():
        for h in range(NA_HEADS):
            sl = slice(h * HEAD_DIM, (h + 1) * HEAD_DIM)
            o_ref[:, sl] = _attend(q_ref[:, sl], [(k_ref[n_lat:, sl], v_ref[n_lat:, sl])], [None]).astype(BF16)


def _na_call(st, qkv, rpb):
    nbr, starts, variant_of, bias = _na_bias_tables(rpb, st.n_lat)
    band = nbr * GRID_W
    n_q = st.nt // NA_Q_TILE
    pad = n_q - starts.shape[0]
    starts = jnp.asarray(np.concatenate([starts, np.zeros(pad, np.int32)]))
    variant_of = jnp.asarray(np.concatenate([variant_of, np.zeros(pad, np.int32)]))
    qkv3 = qkv.reshape(st.bsz, st.nt, 3 * NA_WIDTH)
    grid_spec = pltpu.PrefetchScalarGridSpec(
        num_scalar_prefetch=2,
        grid=(st.bsz, n_q),
        in_specs=[pl.BlockSpec((None, NA_Q_TILE, NA_WIDTH), lambda b, i, s, v: (b, i, 0)),
                  pl.BlockSpec((None, st.nt, NA_WIDTH), lambda b, i, s, v: (b, 0, 1)),
                  pl.BlockSpec((None, st.nt, NA_WIDTH), lambda b, i, s, v: (b, 0, 2)),
                  pl.BlockSpec((None, NA_HEADS, NA_Q_TILE, band), lambda b, i, s, v: (v[i], 0, 0, 0))],
        out_specs=pl.BlockSpec((None, NA_Q_TILE, NA_WIDTH), lambda b, i, s, v: (b, i, 0)),
    )
    o = pl.pallas_call(
        functools.partial(_na_kernel, n_lat=st.n_lat, band=band),
        grid_spec=grid_spec,
        out_shape=jax.ShapeDtypeStruct((st.bsz, st.nt, NA_WIDTH), BF16),
        compiler_params=_params("arbitrary", "arbitrary"),
        name="na_attn",
    )(starts, variant_of, qkv3, qkv3, qkv3, bias)
    return o.reshape(st.t, NA_WIDTH)


def kernel(x, c, ctx, c_ctx, w_mod, b_mod, ln1_g, ln1_b, ln2_g, ln2_b, ab_w_in, ab_w_fnet, ab_q_norm, ab_k_norm,
           ab_w_out, na_w_in, na_rpb, na_w_out, moe_w_router, moe_bias, moe_w_gate, moe_w_up, moe_w_down,
           sh_w_gate, sh_w_up, sh_w_down):
    bsz, n_lat, d = x.shape
    n_ctx = ctx.shape[1]
    depth = w_mod.shape[0]
    st = _Stream(bsz, n_lat, n_ctx, d)
    alpha = (2 * depth) ** 0.25

    cc = jnp.concatenate([c, c_ctx[None, :], jnp.zeros((MOD_ROWS - bsz - 1, d), F32)], axis=0)
    mod = _mod_call(cc, w_mod, b_mod).reshape(depth, MOD_ROWS, 6, d)
    xs = jnp.concatenate([x, ctx], axis=1).reshape(st.t, d)
    h = _modulate_call(st, xs, mod, 0)
    rope = _rope_tables(n_lat, n_ctx)

    for l in range(depth):
        j = l // 2
        if l % 2 == 0:
            f, q, k, v = _ab_in_call(st, h, ab_w_in[j].astype(BF16), ab_q_norm[j], ab_k_norm[j], rope)
            acts = [_fnet_call(st, f, ab_w_fnet[j]), _gqa_call(st, q, k, v)]
            w_out = ab_w_out[j]
        else:
            qkv = _na_in_call(st, h, na_w_in[j].astype(BF16))
            acts = [_na_call(st, qkv, na_rpb[j])]
            w_out = na_w_out[j]
        x1, h2, cw = _out_ln_call(st, acts, w_out, xs, mod, l, ln1_g[l], ln1_b[l], moe_w_router[l], moe_bias[l], alpha)
        routed = _moe_dense_call(st, h2, cw, moe_w_gate, moe_w_up, moe_w_down, l)
        xs, h = _ffn_ln_call(st, routed, h2, x1, sh_w_gate[l], sh_w_up[l], sh_w_down[l], mod, l,
                             min(l + 1, depth - 1), ln2_g[l], ln2_b[l], alpha)
    return xs.reshape(bsz, st.nt, d)[:, :n_lat]
```

```python
import functools
import math

import numpy as np
import jax
import jax.numpy as jnp
from jax import lax
from jax.experimental import pallas as pl
from jax.experimental.pallas import tpu as pltpu

GRID_W = 64
HEAD_DIM = 128
FNET_GROUPS = 4
FNET_GROUP_DIM = 64
FNET_WIDTH = FNET_GROUPS * FNET_GROUP_DIM
GQA_Q_HEADS = 6
GQA_KV_HEADS = 2
GQA_GROUP = GQA_Q_HEADS // GQA_KV_HEADS
ROPE_THETA = 10000.0
NA_HEADS = 8
NA_WIDTH = NA_HEADS * HEAD_DIM
NA_KH = 8
NA_KW = 16
NA_Q_ROWS = 2
NEG_INF = -1e30
N_EXPERTS = 64
TOP_K = 8
EXPERT_DIM = 256
ROUTE_SCALE = 2.5
LN_EPS = 1e-6
RMS_EPS = 1e-6
ATTN_SCALE = HEAD_DIM ** -0.5

V7X_LANES = 128
V7X_SUBLANES = 8
V7X_VMEM_LIMIT_BYTES = 56 * 1024 * 1024

TOKEN_TILE = 256
MOD_ROWS = 8

F32 = jnp.float32
BF16 = jnp.bfloat16


def _params(*sem):
    return pltpu.CompilerParams(dimension_semantics=sem, vmem_limit_bytes=V7X_VMEM_LIMIT_BYTES)


def _bdot(a, b):
    return jnp.dot(a, b, preferred_element_type=F32)


def _bdot_t(a, b):
    return lax.dot_general(a, b, (((1,), (1,)), ((), ())), preferred_element_type=F32)


def _split(x):
    hi = x.astype(BF16)
    lo = (x - hi.astype(F32)).astype(BF16)
    return hi, lo


def _dot3(a, b):
    ah, al = _split(a)
    bh, bl = _split(b)
    return _bdot(ah, bh) + (_bdot(ah, bl) + _bdot(al, bh))


def _silu(x):
    return x * jax.nn.sigmoid(x)


def _layer_norm(z, g, b):
    mu = jnp.mean(z, axis=-1, keepdims=True)
    zc = z - mu
    var = jnp.mean(zc * zc, axis=-1, keepdims=True)
    return zc * lax.rsqrt(var + LN_EPS) * g + b


class _Stream:
    def __init__(self, bsz, n_lat, n_ctx, d):
        assert n_lat % TOKEN_TILE == 0 and n_ctx % TOKEN_TILE == 0
        assert bsz < MOD_ROWS
        self.bsz, self.n_lat, self.n_ctx, self.d = bsz, n_lat, n_ctx, d
        self.nt = n_lat + n_ctx
        self.t = bsz * self.nt
        self.tiles_per_sample = self.nt // TOKEN_TILE
        self.lat_tiles = n_lat // TOKEN_TILE
        self.n_tiles = self.t // TOKEN_TILE

    def mod_row(self, tile):
        return jnp.where(tile % self.tiles_per_sample < self.lat_tiles, tile // self.tiles_per_sample, self.bsz)

    def mod_spec(self, layer):
        return pl.BlockSpec((None, None, 6, self.d), lambda t: (layer, self.mod_row(t), 0, 0))

    def tok_spec(self, width):
        return pl.BlockSpec((TOKEN_TILE, width), lambda t: (t, 0))


def _full_spec(shape):
    nd = len(shape)
    return pl.BlockSpec(shape, lambda *_: (0,) * nd)


def _mod_kernel(cc_ref, w_ref, b_ref, o_ref):
    o_ref[...] = _dot3(_silu(cc_ref[...]), w_ref[...]) + b_ref[...]


def _mod_call(cc, w_mod, b_mod):
    depth, d, n = w_mod.shape
    tn = n // 4
    return pl.pallas_call(
        _mod_kernel,
        grid=(depth, n // tn),
        in_specs=[pl.BlockSpec((MOD_ROWS, d), lambda l, j: (0, 0)),
                  pl.BlockSpec((None, d, tn), lambda l, j: (l, 0, j)),
                  pl.BlockSpec((None, 1, tn), lambda l, j: (l, 0, j))],
        out_specs=pl.BlockSpec((None, MOD_ROWS, tn), lambda l, j: (l, 0, j)),
        out_shape=jax.ShapeDtypeStruct((depth, MOD_ROWS, n), F32),
        compiler_params=_params("arbitrary", "arbitrary"),
        name="mod",
    )(cc, w_mod, b_mod.reshape(depth, 1, n))


def _modulate_kernel(x_ref, mod_ref, h_ref):
    h_ref[...] = (x_ref[...] * (1.0 + mod_ref[1:2, :]) + mod_ref[0:1, :]).astype(BF16)


def _modulate_call(st, x, mod, layer):
    return pl.pallas_call(
        _modulate_kernel,
        grid=(st.n_tiles,),
        in_specs=[st.tok_spec(st.d), st.mod_spec(layer)],
        out_specs=st.tok_spec(st.d),
        out_shape=jax.ShapeDtypeStruct((st.t, st.d), BF16),
        compiler_params=_params("arbitrary"),
        name="modulate",
    )(x, mod)


def _rope_tables(n_lat, n_ctx):
    half = HEAD_DIM // 2
    nf = half // 2
    t = np.arange(n_lat)
    inv = ROPE_THETA ** (-(2.0 / half) * np.arange(nf, dtype=np.float64))
    ang_r = (t // GRID_W)[:, None] * inv
    ang_c = (t % GRID_W)[:, None] * inv
    zeros = np.zeros_like(ang_r)
    cos = np.concatenate([np.cos(ang_r), np.cos(ang_r), np.cos(ang_c), np.cos(ang_c)], axis=1)
    sin_fwd = np.concatenate([-np.sin(ang_r), zeros, -np.sin(ang_c), zeros], axis=1)
    sin_bwd = np.concatenate([zeros, np.sin(ang_r), zeros, np.sin(ang_c)], axis=1)
    pad = lambda a, v: np.concatenate([a, np.full((n_ctx, HEAD_DIM), v)], axis=0).astype(np.float32)
    return jnp.asarray(pad(cos, 1.0)), jnp.asarray(pad(sin_fwd, 0.0)), jnp.asarray(pad(sin_bwd, 0.0))


def _ab_in_kernel(h_ref, w_ref, qg_ref, kg_ref, cos_ref, sf_ref, sb_ref, f_ref, q_ref, k_ref, v_ref):
    acc = _bdot(h_ref[...], w_ref[...])
    cos, sf, sb = cos_ref[...], sf_ref[...], sb_ref[...]
    nf = HEAD_DIM // 4

    def norm_rope(xh, gain):
        ms = jnp.mean(xh * xh, axis=-1, keepdims=True)
        y = xh * lax.rsqrt(ms + RMS_EPS) * gain
        return y * cos + pltpu.roll(y, HEAD_DIM - nf, 1) * sf + pltpu.roll(y, nf, 1) * sb

    f_ref[...] = acc[:, :FNET_WIDTH]
    q0 = FNET_WIDTH
    k0 = q0 + GQA_Q_HEADS * HEAD_DIM
    v0 = k0 + GQA_KV_HEADS * HEAD_DIM
    for h in range(GQA_Q_HEADS):
        xh = acc[:, q0 + h * HEAD_DIM:q0 + (h + 1) * HEAD_DIM]
        q_ref[:, h * HEAD_DIM:(h + 1) * HEAD_DIM] = (norm_rope(xh, qg_ref[...]) * ATTN_SCALE).astype(BF16)
    for h in range(GQA_KV_HEADS):
        xh = acc[:, k0 + h * HEAD_DIM:k0 + (h + 1) * HEAD_DIM]
        k_ref[:, h * HEAD_DIM:(h + 1) * HEAD_DIM] = norm_rope(xh, kg_ref[...]).astype(BF16)
    v_ref[...] = acc[:, v0:].astype(BF16)


def _ab_in_call(st, h, w_in, q_gain, k_gain, rope):
    nq = GQA_Q_HEADS * HEAD_DIM
    nkv = GQA_KV_HEADS * HEAD_DIM
    pos_spec = pl.BlockSpec((TOKEN_TILE, HEAD_DIM), lambda t: (t % st.tiles_per_sample, 0))
    return pl.pallas_call(
        _ab_in_kernel,
        grid=(st.n_tiles,),
        in_specs=[st.tok_spec(st.d), _full_spec(w_in.shape), _full_spec((1, HEAD_DIM)), _full_spec((1, HEAD_DIM)),
                  pos_spec, pos_spec, pos_spec],
        out_specs=[st.tok_spec(FNET_WIDTH), st.tok_spec(nq), st.tok_spec(nkv), st.tok_spec(nkv)],
        out_shape=[jax.ShapeDtypeStruct((st.t, FNET_WIDTH), F32), jax.ShapeDtypeStruct((st.t, nq), BF16),
                   jax.ShapeDtypeStruct((st.t, nkv), BF16), jax.ShapeDtypeStruct((st.t, nkv), BF16)],
        compiler_params=_params("arbitrary"),
        name="ab_in",
    )(h, w_in, q_gain.reshape(1, HEAD_DIM), k_gain.reshape(1, HEAD_DIM), *rope)


def _attend(q, keys_values, biases):
    scores = []
    for (k, _), bias in zip(keys_values, biases):
        s = _bdot_t(q, k)
        scores.append(s if bias is None else s + bias)
    m = scores[0].max(axis=-1, keepdims=True)
    for s in scores[1:]:
        m = jnp.maximum(m, s.max(axis=-1, keepdims=True))
    num = None
    den = None
    for s, (_, v) in zip(scores, keys_values):
        p = jnp.exp(s - m)
        pv = _bdot(p.astype(BF16), v)
        ps = p.sum(axis=-1, keepdims=True)
        num = pv if num is None else num + pv
        den = ps if den is None else den + ps
    return num / den


def _gqa_kernel(q_ref, k_ref, v_ref, o_ref, *, n_lat, lat_tiles):
    def run(k, v):
        for h in range(GQA_GROUP):
            sl = slice(h * HEAD_DIM, (h + 1) * HEAD_DIM)
            o_ref[:, sl] = _attend(q_ref[:, sl], [(k, v)], [None]).astype(BF16)

    is_lat = pl.program_id(2) < lat_tiles

    @pl.when(is_lat)
    def _():
        run(k_ref[...], v_ref[...])

    @pl.when(jnp.logical_not(is_lat))
    def _():
        run(k_ref[n_lat:, :], v_ref[n_lat:, :])


def _gqa_call(st, q, k, v):
    gw = GQA_GROUP * HEAD_DIM
    q3 = q.reshape(st.bsz, st.nt, GQA_Q_HEADS * HEAD_DIM)
    k3 = k.reshape(st.bsz, st.nt, GQA_KV_HEADS * HEAD_DIM)
    v3 = v.reshape(st.bsz, st.nt, GQA_KV_HEADS * HEAD_DIM)
    q_spec = pl.BlockSpec((None, TOKEN_TILE, gw), lambda b, g, i: (b, i, g))
    kv_spec = pl.BlockSpec((None, st.nt, HEAD_DIM), lambda b, g, i: (b, 0, g))
    o = pl.pallas_call(
        functools.partial(_gqa_kernel, n_lat=st.n_lat, lat_tiles=st.lat_tiles),
        grid=(st.bsz, GQA_KV_HEADS, st.tiles_per_sample),
        in_specs=[q_spec, kv_spec, kv_spec],
        out_specs=q_spec,
        out_shape=jax.ShapeDtypeStruct(q3.shape, BF16),
        compiler_params=_params("arbitrary", "arbitrary", "arbitrary"),
        name="gqa",
    )(q3, k3, v3)
    return o.reshape(st.t, GQA_Q_HEADS * HEAD_DIM)


def _fft_split(n):
    l1 = 1 << ((n.bit_length() - 1 + 1) // 2)
    assert n % l1 == 0 and n == l1 * (n // l1)
    return l1, n // l1


def _fft_tables(n):
    l1, l2 = _fft_split(n)
    a = np.arange(l1, dtype=np.float64)
    ang1 = 2.0 * np.pi * np.outer(a, a) / l1
    stage1 = np.concatenate([np.cos(ang1), -np.sin(ang1)], axis=0)
    b = np.arange(l2, dtype=np.float64)
    ang_t = 2.0 * np.pi * np.outer(b, a) / n
    tw_cos = np.cos(ang_t)[:, :, None]
    tw_sin = np.sin(ang_t)[:, :, None]
    ang2 = 2.0 * np.pi * np.outer(b, b) / l2
    c2, s2 = np.cos(ang2), np.sin(ang2)
    stage2 = np.block([[c2, s2], [-s2, c2]])
    f32 = lambda x: jnp.asarray(x.astype(np.float32))
    return f32(stage1), f32(tw_cos), f32(tw_sin), f32(stage2)


FNET_GROUPS_PER_SLAB = V7X_LANES // FNET_GROUP_DIM
FNET_SLABS = FNET_WIDTH // V7X_LANES


def _fnet_channel_tables(n_positions):
    c = np.arange(FNET_GROUP_DIM, dtype=np.float64)
    ang = 2.0 * np.pi * np.outer(c, c) / FNET_GROUP_DIM
    eye = np.eye(FNET_GROUPS_PER_SLAB)
    scale = 1.0 / math.sqrt(n_positions * FNET_GROUP_DIM)
    m = np.concatenate([np.kron(eye, np.cos(ang)), np.kron(eye, np.sin(ang))], axis=0) * scale
    return jnp.asarray(m.astype(np.float32))


def _fnet_part(f_ref, o_ref, a_ref, row0, n, s1_ref, tc_ref, ts_ref, s2_ref, ch_ref, wf_ref):
    l1, l2 = _fft_split(n)
    stage1 = s1_ref[...]
    stage2 = s2_ref[...]
    chan = ch_ref[...]
    wf = wf_ref[...]

    def first(j, carry):
        xs = f_ref[pl.ds(row0 + j, l1, stride=l2), :]
        a = _dot3(stage1, xs)
        ar, ai = a[:l1], a[l1:]
        tc, ts = tc_ref[j], ts_ref[j]
        a_ref[0, pl.ds(pl.multiple_of(j * l1, l1), l1), :] = ar * tc + ai * ts
        a_ref[1, pl.ds(pl.multiple_of(j * l1, l1), l1), :] = ai * tc - ar * ts
        return carry

    lax.fori_loop(0, l2, first, 0)

    def second(j, carry):
        br = a_ref[0, pl.ds(j, l2, stride=l1), :]
        bi = a_ref[1, pl.ds(j, l2, stride=l1), :]
        p = _dot3(stage2, jnp.concatenate([br, bi], axis=0))
        re = _dot3(jnp.concatenate([p[:l2], p[l2:]], axis=1), chan)
        o_ref[pl.ds(row0 + j, l2, stride=l1), :] = _bdot(re.astype(BF16), wf)
        return carry

    lax.fori_loop(0, l1, second, 0)


def _fnet_kernel(f_ref, s1l, tcl, tsl, s2l, chl, s1c, tcc, tsc, s2c, chc, wf_ref, o_ref, a_ref, *, n_lat, n_ctx):
    _fnet_part(f_ref, o_ref, a_ref, 0, n_lat, s1l, tcl, tsl, s2l, chl, wf_ref)
    _fnet_part(f_ref, o_ref, a_ref, n_lat, n_ctx, s1c, tcc, tsc, s2c, chc, wf_ref)


def _fnet_call(st, f, w_fnet):
    gps = FNET_GROUPS_PER_SLAB
    eye = jnp.eye(gps, dtype=F32)
    wg = w_fnet.reshape(FNET_SLABS, gps, FNET_GROUP_DIM, FNET_GROUP_DIM)
    wf = (eye[None, :, None, :, None] * wg[:, :, :, None, :]).reshape(FNET_SLABS, V7X_LANES, V7X_LANES).astype(BF16)
    consts = (*_fft_tables(st.n_lat), _fnet_channel_tables(st.n_lat),
              *_fft_tables(st.n_ctx), _fnet_channel_tables(st.n_ctx))
    f3 = f.reshape(st.bsz, st.nt, FNET_WIDTH)
    blk = pl.BlockSpec((None, st.nt, V7X_LANES), lambda b, s: (b, 0, s))
    o = pl.pallas_call(
        functools.partial(_fnet_kernel, n_lat=st.n_lat, n_ctx=st.n_ctx),
        grid=(st.bsz, FNET_SLABS),
        in_specs=[blk] + [_full_spec(c.shape) for c in consts]
        + [pl.BlockSpec((None, V7X_LANES, V7X_LANES), lambda b, s: (s, 0, 0))],
        out_specs=blk,
        out_shape=jax.ShapeDtypeStruct(f3.shape, F32),
        scratch_shapes=[pltpu.VMEM((2, st.n_lat, V7X_LANES), F32)],
        compiler_params=_params("arbitrary", "arbitrary"),
        name="fnet",
    )(f3, *consts, wf)
    return o.reshape(st.t, FNET_WIDTH)


def _route(h2, wr, e_bias, seen):
    tm = h2.shape[0]
    scores = jax.nn.sigmoid(_dot3(h2, wr))
    sel = scores + e_bias
    lane = lax.broadcasted_iota(jnp.int32, sel.shape, 1).astype(F32)
    hits, ids = [], []
    for _ in range(TOP_K):
        best = sel.max(axis=-1, keepdims=True)
        first = jnp.where(sel == best, lane, float(N_EXPERTS)).min(axis=-1, keepdims=True)
        hit = lane == first
        hits.append(hit)
        ids.append(first)
        sel = jnp.where(hit, -jnp.inf, sel)
    chosen = hits[0]
    for hit in hits[1:]:
        chosen = jnp.logical_or(chosen, hit)
    chosen_f = jnp.where(chosen, 1.0, 0.0)
    r = lax.broadcasted_iota(jnp.int32, (tm, tm), 0)
    c = lax.broadcasted_iota(jnp.int32, (tm, tm), 1)
    earlier = jnp.where(c < r, 1.0, 0.0).astype(BF16)
    rank_all = _bdot(earlier, chosen_f.astype(BF16)) + seen
    k_lane = lax.broadcasted_iota(jnp.int32, (tm, TOP_K), 1)
    idx = jnp.zeros((tm, TOP_K), F32)
    rank = jnp.zeros((tm, TOP_K), F32)
    wts = jnp.zeros((tm, TOP_K), F32)
    for k, (hit, first) in enumerate(zip(hits, ids)):
        idx = jnp.where(k_lane == k, first, idx)
        rank = jnp.where(k_lane == k, jnp.where(hit, rank_all, 0.0).sum(axis=-1, keepdims=True), rank)
        wts = jnp.where(k_lane == k, jnp.where(hit, scores, 0.0).sum(axis=-1, keepdims=True), wts)
    wts = wts / wts.sum(axis=-1, keepdims=True) * ROUTE_SCALE
    return idx.astype(jnp.int32), rank.astype(jnp.int32), wts, seen + chosen_f.sum(axis=0, keepdims=True)


def _out_ln_kernel(*refs, n_in, alpha):
    a_refs = refs[:n_in]
    w_refs = refs[n_in:2 * n_in]
    (x_ref, mod_ref, g_ref, b_ref, wr_ref, eb_ref,
     x1_ref, h2_ref, idx_ref, rank_ref, wts_ref, cnt_ref, seen_ref) = refs[2 * n_in:]

    @pl.when(pl.program_id(0) == 0)
    def _():
        seen_ref[...] = jnp.zeros_like(seen_ref)

    y = None
    for a_ref, w_ref in zip(a_refs, w_refs):
        part = _bdot(a_ref[...].astype(BF16), w_ref[...])
        y = part if y is None else y + part
    z = alpha * x_ref[...] + mod_ref[2:3, :] * y
    x1 = _layer_norm(z, g_ref[...], b_ref[...])
    x1_ref[...] = x1
    h2 = x1 * (1.0 + mod_ref[4:5, :]) + mod_ref[3:4, :]
    h2_ref[...] = h2
    idx, rank, wts, seen = _route(h2, wr_ref[...], eb_ref[...], seen_ref[...])
    idx_ref[...] = idx
    rank_ref[...] = rank
    wts_ref[...] = wts
    seen_ref[...] = seen
    cnt_ref[...] = seen.astype(jnp.int32)


def _out_ln_call(st, acts, w_out, x, mod, layer, ln_g, ln_b, w_router, e_bias, alpha):
    ws, r0 = [], 0
    for a in acts:
        ws.append(w_out[r0:r0 + a.shape[1]].astype(BF16))
        r0 += a.shape[1]
    assert r0 == w_out.shape[0]
    row = lambda v: v.reshape(1, -1)
    return pl.pallas_call(
        functools.partial(_out_ln_kernel, n_in=len(acts), alpha=alpha),
        grid=(st.n_tiles,),
        in_specs=[st.tok_spec(a.shape[1]) for a in acts] + [_full_spec(w.shape) for w in ws]
        + [st.tok_spec(st.d), st.mod_spec(layer), _full_spec((1, st.d)), _full_spec((1, st.d)),
           _full_spec(w_router.shape), _full_spec((1, N_EXPERTS))],
        out_specs=[st.tok_spec(st.d), st.tok_spec(st.d), st.tok_spec(TOP_K), st.tok_spec(TOP_K), st.tok_spec(TOP_K),
                   _full_spec((1, N_EXPERTS))],
        out_shape=[jax.ShapeDtypeStruct((st.t, st.d), F32), jax.ShapeDtypeStruct((st.t, st.d), F32),
                   jax.ShapeDtypeStruct((st.t, TOP_K), jnp.int32), jax.ShapeDtypeStruct((st.t, TOP_K), jnp.int32),
                   jax.ShapeDtypeStruct((st.t, TOP_K), F32), jax.ShapeDtypeStruct((1, N_EXPERTS), jnp.int32)],
        scratch_shapes=[pltpu.VMEM((1, N_EXPERTS), F32)],
        compiler_params=_params("arbitrary"),
        name="out_ln",
    )(*acts, *ws, x, mod, row(ln_g), row(ln_b), w_router, row(e_bias))


MOE_BLOCK = 256
PAIRS_PER_TILE = TOKEN_TILE * TOP_K


def _moe_blocks(n_tok):
    return (n_tok * TOP_K) // MOE_BLOCK + N_EXPERTS


def _slot_plan(counts, idx, rank, n_blocks):
    counts = counts.reshape(N_EXPERTS)
    padded = (counts + MOE_BLOCK - 1) // MOE_BLOCK * MOE_BLOCK
    pad_end = jnp.cumsum(padded)
    pad_start = pad_end - padded
    n_used = jnp.maximum(pad_end[-1] // MOE_BLOCK, 1)
    first_row = jnp.arange(n_blocks, dtype=jnp.int32) * MOE_BLOCK
    block_e = jnp.minimum((first_row[:, None] >= pad_end[None, :]).sum(-1), N_EXPERTS - 1).astype(jnp.int32)
    onehot = idx[:, :, None] == jnp.arange(N_EXPERTS, dtype=jnp.int32)
    dest = jnp.where(onehot, pad_start[None, None, :], 0).sum(-1) + rank
    ends_expert = ((first_row[:, None] + MOE_BLOCK == pad_end[None, :]) & (padded[None, :] > 0)).any(-1)
    fill = (ends_expert | (first_row >= pad_end[-1])).astype(jnp.int32)
    return dest.astype(jnp.int32), block_e, n_used.astype(jnp.int32).reshape(1), fill


def _dispatch_kernel(fill_ref, dest_ref, h_ref, xs_ref, zero_ref, zsem, sem, *, n_blocks):
    @pl.when(pl.program_id(0) == 0)
    def _():
        zero_ref[...] = jnp.zeros_like(zero_ref)

        def fill(b):
            return pltpu.make_async_copy(zero_ref, xs_ref.at[pl.ds(pl.multiple_of(b * MOE_BLOCK, MOE_BLOCK), MOE_BLOCK)],
                                         zsem)

        def start(b, c):
            @pl.when(fill_ref[b] > 0)
            def _():
                fill(b).start()
            return c

        def wait(b, c):
            @pl.when(fill_ref[b] > 0)
            def _():
                fill(b).wait()
            return c

        lax.fori_loop(0, n_blocks, start, 0)
        lax.fori_loop(0, n_blocks, wait, 0)

    def send(t, c):
        for k in range(TOP_K):
            dst = dest_ref[0, t * TOP_K + k]
            pltpu.make_async_copy(h_ref.at[pl.ds(t, 1)], xs_ref.at[pl.ds(dst, 1)], sem).start()
        return c

    lax.fori_loop(0, TOKEN_TILE, send, 0)
    for _ in range(TOP_K):
        pltpu.make_async_copy(h_ref, xs_ref.at[pl.ds(0, TOKEN_TILE)], sem).wait()


def _dispatch_call(st, h2, dest, fill, n_blocks):
    dest3 = dest.reshape(st.n_tiles, 1, PAIRS_PER_TILE)
    grid_spec = pltpu.PrefetchScalarGridSpec(
        num_scalar_prefetch=1,
        grid=(st.n_tiles,),
        in_specs=[pl.BlockSpec((None, 1, PAIRS_PER_TILE), lambda t, *_: (t, 0, 0), memory_space=pltpu.SMEM),
                  pl.BlockSpec((TOKEN_TILE, st.d), lambda t, *_: (t, 0))],
        out_specs=pl.BlockSpec(memory_space=pl.ANY),
        scratch_shapes=[pltpu.VMEM((MOE_BLOCK, st.d), F32), pltpu.SemaphoreType.DMA, pltpu.SemaphoreType.DMA],
    )
    return pl.pallas_call(
        functools.partial(_dispatch_kernel, n_blocks=n_blocks),
        grid_spec=grid_spec,
        out_shape=jax.ShapeDtypeStruct((n_blocks * MOE_BLOCK, st.d), F32),
        compiler_params=_params("arbitrary"),
        name="moe_dispatch",
    )(fill, dest3, h2)


def _experts_kernel(be_ref, nu_ref, x_ref, wg_ref, wu_ref, wd_ref, y_ref, wg_s, wu_s, wd_s):
    i = pl.program_id(0)
    live = i < nu_ref[0]
    prev = be_ref[jnp.maximum(i, 1) - 1]

    @pl.when(jnp.logical_and(live, jnp.logical_or(i == 0, be_ref[i] != prev)))
    def _():
        wg_s[...] = wg_ref[...].astype(BF16)
        wu_s[...] = wu_ref[...].astype(BF16)
        wd_s[...] = wd_ref[...].astype(BF16)

    @pl.when(live)
    def _():
        x = x_ref[...].astype(BF16)
        a = _silu(_bdot(x, wg_s[...])) * _bdot(x, wu_s[...])
        y_ref[...] = _bdot(a.astype(BF16), wd_s[...])

    @pl.when(jnp.logical_not(live))
    def _():
        y_ref[...] = jnp.zeros_like(y_ref)


def _experts_call(st, xs, block_e, n_used, w_gate, w_up, w_down, layer, n_blocks):
    w_in_spec = pl.BlockSpec((None, None, st.d, EXPERT_DIM), lambda i, be, nu: (layer, be[i], 0, 0))
    grid_spec = pltpu.PrefetchScalarGridSpec(
        num_scalar_prefetch=2,
        grid=(n_blocks,),
        in_specs=[pl.BlockSpec((MOE_BLOCK, st.d), lambda i, be, nu: (i, 0)),
                  w_in_spec, w_in_spec,
                  pl.BlockSpec((None, None, EXPERT_DIM, st.d), lambda i, be, nu: (layer, be[i], 0, 0))],
        out_specs=pl.BlockSpec((MOE_BLOCK, st.d), lambda i, be, nu: (i, 0)),
        scratch_shapes=[pltpu.VMEM((st.d, EXPERT_DIM), BF16), pltpu.VMEM((st.d, EXPERT_DIM), BF16),
                        pltpu.VMEM((EXPERT_DIM, st.d), BF16)],
    )
    return pl.pallas_call(
        _experts_kernel,
        grid_spec=grid_spec,
        out_shape=jax.ShapeDtypeStruct(xs.shape, F32),
        compiler_params=_params("arbitrary"),
        name="moe_experts",
    )(block_e, n_used, xs, w_gate, w_up, w_down)


def _ffn_ln_kernel(dest_ref, dnext_ref, wts_ref, h2_ref, x1_ref, sg_ref, su_ref, sd_ref, mod_ref, modn_ref, g_ref,
                   b_ref, ys_ref, x2_ref, hn_ref, rows_ref, sem, *, alpha):
    i = pl.program_id(0)
    n = pl.num_programs(0)
    slot = i % 2

    def fetch(d_ref, s):
        def one(t, c):
            for k in range(TOP_K):
                src = d_ref[0, t * TOP_K + k]
                pltpu.make_async_copy(ys_ref.at[pl.ds(src, 1)], rows_ref.at[s, k, pl.ds(t, 1)], sem.at[s]).start()
            return c

        lax.fori_loop(0, TOKEN_TILE, one, 0)

    @pl.when(i == 0)
    def _():
        fetch(dest_ref, 0)

    @pl.when(i + 1 < n)
    def _():
        fetch(dnext_ref, 1 - slot)

    h2 = h2_ref[...].astype(BF16)
    a = _silu(_bdot(h2, sg_ref[...])) * _bdot(h2, su_ref[...])
    ff = _bdot(a.astype(BF16), sd_ref[...])
    wts = wts_ref[...]
    for k in range(TOP_K):
        pltpu.make_async_copy(ys_ref.at[pl.ds(0, TOKEN_TILE)], rows_ref.at[slot, k], sem.at[slot]).wait()
    for k in range(TOP_K):
        ff = ff + rows_ref[slot, k] * wts[:, k:k + 1]
    z = alpha * x1_ref[...] + mod_ref[5:6, :] * ff
    x2 = _layer_norm(z, g_ref[...], b_ref[...])
    x2_ref[...] = x2
    hn_ref[...] = (x2 * (1.0 + modn_ref[1:2, :]) + modn_ref[0:1, :]).astype(BF16)


def _ffn_ln_call(st, ys, dest, wts, h2, x1, s_gate, s_up, s_down, mod, layer, next_layer, ln_g, ln_b, alpha):
    row = lambda v: v.reshape(1, -1)
    sg, su, sd = s_gate.astype(BF16), s_up.astype(BF16), s_down.astype(BF16)
    dest3 = dest.reshape(st.n_tiles, 1, PAIRS_PER_TILE)
    last = st.n_tiles - 1
    return pl.pallas_call(
        functools.partial(_ffn_ln_kernel, alpha=alpha),
        grid=(st.n_tiles,),
        in_specs=[pl.BlockSpec((None, 1, PAIRS_PER_TILE), lambda t: (t, 0, 0), memory_space=pltpu.SMEM),
                  pl.BlockSpec((None, 1, PAIRS_PER_TILE), lambda t: (jnp.minimum(t + 1, last), 0, 0),
                               memory_space=pltpu.SMEM),
                  st.tok_spec(TOP_K), st.tok_spec(st.d), st.tok_spec(st.d),
                  _full_spec(sg.shape), _full_spec(su.shape), _full_spec(sd.shape),
                  st.mod_spec(layer), st.mod_spec(next_layer), _full_spec((1, st.d)), _full_spec((1, st.d)),
                  pl.BlockSpec(memory_space=pl.ANY)],
        out_specs=[st.tok_spec(st.d), st.tok_spec(st.d)],
        out_shape=[jax.ShapeDtypeStruct((st.t, st.d), F32), jax.ShapeDtypeStruct((st.t, st.d), BF16)],
        scratch_shapes=[pltpu.VMEM((2, TOP_K, TOKEN_TILE, st.d), F32), pltpu.SemaphoreType.DMA((2,))],
        compiler_params=_params("arbitrary"),
        name="ffn_ln",
    )(dest3, dest3, wts, h2, x1, sg, su, sd, mod, mod, row(ln_g), row(ln_b), ys)


def _na_in_kernel(h_ref, w_ref, o_ref):
    scale = jnp.where(pl.program_id(1) == 0, ATTN_SCALE, 1.0)
    o_ref[...] = (_bdot(h_ref[...], w_ref[...]) * scale).astype(BF16)


def _na_in_call(st, h, w_in):
    return pl.pallas_call(
        _na_in_kernel,
        grid=(st.n_tiles, 3),
        in_specs=[pl.BlockSpec((TOKEN_TILE, st.d), lambda t, j: (t, 0)),
                  pl.BlockSpec((st.d, NA_WIDTH), lambda t, j: (0, j))],
        out_specs=pl.BlockSpec((TOKEN_TILE, NA_WIDTH), lambda t, j: (t, j)),
        out_shape=jax.ShapeDtypeStruct((st.t, 3 * NA_WIDTH), BF16),
        compiler_params=_params("arbitrary", "arbitrary"),
        name="na_in",
    )(h, w_in)


def _na_geometry(n_lat):
    rows = n_lat // GRID_W
    kh, kw, qr = min(NA_KH, rows), min(NA_KW, GRID_W), NA_Q_ROWS
    nbr = min(qr + kh - 1, rows)
    col = np.arange(GRID_W)
    col_start = np.clip(col - kw // 2, 0, GRID_W - kw)
    in_col = (col[None, :] >= col_start[:, None]) & (col[None, :] < col_start[:, None] + kw)
    dc = np.clip(col[None, :] - col[:, None] + NA_KW - 1, 0, 2 * NA_KW - 2)
    starts, variant_of, variants = [], [], {}
    for i in range(rows // qr):
        qrow = i * qr + np.arange(qr)
        rstart = np.clip(qrow - kh // 2, 0, rows - kh)
        bs = min(int(rstart[0]), rows - nbr)
        krow = bs + np.arange(nbr)
        in_row = (krow[None, :] >= rstart[:, None]) & (krow[None, :] < rstart[:, None] + kh)
        dr = np.clip(krow[None, :] - qrow[:, None] + NA_KH - 1, 0, 2 * NA_KH - 2)
        key = (in_row.tobytes(), dr.tobytes())
        if key not in variants:
            mask = (in_row[:, None, :, None] & in_col[None, :, None, :]).reshape(qr * GRID_W, nbr * GRID_W)
            variants[key] = (len(variants), dr, mask)
        starts.append(bs)
        variant_of.append(variants[key][0])
    ordered = sorted(variants.values(), key=lambda v: v[0])
    return nbr, np.asarray(starts, np.int32), np.asarray(variant_of, np.int32), [(v[1], v[2]) for v in ordered], dc


def _na_bias_tables(rpb, n_lat):
    nbr, starts, variant_of, variants, dc = _na_geometry(n_lat)
    n_dr, n_dc = 2 * NA_KH - 1, 2 * NA_KW - 1
    col_sel = jnp.asarray((dc[:, :, None] == np.arange(n_dc)).astype(np.float32))
    by_col = jnp.einsum("hrc,wuc->hrwu", rpb, col_sel, precision=lax.Precision.HIGHEST)
    tables = []
    for dr, mask in variants:
        row_sel = jnp.asarray((dr[:, :, None] == np.arange(n_dr)).astype(np.float32))
        bias = jnp.einsum("ajr,hrwu->hawju", row_sel, by_col, precision=lax.Precision.HIGHEST)
        bias = bias.reshape(NA_HEADS, NA_Q_ROWS * GRID_W, nbr * GRID_W)
        tables.append(jnp.where(jnp.asarray(mask)[None], bias, NEG_INF))
    return nbr, starts, variant_of, jnp.stack(tables)


NA_Q_TILE = NA_Q_ROWS * GRID_W


def _na_kernel(start_ref, var_ref, q_ref, k_ref, v_ref, bias_ref, o_ref, *, n_lat, band):
    i = pl.program_id(1)
    is_lat = i < n_lat // NA_Q_TILE

    @pl.when(is_lat)
    def _():
        off = pl.multiple_of(start_ref[i] * GRID_W, GRID_W)
        for h in range(NA_HEADS):
            sl = slice(h * HEAD_DIM, (h + 1) * HEAD_DIM)
            local = (k_ref[pl.ds(off, band), sl], v_ref[pl.ds(off, band), sl])
            ctx = (k_ref[n_lat:, sl], v_ref[n_lat:, sl])
            o_ref[:, sl] = _attend(q_ref[:, sl], [local, ctx], [bias_ref[h], None]).astype(BF16)

    @pl.when(jnp.logical_not(is_lat))
    def _():
        for h in range(NA_HEADS):
            sl = slice(h * HEAD_DIM, (h + 1) * HEAD_DIM)
            o_ref[:, sl] = _attend(q_ref[:, sl], [(k_ref[n_lat:, sl], v_ref[n_lat:, sl])], [None]).astype(BF16)


def _na_call(st, qkv, rpb):
    nbr, starts, variant_of, bias = _na_bias_tables(rpb, st.n_lat)
    band = nbr * GRID_W
    n_q = st.nt // NA_Q_TILE
    pad = n_q - starts.shape[0]
    starts = jnp.asarray(np.concatenate([starts, np.zeros(pad, np.int32)]))
    variant_of = jnp.asarray(np.concatenate([variant_of, np.zeros(pad, np.int32)]))
    qkv3 = qkv.reshape(st.bsz, st.nt, 3 * NA_WIDTH)
    grid_spec = pltpu.PrefetchScalarGridSpec(
        num_scalar_prefetch=2,
        grid=(st.bsz, n_q),
        in_specs=[pl.BlockSpec((None, NA_Q_TILE, NA_WIDTH), lambda b, i, s, v: (b, i, 0)),
                  pl.BlockSpec((None, st.nt, NA_WIDTH), lambda b, i, s, v: (b, 0, 1)),
                  pl.BlockSpec((None, st.nt, NA_WIDTH), lambda b, i, s, v: (b, 0, 2)),
                  pl.BlockSpec((None, NA_HEADS, NA_Q_TILE, band), lambda b, i, s, v: (v[i], 0, 0, 0))],
        out_specs=pl.BlockSpec((None, NA_Q_TILE, NA_WIDTH), lambda b, i, s, v: (b, i, 0)),
    )
    o = pl.pallas_call(
        functools.partial(_na_kernel, n_lat=st.n_lat, band=band),
        grid_spec=grid_spec,
        out_shape=jax.ShapeDtypeStruct((st.bsz, st.nt, NA_WIDTH), BF16),
        compiler_params=_params("arbitrary", "arbitrary"),
        name="na_attn",
    )(starts, variant_of, qkv3, qkv3, qkv3, bias)
    return o.reshape(st.t, NA_WIDTH)


def kernel(x, c, ctx, c_ctx, w_mod, b_mod, ln1_g, ln1_b, ln2_g, ln2_b, ab_w_in, ab_w_fnet, ab_q_norm, ab_k_norm,
           ab_w_out, na_w_in, na_rpb, na_w_out, moe_w_router, moe_bias, moe_w_gate, moe_w_up, moe_w_down,
           sh_w_gate, sh_w_up, sh_w_down):
    bsz, n_lat, d = x.shape
    n_ctx = ctx.shape[1]
    depth = w_mod.shape[0]
    st = _Stream(bsz, n_lat, n_ctx, d)
    alpha = (2 * depth) ** 0.25

    cc = jnp.concatenate([c, c_ctx[None, :], jnp.zeros((MOD_ROWS - bsz - 1, d), F32)], axis=0)
    mod = _mod_call(cc, w_mod, b_mod).reshape(depth, MOD_ROWS, 6, d)
    xs = jnp.concatenate([x, ctx], axis=1).reshape(st.t, d)
    h = _modulate_call(st, xs, mod, 0)
    rope = _rope_tables(n_lat, n_ctx)
    n_blocks = _moe_blocks(st.t)

    for l in range(depth):
        j = l // 2
        if l % 2 == 0:
            f, q, k, v = _ab_in_call(st, h, ab_w_in[j].astype(BF16), ab_q_norm[j], ab_k_norm[j], rope)
            acts = [_fnet_call(st, f, ab_w_fnet[j]), _gqa_call(st, q, k, v)]
            w_out = ab_w_out[j]
        else:
            qkv = _na_in_call(st, h, na_w_in[j].astype(BF16))
            acts = [_na_call(st, qkv, na_rpb[j])]
            w_out = na_w_out[j]
        x1, h2, idx, rank, wts, counts = _out_ln_call(st, acts, w_out, xs, mod, l, ln1_g[l], ln1_b[l],
                                                      moe_w_router[l], moe_bias[l], alpha)
        dest, block_e, n_used, fill = _slot_plan(counts, idx, rank, n_blocks)
        rows = _dispatch_call(st, h2, dest, fill, n_blocks)
        ys = _experts_call(st, rows, block_e, n_used, moe_w_gate, moe_w_up, moe_w_down, l, n_blocks)
        xs, h = _ffn_ln_call(st, ys, dest, wts, h2, x1, sh_w_gate[l], sh_w_up[l], sh_w_down[l], mod, l,
                             min(l + 1, depth - 1), ln2_g[l], ln2_b[l], alpha)
    return xs.reshape(bsz, st.nt, d)[:, :n_lat]
```

```python
import functools
import math

import numpy as np
import jax
import jax.numpy as jnp
from jax import lax
from jax.experimental import pallas as pl
from jax.experimental.pallas import tpu as pltpu

GRID_W = 64
HEAD_DIM = 128
FNET_GROUPS = 4
FNET_GROUP_DIM = 64
FNET_WIDTH = FNET_GROUPS * FNET_GROUP_DIM
GQA_Q_HEADS = 6
GQA_KV_HEADS = 2
GQA_GROUP = GQA_Q_HEADS // GQA_KV_HEADS
ROPE_THETA = 10000.0
NA_HEADS = 8
NA_WIDTH = NA_HEADS * HEAD_DIM
NA_KH = 8
NA_KW = 16
NA_Q_ROWS = 2
NEG_INF = -1e30
N_EXPERTS = 64
TOP_K = 8
EXPERT_DIM = 256
ROUTE_SCALE = 2.5
LN_EPS = 1e-6
RMS_EPS = 1e-6
ATTN_SCALE = HEAD_DIM ** -0.5

V7X_LANES = 128
V7X_SUBLANES = 8
V7X_VMEM_LIMIT_BYTES = 56 * 1024 * 1024

TOKEN_TILE = 256
MOD_ROWS = 8

F32 = jnp.float32
BF16 = jnp.bfloat16


def _params(*sem):
    return pltpu.CompilerParams(dimension_semantics=sem, vmem_limit_bytes=V7X_VMEM_LIMIT_BYTES)


def _bdot(a, b):
    return jnp.dot(a, b, preferred_element_type=F32)


def _bdot_t(a, b):
    return lax.dot_general(a, b, (((1,), (1,)), ((), ())), preferred_element_type=F32)


def _split(x):
    hi = x.astype(BF16)
    lo = (x - hi.astype(F32)).astype(BF16)
    return hi, lo


def _dot3(a, b):
    ah, al = _split(a)
    bh, bl = _split(b)
    return _bdot(ah, bh) + (_bdot(ah, bl) + _bdot(al, bh))


def _silu(x):
    return x * jax.nn.sigmoid(x)


def _layer_norm(z, g, b):
    mu = jnp.mean(z, axis=-1, keepdims=True)
    zc = z - mu
    var = jnp.mean(zc * zc, axis=-1, keepdims=True)
    return zc * lax.rsqrt(var + LN_EPS) * g + b


class _Stream:
    def __init__(self, bsz, n_lat, n_ctx, d):
        assert n_lat % TOKEN_TILE == 0 and n_ctx % TOKEN_TILE == 0
        assert bsz < MOD_ROWS
        self.bsz, self.n_lat, self.n_ctx, self.d = bsz, n_lat, n_ctx, d
        self.nt = n_lat + n_ctx
        self.t = bsz * self.nt
        self.tiles_per_sample = self.nt // TOKEN_TILE
        self.lat_tiles = n_lat // TOKEN_TILE
        self.n_tiles = self.t // TOKEN_TILE

    def mod_row(self, tile):
        return jnp.where(tile % self.tiles_per_sample < self.lat_tiles, tile // self.tiles_per_sample, self.bsz)

    def mod_spec(self, layer):
        return pl.BlockSpec((None, None, 6, self.d), lambda t: (layer, self.mod_row(t), 0, 0))

    def tok_spec(self, width):
        return pl.BlockSpec((TOKEN_TILE, width), lambda t: (t, 0))


def _full_spec(shape):
    nd = len(shape)
    return pl.BlockSpec(shape, lambda *_: (0,) * nd)


def _mod_kernel(cc_ref, w_ref, b_ref, o_ref):
    o_ref[...] = _dot3(_silu(cc_ref[...]), w_ref[...]) + b_ref[...]


def _mod_call(cc, w_mod, b_mod):
    depth, d, n = w_mod.shape
    tn = n // 4
    return pl.pallas_call(
        _mod_kernel,
        grid=(depth, n // tn),
        in_specs=[pl.BlockSpec((MOD_ROWS, d), lambda l, j: (0, 0)),
                  pl.BlockSpec((None, d, tn), lambda l, j: (l, 0, j)),
                  pl.BlockSpec((None, 1, tn), lambda l, j: (l, 0, j))],
        out_specs=pl.BlockSpec((None, MOD_ROWS, tn), lambda l, j: (l, 0, j)),
        out_shape=jax.ShapeDtypeStruct((depth, MOD_ROWS, n), F32),
        compiler_params=_params("arbitrary", "arbitrary"),
        name="mod",
    )(cc, w_mod, b_mod.reshape(depth, 1, n))


def _modulate_kernel(x_ref, mod_ref, h_ref):
    h_ref[...] = (x_ref[...] * (1.0 + mod_ref[1:2, :]) + mod_ref[0:1, :]).astype(BF16)


def _modulate_call(st, x, mod, layer):
    return pl.pallas_call(
        _modulate_kernel,
        grid=(st.n_tiles,),
        in_specs=[st.tok_spec(st.d), st.mod_spec(layer)],
        out_specs=st.tok_spec(st.d),
        out_shape=jax.ShapeDtypeStruct((st.t, st.d), BF16),
        compiler_params=_params("arbitrary"),
        name="modulate",
    )(x, mod)


def _rope_tables(n_lat, n_ctx):
    half = HEAD_DIM // 2
    nf = half // 2
    t = np.arange(n_lat)
    inv = ROPE_THETA ** (-(2.0 / half) * np.arange(nf, dtype=np.float64))
    ang_r = (t // GRID_W)[:, None] * inv
    ang_c = (t % GRID_W)[:, None] * inv
    zeros = np.zeros_like(ang_r)
    cos = np.concatenate([np.cos(ang_r), np.cos(ang_r), np.cos(ang_c), np.cos(ang_c)], axis=1)
    sin_fwd = np.concatenate([-np.sin(ang_r), zeros, -np.sin(ang_c), zeros], axis=1)
    sin_bwd = np.concatenate([zeros, np.sin(ang_r), zeros, np.sin(ang_c)], axis=1)
    pad = lambda a, v: np.concatenate([a, np.full((n_ctx, HEAD_DIM), v)], axis=0).astype(np.float32)
    return jnp.asarray(pad(cos, 1.0)), jnp.asarray(pad(sin_fwd, 0.0)), jnp.asarray(pad(sin_bwd, 0.0))


def _ab_in_kernel(h_ref, w_ref, qg_ref, kg_ref, cos_ref, sf_ref, sb_ref, f_ref, q_ref, k_ref, v_ref):
    acc = _bdot(h_ref[...], w_ref[...])
    cos, sf, sb = cos_ref[...], sf_ref[...], sb_ref[...]
    nf = HEAD_DIM // 4

    def norm_rope(xh, gain):
        ms = jnp.mean(xh * xh, axis=-1, keepdims=True)
        y = xh * lax.rsqrt(ms + RMS_EPS) * gain
        return y * cos + pltpu.roll(y, HEAD_DIM - nf, 1) * sf + pltpu.roll(y, nf, 1) * sb

    f_ref[...] = acc[:, :FNET_WIDTH]
    q0 = FNET_WIDTH
    k0 = q0 + GQA_Q_HEADS * HEAD_DIM
    v0 = k0 + GQA_KV_HEADS * HEAD_DIM
    for h in range(GQA_Q_HEADS):
        xh = acc[:, q0 + h * HEAD_DIM:q0 + (h + 1) * HEAD_DIM]
        q_ref[:, h * HEAD_DIM:(h + 1) * HEAD_DIM] = (norm_rope(xh, qg_ref[...]) * ATTN_SCALE).astype(BF16)
    for h in range(GQA_KV_HEADS):
        xh = acc[:, k0 + h * HEAD_DIM:k0 + (h + 1) * HEAD_DIM]
        k_ref[:, h * HEAD_DIM:(h + 1) * HEAD_DIM] = norm_rope(xh, kg_ref[...]).astype(BF16)
    v_ref[...] = acc[:, v0:].astype(BF16)


def _ab_in_call(st, h, w_in, q_gain, k_gain, rope):
    nq = GQA_Q_HEADS * HEAD_DIM
    nkv = GQA_KV_HEADS * HEAD_DIM
    pos_spec = pl.BlockSpec((TOKEN_TILE, HEAD_DIM), lambda t: (t % st.tiles_per_sample, 0))
    return pl.pallas_call(
        _ab_in_kernel,
        grid=(st.n_tiles,),
        in_specs=[st.tok_spec(st.d), _full_spec(w_in.shape), _full_spec((1, HEAD_DIM)), _full_spec((1, HEAD_DIM)),
                  pos_spec, pos_spec, pos_spec],
        out_specs=[st.tok_spec(FNET_WIDTH), st.tok_spec(nq), st.tok_spec(nkv), st.tok_spec(nkv)],
        out_shape=[jax.ShapeDtypeStruct((st.t, FNET_WIDTH), F32), jax.ShapeDtypeStruct((st.t, nq), BF16),
                   jax.ShapeDtypeStruct((st.t, nkv), BF16), jax.ShapeDtypeStruct((st.t, nkv), BF16)],
        compiler_params=_params("arbitrary"),
        name="ab_in",
    )(h, w_in, q_gain.reshape(1, HEAD_DIM), k_gain.reshape(1, HEAD_DIM), *rope)


def _attend(q, keys_values, biases):
    scores = []
    for (k, _), bias in zip(keys_values, biases):
        s = _bdot_t(q, k)
        scores.append(s if bias is None else s + bias)
    m = scores[0].max(axis=-1, keepdims=True)
    for s in scores[1:]:
        m = jnp.maximum(m, s.max(axis=-1, keepdims=True))
    num = None
    den = None
    for s, (_, v) in zip(scores, keys_values):
        p = jnp.exp(s - m)
        pv = _bdot(p.astype(BF16), v)
        ps = p.sum(axis=-1, keepdims=True)
        num = pv if num is None else num + pv
        den = ps if den is None else den + ps
    return num / den


def _gqa_kernel(q_ref, k_ref, v_ref, o_ref, *, n_lat, lat_tiles):
    def run(k, v):
        for h in range(GQA_GROUP):
            sl = slice(h * HEAD_DIM, (h + 1) * HEAD_DIM)
            o_ref[:, sl] = _attend(q_ref[:, sl], [(k, v)], [None]).astype(BF16)

    is_lat = pl.program_id(2) < lat_tiles

    @pl.when(is_lat)
    def _():
        run(k_ref[...], v_ref[...])

    @pl.when(jnp.logical_not(is_lat))
    def _():
        run(k_ref[n_lat:, :], v_ref[n_lat:, :])


def _gqa_call(st, q, k, v):
    gw = GQA_GROUP * HEAD_DIM
    q3 = q.reshape(st.bsz, st.nt, GQA_Q_HEADS * HEAD_DIM)
    k3 = k.reshape(st.bsz, st.nt, GQA_KV_HEADS * HEAD_DIM)
    v3 = v.reshape(st.bsz, st.nt, GQA_KV_HEADS * HEAD_DIM)
    q_spec = pl.BlockSpec((None, TOKEN_TILE, gw), lambda b, g, i: (b, i, g))
    kv_spec = pl.BlockSpec((None, st.nt, HEAD_DIM), lambda b, g, i: (b, 0, g))
    o = pl.pallas_call(
        functools.partial(_gqa_kernel, n_lat=st.n_lat, lat_tiles=st.lat_tiles),
        grid=(st.bsz, GQA_KV_HEADS, st.tiles_per_sample),
        in_specs=[q_spec, kv_spec, kv_spec],
        out_specs=q_spec,
        out_shape=jax.ShapeDtypeStruct(q3.shape, BF16),
        compiler_params=_params("arbitrary", "arbitrary", "arbitrary"),
        name="gqa",
    )(q3, k3, v3)
    return o.reshape(st.t, GQA_Q_HEADS * HEAD_DIM)


def _fft_split(n):
    l1 = 1 << ((n.bit_length() - 1 + 1) // 2)
    assert n % l1 == 0 and n == l1 * (n // l1)
    return l1, n // l1


def _fft_tables(n):
    l1, l2 = _fft_split(n)
    a = np.arange(l1, dtype=np.float64)
    ang1 = 2.0 * np.pi * np.outer(a, a) / l1
    stage1 = np.concatenate([np.cos(ang1), -np.sin(ang1)], axis=0)
    b = np.arange(l2, dtype=np.float64)
    ang_t = 2.0 * np.pi * np.outer(b, a) / n
    tw_cos = np.cos(ang_t)[:, :, None]
    tw_sin = np.sin(ang_t)[:, :, None]
    ang2 = 2.0 * np.pi * np.outer(b, b) / l2
    c2, s2 = np.cos(ang2), np.sin(ang2)
    stage2 = np.block([[c2, s2], [-s2, c2]])
    f32 = lambda x: jnp.asarray(x.astype(np.float32))
    return f32(stage1), f32(tw_cos), f32(tw_sin), f32(stage2)


FNET_GROUPS_PER_SLAB = V7X_LANES // FNET_GROUP_DIM
FNET_SLABS = FNET_WIDTH // V7X_LANES


def _fnet_channel_tables(n_positions):
    c = np.arange(FNET_GROUP_DIM, dtype=np.float64)
    ang = 2.0 * np.pi * np.outer(c, c) / FNET_GROUP_DIM
    eye = np.eye(FNET_GROUPS_PER_SLAB)
    scale = 1.0 / math.sqrt(n_positions * FNET_GROUP_DIM)
    m = np.concatenate([np.kron(eye, np.cos(ang)), np.kron(eye, np.sin(ang))], axis=0) * scale
    return jnp.asarray(m.astype(np.float32))


def _fnet_part(f_ref, o_ref, a_ref, row0, n, s1_ref, tc_ref, ts_ref, s2_ref, ch_ref, wf_ref):
    l1, l2 = _fft_split(n)
    stage1 = s1_ref[...]
    stage2 = s2_ref[...]
    chan = ch_ref[...]
    wf = wf_ref[...]

    def first(j, carry):
        xs = f_ref[pl.ds(row0 + j, l1, stride=l2), :]
        a = _dot3(stage1, xs)
        ar, ai = a[:l1], a[l1:]
        tc, ts = tc_ref[j], ts_ref[j]
        a_ref[0, pl.ds(pl.multiple_of(j * l1, l1), l1), :] = ar * tc + ai * ts
        a_ref[1, pl.ds(pl.multiple_of(j * l1, l1), l1), :] = ai * tc - ar * ts
        return carry

    lax.fori_loop(0, l2, first, 0)

    def second(j, carry):
        br = a_ref[0, pl.ds(j, l2, stride=l1), :]
        bi = a_ref[1, pl.ds(j, l2, stride=l1), :]
        p = _dot3(stage2, jnp.concatenate([br, bi], axis=0))
        re = _dot3(jnp.concatenate([p[:l2], p[l2:]], axis=1), chan)
        o_ref[pl.ds(row0 + j, l2, stride=l1), :] = _bdot(re.astype(BF16), wf)
        return carry

    lax.fori_loop(0, l1, second, 0)


def _fnet_kernel(f_ref, s1l, tcl, tsl, s2l, chl, s1c, tcc, tsc, s2c, chc, wf_ref, o_ref, a_ref, *, n_lat, n_ctx):
    _fnet_part(f_ref, o_ref, a_ref, 0, n_lat, s1l, tcl, tsl, s2l, chl, wf_ref)
    _fnet_part(f_ref, o_ref, a_ref, n_lat, n_ctx, s1c, tcc, tsc, s2c, chc, wf_ref)


def _fnet_call(st, f, w_fnet):
    gps = FNET_GROUPS_PER_SLAB
    eye = jnp.eye(gps, dtype=F32)
    wg = w_fnet.reshape(FNET_SLABS, gps, FNET_GROUP_DIM, FNET_GROUP_DIM)
    wf = (eye[None, :, None, :, None] * wg[:, :, :, None, :]).reshape(FNET_SLABS, V7X_LANES, V7X_LANES).astype(BF16)
    consts = (*_fft_tables(st.n_lat), _fnet_channel_tables(st.n_lat),
              *_fft_tables(st.n_ctx), _fnet_channel_tables(st.n_ctx))
    f3 = f.reshape(st.bsz, st.nt, FNET_WIDTH)
    blk = pl.BlockSpec((None, st.nt, V7X_LANES), lambda b, s: (b, 0, s))
    o = pl.pallas_call(
        functools.partial(_fnet_kernel, n_lat=st.n_lat, n_ctx=st.n_ctx),
        grid=(st.bsz, FNET_SLABS),
        in_specs=[blk] + [_full_spec(c.shape) for c in consts]
        + [pl.BlockSpec((None, V7X_LANES, V7X_LANES), lambda b, s: (s, 0, 0))],
        out_specs=blk,
        out_shape=jax.ShapeDtypeStruct(f3.shape, F32),
        scratch_shapes=[pltpu.VMEM((2, st.n_lat, V7X_LANES), F32)],
        compiler_params=_params("arbitrary", "arbitrary"),
        name="fnet",
    )(f3, *consts, wf)
    return o.reshape(st.t, FNET_WIDTH)


SEG_ALIGN = V7X_SUBLANES
PAIRS_PER_TILE = TOKEN_TILE * TOP_K
LOCAL_ROWS = PAIRS_PER_TILE + N_EXPERTS * SEG_ALIGN
ROUTE_LANES = V7X_LANES


def _route(h2, wr, e_bias):
    tm = h2.shape[0]
    scores = jax.nn.sigmoid(_dot3(h2, wr))
    sel = scores + e_bias
    lane = lax.broadcasted_iota(jnp.int32, sel.shape, 1).astype(F32)
    hits = []
    for _ in range(TOP_K):
        best = sel.max(axis=-1, keepdims=True)
        first = jnp.where(sel == best, lane, float(N_EXPERTS)).min(axis=-1, keepdims=True)
        hit = lane == first
        hits.append(hit)
        sel = jnp.where(hit, -jnp.inf, sel)
    chosen = hits[0]
    for hit in hits[1:]:
        chosen = jnp.logical_or(chosen, hit)
    chosen_f = jnp.where(chosen, 1.0, 0.0)
    counts = chosen_f.sum(axis=0, keepdims=True)
    seg_len = jnp.ceil(counts * (1.0 / SEG_ALIGN)) * SEG_ALIGN
    er = lax.broadcasted_iota(jnp.int32, (N_EXPERTS, N_EXPERTS), 0)
    ec = lax.broadcasted_iota(jnp.int32, (N_EXPERTS, N_EXPERTS), 1)
    lower_experts = jnp.where(er < ec, 1.0, 0.0).astype(BF16)
    seg_start = _bdot(jnp.broadcast_to(seg_len, (V7X_SUBLANES, N_EXPERTS)).astype(BF16), lower_experts)[0:1]
    r = lax.broadcasted_iota(jnp.int32, (tm, tm), 0)
    c = lax.broadcasted_iota(jnp.int32, (tm, tm), 1)
    earlier = jnp.where(c < r, 1.0, 0.0).astype(BF16)
    row_all = _bdot(earlier, chosen_f.astype(BF16)) + seg_start
    rec_lane = lax.broadcasted_iota(jnp.int32, (tm, ROUTE_LANES), 1)
    rec = jnp.zeros((tm, ROUTE_LANES), F32)
    raw = []
    for k, hit in enumerate(hits):
        rec = jnp.where(rec_lane == k, jnp.where(hit, row_all, 0.0).sum(axis=-1, keepdims=True), rec)
        raw.append(jnp.where(hit, scores, 0.0).sum(axis=-1, keepdims=True))
    total = raw[0]
    for w in raw[1:]:
        total = total + w
    for k, w in enumerate(raw):
        rec = jnp.where(rec_lane == TOP_K + k, w / total * ROUTE_SCALE, rec)
    return rec, counts.astype(jnp.int32)


def _out_ln_kernel(*refs, n_in, alpha):
    a_refs = refs[:n_in]
    w_refs = refs[n_in:2 * n_in]
    x_ref, mod_ref, g_ref, b_ref, wr_ref, eb_ref, x1_ref, h2_ref, rec_ref, cnt_ref = refs[2 * n_in:]
    y = None
    for a_ref, w_ref in zip(a_refs, w_refs):
        part = _bdot(a_ref[...].astype(BF16), w_ref[...])
        y = part if y is None else y + part
    z = alpha * x_ref[...] + mod_ref[2:3, :] * y
    x1 = _layer_norm(z, g_ref[...], b_ref[...])
    x1_ref[...] = x1
    h2 = x1 * (1.0 + mod_ref[4:5, :]) + mod_ref[3:4, :]
    h2_ref[...] = h2.astype(BF16)
    rec, counts = _route(h2, wr_ref[...], eb_ref[...])
    rec_ref[...] = rec
    cnt_ref[...] = counts


def _out_ln_call(st, acts, w_out, x, mod, layer, ln_g, ln_b, w_router, e_bias, alpha):
    ws, r0 = [], 0
    for a in acts:
        ws.append(w_out[r0:r0 + a.shape[1]].astype(BF16))
        r0 += a.shape[1]
    assert r0 == w_out.shape[0]
    row = lambda v: v.reshape(1, -1)
    return pl.pallas_call(
        functools.partial(_out_ln_kernel, n_in=len(acts), alpha=alpha),
        grid=(st.n_tiles,),
        in_specs=[st.tok_spec(a.shape[1]) for a in acts] + [_full_spec(w.shape) for w in ws]
        + [st.tok_spec(st.d), st.mod_spec(layer), _full_spec((1, st.d)), _full_spec((1, st.d)),
           _full_spec(w_router.shape), _full_spec((1, N_EXPERTS))],
        out_specs=[st.tok_spec(st.d), st.tok_spec(st.d), st.tok_spec(ROUTE_LANES),
                   pl.BlockSpec((None, 1, N_EXPERTS), lambda t: (t, 0, 0))],
        out_shape=[jax.ShapeDtypeStruct((st.t, st.d), F32), jax.ShapeDtypeStruct((st.t, st.d), BF16),
                   jax.ShapeDtypeStruct((st.t, ROUTE_LANES), F32),
                   jax.ShapeDtypeStruct((st.n_tiles, 1, N_EXPERTS), jnp.int32)],
        compiler_params=_params("arbitrary"),
        name="out_ln",
    )(*acts, *ws, x, mod, row(ln_g), row(ln_b), w_router, row(e_bias))


MOE_BLOCK = 256
SEG_SIZES = tuple(TOKEN_TILE >> s for s in range((TOKEN_TILE // SEG_ALIGN).bit_length()))
ROW_CHUNK = 256


def _moe_blocks(st):
    return (st.t * TOP_K + st.n_tiles * N_EXPERTS * (SEG_ALIGN - 1)) // MOE_BLOCK + N_EXPERTS


def _slot_plan(tile_counts, n_blocks):
    n_tiles = tile_counts.shape[0]
    cnt = tile_counts.reshape(n_tiles, N_EXPERTS)
    seg = (cnt + SEG_ALIGN - 1) // SEG_ALIGN * SEG_ALIGN
    local_start = jnp.cumsum(seg, axis=1) - seg
    tiles_before = jnp.cumsum(seg, axis=0) - seg
    total = seg.sum(axis=0)
    padded = (total + MOE_BLOCK - 1) // MOE_BLOCK * MOE_BLOCK
    pad_end = jnp.cumsum(padded)
    pad_start = pad_end - padded
    sorted_start = pad_start[None, :] + tiles_before
    plan = jnp.concatenate([seg, local_start, sorted_start], axis=1).astype(jnp.int32).reshape(n_tiles, 1, -1)
    n_used = jnp.maximum(pad_end[-1] // MOE_BLOCK, 1)
    first_row = jnp.arange(n_blocks, dtype=jnp.int32) * MOE_BLOCK
    block_e = jnp.minimum((first_row[:, None] >= pad_end[None, :]).sum(-1), N_EXPERTS - 1).astype(jnp.int32)
    ends_expert = ((first_row[:, None] + MOE_BLOCK == pad_end[None, :]) & (padded[None, :] > 0)).any(-1)
    fill = (ends_expert | (first_row >= pad_end[-1])).astype(jnp.int32)
    return plan, block_e, n_used.astype(jnp.int32).reshape(1), fill


def _for_each_piece(plan_ref, fn):
    def per_expert(e, carry):
        seg = plan_ref[0, e]
        local = plan_ref[0, N_EXPERTS + e]
        dst = plan_ref[0, 2 * N_EXPERTS + e]
        done = 0
        for size in SEG_SIZES:
            take = (seg & size) != 0

            @pl.when(take)
            def _(done=done, size=size):
                fn(pl.multiple_of(local + done, SEG_ALIGN), pl.multiple_of(dst + done, SEG_ALIGN), size)

            done = done + jnp.where(take, size, 0)
        return carry

    lax.fori_loop(0, N_EXPERTS, per_expert, 0)


def _pack_halves(x):
    bits = pltpu.bitcast(x, jnp.uint32)
    half = x.shape[1] // 2
    return (bits[:, :half] >> 16) | (bits[:, half:] & jnp.uint32(0xFFFF0000))


def _unpack_halves(w):
    lo = pltpu.bitcast(w << 16, F32).astype(BF16)
    hi = pltpu.bitcast(w & jnp.uint32(0xFFFF0000), F32).astype(BF16)
    return lo, hi


def _dispatch_kernel(fill_ref, plan_ref, plan1_ref, plan2_ref, rec_ref, h_ref, xs_ref, loc_ref, zero_ref, zsem, sem,
                     *, n_blocks, n_tiles):
    i = pl.program_id(0)
    slot = i % 2

    @pl.when(i == 0)
    def _():
        zero_ref[...] = jnp.zeros_like(zero_ref)

        def fill(b):
            return pltpu.make_async_copy(zero_ref, xs_ref.at[pl.ds(pl.multiple_of(b * MOE_BLOCK, MOE_BLOCK), MOE_BLOCK)],
                                         zsem)

        def start(b, c):
            @pl.when(fill_ref[b] > 0)
            def _():
                fill(b).start()
            return c

        def wait(b, c):
            @pl.when(fill_ref[b] > 0)
            def _():
                fill(b).wait()
            return c

        lax.fori_loop(0, n_blocks, start, 0)
        lax.fori_loop(0, n_blocks, wait, 0)

    def piece(s):
        def copy(local, dst, size):
            return pltpu.make_async_copy(loc_ref.at[s, pl.ds(local, size)], xs_ref.at[pl.ds(dst, size)], sem.at[s])
        return copy

    @pl.when(i >= 2)
    def _():
        _for_each_piece(plan2_ref, lambda *a: piece(slot)(*a).wait())

    rows_of = rec_ref[...].T
    x = h_ref[...]
    for c0 in range(0, LOCAL_ROWS, ROW_CHUNK):
        local_row = (lax.broadcasted_iota(jnp.int32, (ROW_CHUNK, TOKEN_TILE), 0) + c0).astype(F32)
        onehot = jnp.zeros((ROW_CHUNK, TOKEN_TILE), F32)
        for k in range(TOP_K):
            onehot = jnp.where(local_row == rows_of[k:k + 1, :], 1.0, onehot)
        loc_ref[slot, c0:c0 + ROW_CHUNK, :] = _pack_halves(_bdot(onehot.astype(BF16), x))

    _for_each_piece(plan_ref, lambda *a: piece(slot)(*a).start())

    @pl.when(i == n_tiles - 1)
    def _():
        if n_tiles >= 2:
            _for_each_piece(plan1_ref, lambda *a: piece(1 - slot)(*a).wait())
        _for_each_piece(plan_ref, lambda *a: piece(slot)(*a).wait())


def _dispatch_call(st, h2, rec, plan, fill, n_blocks):
    half = st.d // 2
    plan_spec = lambda back: pl.BlockSpec((None, 1, 3 * N_EXPERTS), lambda t, *_: (jnp.maximum(t - back, 0), 0, 0),
                                          memory_space=pltpu.SMEM)
    grid_spec = pltpu.PrefetchScalarGridSpec(
        num_scalar_prefetch=1,
        grid=(st.n_tiles,),
        in_specs=[plan_spec(0), plan_spec(1), plan_spec(2),
                  pl.BlockSpec((TOKEN_TILE, ROUTE_LANES), lambda t, *_: (t, 0)),
                  pl.BlockSpec((TOKEN_TILE, st.d), lambda t, *_: (t, 0))],
        out_specs=pl.BlockSpec(memory_space=pl.ANY),
        scratch_shapes=[pltpu.VMEM((2, LOCAL_ROWS, half), jnp.uint32), pltpu.VMEM((MOE_BLOCK, half), jnp.uint32),
                        pltpu.SemaphoreType.DMA, pltpu.SemaphoreType.DMA((2,))],
    )
    return pl.pallas_call(
        functools.partial(_dispatch_kernel, n_blocks=n_blocks, n_tiles=st.n_tiles),
        grid_spec=grid_spec,
        out_shape=jax.ShapeDtypeStruct((n_blocks * MOE_BLOCK, half), jnp.uint32),
        compiler_params=_params("arbitrary"),
        name="moe_dispatch",
    )(fill, plan, plan, plan, rec, h2)


def _experts_kernel(be_ref, nu_ref, x_ref, wg_ref, wu_ref, wd_ref, y_ref, wg_s, wu_s, wd_s):
    i = pl.program_id(0)
    live = i < nu_ref[0]
    prev = be_ref[jnp.maximum(i, 1) - 1]

    @pl.when(jnp.logical_and(live, jnp.logical_or(i == 0, be_ref[i] != prev)))
    def _():
        wg_s[...] = wg_ref[...].astype(BF16)
        wu_s[...] = wu_ref[...].astype(BF16)
        wd_s[...] = wd_ref[...].astype(BF16)

    @pl.when(live)
    def _():
        half = wg_s.shape[0] // 2
        lo, hi = _unpack_halves(x_ref[...])
        gate = _bdot(lo, wg_s[:half, :]) + _bdot(hi, wg_s[half:, :])
        up = _bdot(lo, wu_s[:half, :]) + _bdot(hi, wu_s[half:, :])
        y = _bdot((_silu(gate) * up).astype(BF16), wd_s[...])
        y_ref[...] = _pack_halves(y.astype(BF16).astype(F32))

    @pl.when(jnp.logical_not(live))
    def _():
        y_ref[...] = jnp.zeros_like(y_ref)


def _experts_call(st, xs, block_e, n_used, w_gate, w_up, w_down, layer, n_blocks):
    half = st.d // 2
    w_in_spec = pl.BlockSpec((None, None, st.d, EXPERT_DIM), lambda i, be, nu: (layer, be[i], 0, 0))
    grid_spec = pltpu.PrefetchScalarGridSpec(
        num_scalar_prefetch=2,
        grid=(n_blocks,),
        in_specs=[pl.BlockSpec((MOE_BLOCK, half), lambda i, be, nu: (i, 0)),
                  w_in_spec, w_in_spec,
                  pl.BlockSpec((None, None, EXPERT_DIM, st.d), lambda i, be, nu: (layer, be[i], 0, 0))],
        out_specs=pl.BlockSpec((MOE_BLOCK, half), lambda i, be, nu: (i, 0)),
        scratch_shapes=[pltpu.VMEM((st.d, EXPERT_DIM), BF16), pltpu.VMEM((st.d, EXPERT_DIM), BF16),
                        pltpu.VMEM((EXPERT_DIM, st.d), BF16)],
    )
    return pl.pallas_call(
        _experts_kernel,
        grid_spec=grid_spec,
        out_shape=jax.ShapeDtypeStruct(xs.shape, jnp.uint32),
        compiler_params=_params("arbitrary"),
        name="moe_experts",
    )(block_e, n_used, xs, w_gate, w_up, w_down)


def _ffn_ln_kernel(plan_ref, plan_next_ref, rec_ref, h2_ref, x1_ref, sg_ref, su_ref, sd_ref, mod_ref, modn_ref, g_ref,
                   b_ref, ys_ref, x2_ref, hn_ref, loc_ref, sem, *, alpha, n_tiles):
    i = pl.program_id(0)
    slot = i % 2

    def piece(s):
        def copy(local, src, size):
            return pltpu.make_async_copy(ys_ref.at[pl.ds(src, size)], loc_ref.at[s, pl.ds(local, size)], sem.at[s])
        return copy

    @pl.when(i == 0)
    def _():
        loc_ref[...] = jnp.zeros_like(loc_ref)
        _for_each_piece(plan_ref, lambda *a: piece(0)(*a).start())

    @pl.when(i + 1 < n_tiles)
    def _():
        _for_each_piece(plan_next_ref, lambda *a: piece(1 - slot)(*a).start())

    h2 = h2_ref[...]
    a = _silu(_bdot(h2, sg_ref[...])) * _bdot(h2, su_ref[...])
    shared = _bdot(a.astype(BF16), sd_ref[...])

    _for_each_piece(plan_ref, lambda *a: piece(slot)(*a).wait())
    rec = rec_ref[...]
    half = loc_ref.shape[2]
    routed_lo = jnp.zeros((TOKEN_TILE, half), F32)
    routed_hi = jnp.zeros((TOKEN_TILE, half), F32)
    for c0 in range(0, LOCAL_ROWS, ROW_CHUNK):
        local_row = (lax.broadcasted_iota(jnp.int32, (TOKEN_TILE, ROW_CHUNK), 1) + c0).astype(F32)
        weight = jnp.zeros((TOKEN_TILE, ROW_CHUNK), F32)
        for k in range(TOP_K):
            weight = jnp.where(local_row == rec[:, k:k + 1], rec[:, TOP_K + k:TOP_K + k + 1], weight)
        lo, hi = _unpack_halves(loc_ref[slot, c0:c0 + ROW_CHUNK, :])
        weight = weight.astype(BF16)
        routed_lo = routed_lo + _bdot(weight, lo)
        routed_hi = routed_hi + _bdot(weight, hi)
    ff = jnp.concatenate([routed_lo, routed_hi], axis=1) + shared
    z = alpha * x1_ref[...] + mod_ref[5:6, :] * ff
    x2 = _layer_norm(z, g_ref[...], b_ref[...])
    x2_ref[...] = x2
    hn_ref[...] = (x2 * (1.0 + modn_ref[1:2, :]) + modn_ref[0:1, :]).astype(BF16)


def _ffn_ln_call(st, ys, plan, rec, h2, x1, s_gate, s_up, s_down, mod, layer, next_layer, ln_g, ln_b, alpha):
    row = lambda v: v.reshape(1, -1)
    sg, su, sd = s_gate.astype(BF16), s_up.astype(BF16), s_down.astype(BF16)
    last = st.n_tiles - 1
    return pl.pallas_call(
        functools.partial(_ffn_ln_kernel, alpha=alpha, n_tiles=st.n_tiles),
        grid=(st.n_tiles,),
        in_specs=[pl.BlockSpec((None, 1, 3 * N_EXPERTS), lambda t: (t, 0, 0), memory_space=pltpu.SMEM),
                  pl.BlockSpec((None, 1, 3 * N_EXPERTS), lambda t: (jnp.minimum(t + 1, last), 0, 0),
                               memory_space=pltpu.SMEM),
                  st.tok_spec(ROUTE_LANES), st.tok_spec(st.d), st.tok_spec(st.d),
                  _full_spec(sg.shape), _full_spec(su.shape), _full_spec(sd.shape),
                  st.mod_spec(layer), st.mod_spec(next_layer), _full_spec((1, st.d)), _full_spec((1, st.d)),
                  pl.BlockSpec(memory_space=pl.ANY)],
        out_specs=[st.tok_spec(st.d), st.tok_spec(st.d)],
        out_shape=[jax.ShapeDtypeStruct((st.t, st.d), F32), jax.ShapeDtypeStruct((st.t, st.d), BF16)],
        scratch_shapes=[pltpu.VMEM((2, LOCAL_ROWS, st.d // 2), jnp.uint32), pltpu.SemaphoreType.DMA((2,))],
        compiler_params=_params("arbitrary"),
        name="ffn_ln",
    )(plan, plan, rec, h2, x1, sg, su, sd, mod, mod, row(ln_g), row(ln_b), ys)


def _na_in_kernel(h_ref, w_ref, o_ref):
    scale = jnp.where(pl.program_id(1) == 0, ATTN_SCALE, 1.0)
    o_ref[...] = (_bdot(h_ref[...], w_ref[...]) * scale).astype(BF16)


def _na_in_call(st, h, w_in):
    return pl.pallas_call(
        _na_in_kernel,
        grid=(st.n_tiles, 3),
        in_specs=[pl.BlockSpec((TOKEN_TILE, st.d), lambda t, j: (t, 0)),
                  pl.BlockSpec((st.d, NA_WIDTH), lambda t, j: (0, j))],
        out_specs=pl.BlockSpec((TOKEN_TILE, NA_WIDTH), lambda t, j: (t, j)),
        out_shape=jax.ShapeDtypeStruct((st.t, 3 * NA_WIDTH), BF16),
        compiler_params=_params("arbitrary", "arbitrary"),
        name="na_in",
    )(h, w_in)


def _na_geometry(n_lat):
    rows = n_lat // GRID_W
    kh, kw, qr = min(NA_KH, rows), min(NA_KW, GRID_W), NA_Q_ROWS
    nbr = min(qr + kh - 1, rows)
    col = np.arange(GRID_W)
    col_start = np.clip(col - kw // 2, 0, GRID_W - kw)
    in_col = (col[None, :] >= col_start[:, None]) & (col[None, :] < col_start[:, None] + kw)
    dc = np.clip(col[None, :] - col[:, None] + NA_KW - 1, 0, 2 * NA_KW - 2)
    starts, variant_of, variants = [], [], {}
    for i in range(rows // qr):
        qrow = i * qr + np.arange(qr)
        rstart = np.clip(qrow - kh // 2, 0, rows - kh)
        bs = min(int(rstart[0]), rows - nbr)
        krow = bs + np.arange(nbr)
        in_row = (krow[None, :] >= rstart[:, None]) & (krow[None, :] < rstart[:, None] + kh)
        dr = np.clip(krow[None, :] - qrow[:, None] + NA_KH - 1, 0, 2 * NA_KH - 2)
        key = (in_row.tobytes(), dr.tobytes())
        if key not in variants:
            mask = (in_row[:, None, :, None] & in_col[None, :, None, :]).reshape(qr * GRID_W, nbr * GRID_W)
            variants[key] = (len(variants), dr, mask)
        starts.append(bs)
        variant_of.append(variants[key][0])
    ordered = sorted(variants.values(), key=lambda v: v[0])
    return nbr, np.asarray(starts, np.int32), np.asarray(variant_of, np.int32), [(v[1], v[2]) for v in ordered], dc


def _na_bias_tables(rpb, n_lat):
    nbr, starts, variant_of, variants, dc = _na_geometry(n_lat)
    n_dr, n_dc = 2 * NA_KH - 1, 2 * NA_KW - 1
    col_sel = jnp.asarray((dc[:, :, None] == np.arange(n_dc)).astype(np.float32))
    by_col = jnp.einsum("hrc,wuc->hrwu", rpb, col_sel, precision=lax.Precision.HIGHEST)
    tables = []
    for dr, mask in variants:
        row_sel = jnp.asarray((dr[:, :, None] == np.arange(n_dr)).astype(np.float32))
        bias = jnp.einsum("ajr,hrwu->hawju", row_sel, by_col, precision=lax.Precision.HIGHEST)
        bias = bias.reshape(NA_HEADS, NA_Q_ROWS * GRID_W, nbr * GRID_W)
        tables.append(jnp.where(jnp.asarray(mask)[None], bias, NEG_INF))
    return nbr, starts, variant_of, jnp.stack(tables)


NA_Q_TILE = NA_Q_ROWS * GRID_W


def _na_kernel(start_ref, var_ref, q_ref, k_ref, v_ref, bias_ref, o_ref, *, n_lat, band):
    i = pl.program_id(1)
    is_lat = i < n_lat // NA_Q_TILE

    @pl.when(is_lat)
    def _():
        off = pl.multiple_of(start_ref[i] * GRID_W, GRID_W)
        for h in range(NA_HEADS):
            sl = slice(h * HEAD_DIM, (h + 1) * HEAD_DIM)
            local = (k_ref[pl.ds(off, band), sl], v_ref[pl.ds(off, band), sl])
            ctx = (k_ref[n_lat:, sl], v_ref[n_lat:, sl])
            o_ref[:, sl] = _attend(q_ref[:, sl], [local, ctx], [bias_ref[h], None]).astype(BF16)

    @pl.when(jnp.logical_not(is_lat))
    def _():
        for h in range(NA_HEADS):
            sl = slice(h * HEAD_DIM, (h + 1) * HEAD_DIM)
            o_ref[:, sl] = _attend(q_ref[:, sl], [(k_ref[n_lat:, sl], v_ref[n_lat:, sl])], [None]).astype(BF16)


def _na_call(st, qkv, rpb):
    nbr, starts, variant_of, bias = _na_bias_tables(rpb, st.n_lat)
    band = nbr * GRID_W
    n_q = st.nt // NA_Q_TILE
    pad = n_q - starts.shape[0]
    starts = jnp.asarray(np.concatenate([starts, np.zeros(pad, np.int32)]))
    variant_of = jnp.asarray(np.concatenate([variant_of, np.zeros(pad, np.int32)]))
    qkv3 = qkv.reshape(st.bsz, st.nt, 3 * NA_WIDTH)
    grid_spec = pltpu.PrefetchScalarGridSpec(
        num_scalar_prefetch=2,
        grid=(st.bsz, n_q),
        in_specs=[pl.BlockSpec((None, NA_Q_TILE, NA_WIDTH), lambda b, i, s, v: (b, i, 0)),
                  pl.BlockSpec((None, st.nt, NA_WIDTH), lambda b, i, s, v: (b, 0, 1)),
                  pl.BlockSpec((None, st.nt, NA_WIDTH), lambda b, i, s, v: (b, 0, 2)),
                  pl.BlockSpec((None, NA_HEADS, NA_Q_TILE, band), lambda b, i, s, v: (v[i], 0, 0, 0))],
        out_specs=pl.BlockSpec((None, NA_Q_TILE, NA_WIDTH), lambda b, i, s, v: (b, i, 0)),
    )
    o = pl.pallas_call(
        functools.partial(_na_kernel, n_lat=st.n_lat, band=band),
        grid_spec=grid_spec,
        out_shape=jax.ShapeDtypeStruct((st.bsz, st.nt, NA_WIDTH), BF16),
        compiler_params=_params("arbitrary", "arbitrary"),
        name="na_attn",
    )(starts, variant_of, qkv3, qkv3, qkv3, bias)
    return o.reshape(st.t, NA_WIDTH)


def kernel(x, c, ctx, c_ctx, w_mod, b_mod, ln1_g, ln1_b, ln2_g, ln2_b, ab_w_in, ab_w_fnet, ab_q_norm, ab_k_norm,
           ab_w_out, na_w_in, na_rpb, na_w_out, moe_w_router, moe_bias, moe_w_gate, moe_w_up, moe_w_down,
           sh_w_gate, sh_w_up, sh_w_down):
    bsz, n_lat, d = x.shape
    n_ctx = ctx.shape[1]
    depth = w_mod.shape[0]
    st = _Stream(bsz, n_lat, n_ctx, d)
    alpha = (2 * depth) ** 0.25

    cc = jnp.concatenate([c, c_ctx[None, :], jnp.zeros((MOD_ROWS - bsz - 1, d), F32)], axis=0)
    mod = _mod_call(cc, w_mod, b_mod).reshape(depth, MOD_ROWS, 6, d)
    xs = jnp.concatenate([x, ctx], axis=1).reshape(st.t, d)
    h = _modulate_call(st, xs, mod, 0)
    rope = _rope_tables(n_lat, n_ctx)
    n_blocks = _moe_blocks(st)

    for l in range(depth):
        j = l // 2
        if l % 2 == 0:
            f, q, k, v = _ab_in_call(st, h, ab_w_in[j].astype(BF16), ab_q_norm[j], ab_k_norm[j], rope)
            acts = [_fnet_call(st, f, ab_w_fnet[j]), _gqa_call(st, q, k, v)]
            w_out = ab_w_out[j]
        else:
            qkv = _na_in_call(st, h, na_w_in[j].astype(BF16))
            acts = [_na_call(st, qkv, na_rpb[j])]
            w_out = na_w_out[j]
        x1, h2, rec, tile_counts = _out_ln_call(st, acts, w_out, xs, mod, l, ln1_g[l], ln1_b[l],
                                                moe_w_router[l], moe_bias[l], alpha)
        plan, block_e, n_used, fill = _slot_plan(tile_counts, n_blocks)
        rows = _dispatch_call(st, h2, rec, plan, fill, n_blocks)
        ys = _experts_call(st, rows, block_e, n_used, moe_w_gate, moe_w_up, moe_w_down, l, n_blocks)
        xs, h = _ffn_ln_call(st, ys, plan, rec, h2, x1, sh_w_gate[l], sh_w_up[l], sh_w_down[l], mod, l,
                             min(l + 1, depth - 1), ln2_g[l], ln2_b[l], alpha)
    return xs.reshape(bsz, st.nt, d)[:, :n_lat]
```

```python
import functools
import math

import numpy as np
import jax
import jax.numpy as jnp
from jax import lax
from jax.experimental import pallas as pl
from jax.experimental.pallas import tpu as pltpu

GRID_W = 64
HEAD_DIM = 128
FNET_GROUPS = 4
FNET_GROUP_DIM = 64
FNET_WIDTH = FNET_GROUPS * FNET_GROUP_DIM
GQA_Q_HEADS = 6
GQA_KV_HEADS = 2
GQA_GROUP = GQA_Q_HEADS // GQA_KV_HEADS
ROPE_THETA = 10000.0
NA_HEADS = 8
NA_WIDTH = NA_HEADS * HEAD_DIM
NA_KH = 8
NA_KW = 16
NA_Q_ROWS = 2
NEG_INF = -1e30
N_EXPERTS = 64
TOP_K = 8
EXPERT_DIM = 256
ROUTE_SCALE = 2.5
LN_EPS = 1e-6
RMS_EPS = 1e-6
ATTN_SCALE = HEAD_DIM ** -0.5

V7X_LANES = 128
V7X_SUBLANES = 8
V7X_VMEM_LIMIT_BYTES = 56 * 1024 * 1024

TOKEN_TILE = 256
MOD_ROWS = 8

F32 = jnp.float32
BF16 = jnp.bfloat16


def _params(*sem):
    return pltpu.CompilerParams(dimension_semantics=sem, vmem_limit_bytes=V7X_VMEM_LIMIT_BYTES)


def _bdot(a, b):
    return jnp.dot(a, b, preferred_element_type=F32)


def _bdot_t(a, b):
    return lax.dot_general(a, b, (((1,), (1,)), ((), ())), preferred_element_type=F32)


def _split(x):
    hi = x.astype(BF16)
    lo = (x - hi.astype(F32)).astype(BF16)
    return hi, lo


def _dot3(a, b):
    ah, al = _split(a)
    bh, bl = _split(b)
    return _bdot(ah, bh) + (_bdot(ah, bl) + _bdot(al, bh))


def _silu(x):
    return x * jax.nn.sigmoid(x)


def _layer_norm(z, g, b):
    mu = jnp.mean(z, axis=-1, keepdims=True)
    zc = z - mu
    var = jnp.mean(zc * zc, axis=-1, keepdims=True)
    return zc * lax.rsqrt(var + LN_EPS) * g + b


class _Stream:
    def __init__(self, bsz, n_lat, n_ctx, d):
        assert n_lat % TOKEN_TILE == 0 and n_ctx % TOKEN_TILE == 0
        assert bsz < MOD_ROWS
        self.bsz, self.n_lat, self.n_ctx, self.d = bsz, n_lat, n_ctx, d
        self.nt = n_lat + n_ctx
        self.t = bsz * self.nt
        self.tiles_per_sample = self.nt // TOKEN_TILE
        self.lat_tiles = n_lat // TOKEN_TILE
        self.n_tiles = self.t // TOKEN_TILE

    def mod_row(self, tile):
        return jnp.where(tile % self.tiles_per_sample < self.lat_tiles, tile // self.tiles_per_sample, self.bsz)

    def mod_spec(self, layer):
        return pl.BlockSpec((None, None, 6, self.d), lambda t: (layer, self.mod_row(t), 0, 0))

    def tok_spec(self, width):
        return pl.BlockSpec((TOKEN_TILE, width), lambda t: (t, 0))


def _full_spec(shape):
    nd = len(shape)
    return pl.BlockSpec(shape, lambda *_: (0,) * nd)


def _mod_kernel(cc_ref, w_ref, b_ref, o_ref):
    o_ref[...] = _dot3(_silu(cc_ref[...]), w_ref[...]) + b_ref[...]


def _mod_call(cc, w_mod, b_mod):
    depth, d, n = w_mod.shape
    tn = n // 4
    return pl.pallas_call(
        _mod_kernel,
        grid=(depth, n // tn),
        in_specs=[pl.BlockSpec((MOD_ROWS, d), lambda l, j: (0, 0)),
                  pl.BlockSpec((None, d, tn), lambda l, j: (l, 0, j)),
                  pl.BlockSpec((None, 1, tn), lambda l, j: (l, 0, j))],
        out_specs=pl.BlockSpec((None, MOD_ROWS, tn), lambda l, j: (l, 0, j)),
        out_shape=jax.ShapeDtypeStruct((depth, MOD_ROWS, n), F32),
        compiler_params=_params("arbitrary", "arbitrary"),
        name="mod",
    )(cc, w_mod, b_mod.reshape(depth, 1, n))


def _modulate_kernel(x_ref, mod_ref, h_ref):
    h_ref[...] = (x_ref[...] * (1.0 + mod_ref[1:2, :]) + mod_ref[0:1, :]).astype(BF16)


def _modulate_call(st, x, mod, layer):
    return pl.pallas_call(
        _modulate_kernel,
        grid=(st.n_tiles,),
        in_specs=[st.tok_spec(st.d), st.mod_spec(layer)],
        out_specs=st.tok_spec(st.d),
        out_shape=jax.ShapeDtypeStruct((st.t, st.d), BF16),
        compiler_params=_params("arbitrary"),
        name="modulate",
    )(x, mod)


def _rope_tables(n_lat, n_ctx):
    half = HEAD_DIM // 2
    nf = half // 2
    t = np.arange(n_lat)
    inv = ROPE_THETA ** (-(2.0 / half) * np.arange(nf, dtype=np.float64))
    ang_r = (t // GRID_W)[:, None] * inv
    ang_c = (t % GRID_W)[:, None] * inv
    zeros = np.zeros_like(ang_r)
    cos = np.concatenate([np.cos(ang_r), np.cos(ang_r), np.cos(ang_c), np.cos(ang_c)], axis=1)
    sin_fwd = np.concatenate([-np.sin(ang_r), zeros, -np.sin(ang_c), zeros], axis=1)
    sin_bwd = np.concatenate([zeros, np.sin(ang_r), zeros, np.sin(ang_c)], axis=1)
    pad = lambda a, v: np.concatenate([a, np.full((n_ctx, HEAD_DIM), v)], axis=0).astype(np.float32)
    return jnp.asarray(pad(cos, 1.0)), jnp.asarray(pad(sin_fwd, 0.0)), jnp.asarray(pad(sin_bwd, 0.0))


def _ab_in_kernel(h_ref, w_ref, qg_ref, kg_ref, cos_ref, sf_ref, sb_ref, f_ref, q_ref, k_ref, v_ref):
    acc = _bdot(h_ref[...], w_ref[...])
    cos, sf, sb = cos_ref[...], sf_ref[...], sb_ref[...]
    nf = HEAD_DIM // 4

    def norm_rope(xh, gain):
        ms = jnp.mean(xh * xh, axis=-1, keepdims=True)
        y = xh * lax.rsqrt(ms + RMS_EPS) * gain
        return y * cos + pltpu.roll(y, HEAD_DIM - nf, 1) * sf + pltpu.roll(y, nf, 1) * sb

    f_ref[...] = acc[:, :FNET_WIDTH]
    q0 = FNET_WIDTH
    k0 = q0 + GQA_Q_HEADS * HEAD_DIM
    v0 = k0 + GQA_KV_HEADS * HEAD_DIM
    for h in range(GQA_Q_HEADS):
        xh = acc[:, q0 + h * HEAD_DIM:q0 + (h + 1) * HEAD_DIM]
        q_ref[:, h * HEAD_DIM:(h + 1) * HEAD_DIM] = (norm_rope(xh, qg_ref[...]) * ATTN_SCALE).astype(BF16)
    for h in range(GQA_KV_HEADS):
        xh = acc[:, k0 + h * HEAD_DIM:k0 + (h + 1) * HEAD_DIM]
        k_ref[:, h * HEAD_DIM:(h + 1) * HEAD_DIM] = norm_rope(xh, kg_ref[...]).astype(BF16)
    v_ref[...] = acc[:, v0:].astype(BF16)


def _ab_in_call(st, h, w_in, q_gain, k_gain, rope):
    nq = GQA_Q_HEADS * HEAD_DIM
    nkv = GQA_KV_HEADS * HEAD_DIM
    pos_spec = pl.BlockSpec((TOKEN_TILE, HEAD_DIM), lambda t: (t % st.tiles_per_sample, 0))
    return pl.pallas_call(
        _ab_in_kernel,
        grid=(st.n_tiles,),
        in_specs=[st.tok_spec(st.d), _full_spec(w_in.shape), _full_spec((1, HEAD_DIM)), _full_spec((1, HEAD_DIM)),
                  pos_spec, pos_spec, pos_spec],
        out_specs=[st.tok_spec(FNET_WIDTH), st.tok_spec(nq), st.tok_spec(nkv), st.tok_spec(nkv)],
        out_shape=[jax.ShapeDtypeStruct((st.t, FNET_WIDTH), F32), jax.ShapeDtypeStruct((st.t, nq), BF16),
                   jax.ShapeDtypeStruct((st.t, nkv), BF16), jax.ShapeDtypeStruct((st.t, nkv), BF16)],
        compiler_params=_params("arbitrary"),
        name="ab_in",
    )(h, w_in, q_gain.reshape(1, HEAD_DIM), k_gain.reshape(1, HEAD_DIM), *rope)


def _attend(q, keys_values, biases):
    scores = []
    for (k, _), bias in zip(keys_values, biases):
        s = _bdot_t(q, k)
        scores.append(s if bias is None else s + bias)
    m = scores[0].max(axis=-1, keepdims=True)
    for s in scores[1:]:
        m = jnp.maximum(m, s.max(axis=-1, keepdims=True))
    num = None
    den = None
    for s, (_, v) in zip(scores, keys_values):
        p = jnp.exp(s - m)
        pv = _bdot(p.astype(BF16), v)
        ps = p.sum(axis=-1, keepdims=True)
        num = pv if num is None else num + pv
        den = ps if den is None else den + ps
    return num / den


def _gqa_kernel(q_ref, k_ref, v_ref, o_ref, *, n_lat, lat_tiles):
    def run(k, v):
        for h in range(GQA_GROUP):
            sl = slice(h * HEAD_DIM, (h + 1) * HEAD_DIM)
            o_ref[:, sl] = _attend(q_ref[:, sl], [(k, v)], [None]).astype(BF16)

    is_lat = pl.program_id(2) < lat_tiles

    @pl.when(is_lat)
    def _():
        run(k_ref[...], v_ref[...])

    @pl.when(jnp.logical_not(is_lat))
    def _():
        run(k_ref[n_lat:, :], v_ref[n_lat:, :])


def _gqa_call(st, q, k, v):
    gw = GQA_GROUP * HEAD_DIM
    q3 = q.reshape(st.bsz, st.nt, GQA_Q_HEADS * HEAD_DIM)
    k3 = k.reshape(st.bsz, st.nt, GQA_KV_HEADS * HEAD_DIM)
    v3 = v.reshape(st.bsz, st.nt, GQA_KV_HEADS * HEAD_DIM)
    q_spec = pl.BlockSpec((None, TOKEN_TILE, gw), lambda b, g, i: (b, i, g))
    kv_spec = pl.BlockSpec((None, st.nt, HEAD_DIM), lambda b, g, i: (b, 0, g))
    o = pl.pallas_call(
        functools.partial(_gqa_kernel, n_lat=st.n_lat, lat_tiles=st.lat_tiles),
        grid=(st.bsz, GQA_KV_HEADS, st.tiles_per_sample),
        in_specs=[q_spec, kv_spec, kv_spec],
        out_specs=q_spec,
        out_shape=jax.ShapeDtypeStruct(q3.shape, BF16),
        compiler_params=_params("arbitrary", "arbitrary", "arbitrary"),
        name="gqa",
    )(q3, k3, v3)
    return o.reshape(st.t, GQA_Q_HEADS * HEAD_DIM)


def _fft_split(n):
    l1 = 1 << ((n.bit_length() - 1 + 1) // 2)
    assert n % l1 == 0 and n == l1 * (n // l1)
    return l1, n // l1


def _fft_tables(n):
    l1, l2 = _fft_split(n)
    a = np.arange(l1, dtype=np.float64)
    ang1 = 2.0 * np.pi * np.outer(a, a) / l1
    stage1 = np.concatenate([np.cos(ang1), -np.sin(ang1)], axis=0)
    b = np.arange(l2, dtype=np.float64)
    ang_t = 2.0 * np.pi * np.outer(b, a) / n
    tw_cos = np.cos(ang_t)[:, :, None]
    tw_sin = np.sin(ang_t)[:, :, None]
    ang2 = 2.0 * np.pi * np.outer(b, b) / l2
    c2, s2 = np.cos(ang2), np.sin(ang2)
    stage2 = np.block([[c2, s2], [-s2, c2]])
    f32 = lambda x: jnp.asarray(x.astype(np.float32))
    return f32(stage1), f32(tw_cos), f32(tw_sin), f32(stage2)


FNET_GROUPS_PER_SLAB = V7X_LANES // FNET_GROUP_DIM
FNET_SLABS = FNET_WIDTH // V7X_LANES
FNET_UNROLL = 4


def _fnet_channel_tables(n_positions):
    c = np.arange(FNET_GROUP_DIM, dtype=np.float64)
    ang = 2.0 * np.pi * np.outer(c, c) / FNET_GROUP_DIM
    eye = np.eye(FNET_GROUPS_PER_SLAB)
    scale = 1.0 / math.sqrt(n_positions * FNET_GROUP_DIM)
    m = np.concatenate([np.kron(eye, np.cos(ang)), np.kron(eye, np.sin(ang))], axis=0) * scale
    return jnp.asarray(m.astype(np.float32))


def _fnet_part(f_ref, o_ref, a_ref, row0, n, s1_ref, tc_ref, ts_ref, s2_ref, ch_ref, wf_ref):
    l1, l2 = _fft_split(n)
    stage1 = s1_ref[...]
    stage2 = s2_ref[...]
    chan = ch_ref[...]
    wf = wf_ref[...]

    def first(j, carry):
        xs = f_ref[pl.ds(row0 + j, l1, stride=l2), :]
        a = _dot3(stage1, xs)
        ar, ai = a[:l1], a[l1:]
        tc, ts = tc_ref[j], ts_ref[j]
        a_ref[0, pl.ds(pl.multiple_of(j * l1, l1), l1), :] = ar * tc + ai * ts
        a_ref[1, pl.ds(pl.multiple_of(j * l1, l1), l1), :] = ai * tc - ar * ts
        return carry

    lax.fori_loop(0, l2, first, 0, unroll=FNET_UNROLL)

    def second(j, carry):
        br = a_ref[0, pl.ds(j, l2, stride=l1), :]
        bi = a_ref[1, pl.ds(j, l2, stride=l1), :]
        p = _dot3(stage2, jnp.concatenate([br, bi], axis=0))
        re = _dot3(jnp.concatenate([p[:l2], p[l2:]], axis=1), chan)
        o_ref[pl.ds(row0 + j, l2, stride=l1), :] = _bdot(re.astype(BF16), wf)
        return carry

    lax.fori_loop(0, l1, second, 0, unroll=FNET_UNROLL)


def _fnet_kernel(f_ref, s1l, tcl, tsl, s2l, chl, s1c, tcc, tsc, s2c, chc, wf_ref, o_ref, a_ref, *, n_lat, n_ctx):
    _fnet_part(f_ref, o_ref, a_ref, 0, n_lat, s1l, tcl, tsl, s2l, chl, wf_ref)
    _fnet_part(f_ref, o_ref, a_ref, n_lat, n_ctx, s1c, tcc, tsc, s2c, chc, wf_ref)


def _fnet_call(st, f, w_fnet):
    gps = FNET_GROUPS_PER_SLAB
    eye = jnp.eye(gps, dtype=F32)
    wg = w_fnet.reshape(FNET_SLABS, gps, FNET_GROUP_DIM, FNET_GROUP_DIM)
    wf = (eye[None, :, None, :, None] * wg[:, :, :, None, :]).reshape(FNET_SLABS, V7X_LANES, V7X_LANES).astype(BF16)
    consts = (*_fft_tables(st.n_lat), _fnet_channel_tables(st.n_lat),
              *_fft_tables(st.n_ctx), _fnet_channel_tables(st.n_ctx))
    f3 = f.reshape(st.bsz, st.nt, FNET_WIDTH)
    blk = pl.BlockSpec((None, st.nt, V7X_LANES), lambda b, s: (b, 0, s))
    o = pl.pallas_call(
        functools.partial(_fnet_kernel, n_lat=st.n_lat, n_ctx=st.n_ctx),
        grid=(st.bsz, FNET_SLABS),
        in_specs=[blk] + [_full_spec(c.shape) for c in consts]
        + [pl.BlockSpec((None, V7X_LANES, V7X_LANES), lambda b, s: (s, 0, 0))],
        out_specs=blk,
        out_shape=jax.ShapeDtypeStruct(f3.shape, F32),
        scratch_shapes=[pltpu.VMEM((2, st.n_lat, V7X_LANES), F32)],
        compiler_params=_params("arbitrary", "arbitrary"),
        name="fnet",
    )(f3, *consts, wf)
    return o.reshape(st.t, FNET_WIDTH)


SEG_ALIGN = V7X_SUBLANES
PAIRS_PER_TILE = TOKEN_TILE * TOP_K
LOCAL_ROWS = PAIRS_PER_TILE + N_EXPERTS * SEG_ALIGN
ROUTE_LANES = V7X_LANES


def _route(h2, wr, e_bias):
    tm = h2.shape[0]
    scores = jax.nn.sigmoid(_dot3(h2, wr))
    sel = scores + e_bias
    lane = lax.broadcasted_iota(jnp.int32, sel.shape, 1).astype(F32)
    hits = []
    for _ in range(TOP_K):
        best = sel.max(axis=-1, keepdims=True)
        first = jnp.where(sel == best, lane, float(N_EXPERTS)).min(axis=-1, keepdims=True)
        hit = lane == first
        hits.append(hit)
        sel = jnp.where(hit, -jnp.inf, sel)
    chosen = hits[0]
    for hit in hits[1:]:
        chosen = jnp.logical_or(chosen, hit)
    chosen_f = jnp.where(chosen, 1.0, 0.0)
    counts = chosen_f.sum(axis=0, keepdims=True)
    seg_len = jnp.ceil(counts * (1.0 / SEG_ALIGN)) * SEG_ALIGN
    er = lax.broadcasted_iota(jnp.int32, (N_EXPERTS, N_EXPERTS), 0)
    ec = lax.broadcasted_iota(jnp.int32, (N_EXPERTS, N_EXPERTS), 1)
    lower_experts = jnp.where(er < ec, 1.0, 0.0).astype(BF16)
    seg_start = _bdot(jnp.broadcast_to(seg_len, (V7X_SUBLANES, N_EXPERTS)).astype(BF16), lower_experts)[0:1]
    r = lax.broadcasted_iota(jnp.int32, (tm, tm), 0)
    c = lax.broadcasted_iota(jnp.int32, (tm, tm), 1)
    earlier = jnp.where(c < r, 1.0, 0.0).astype(BF16)
    row_all = _bdot(earlier, chosen_f.astype(BF16)) + seg_start
    rec_lane = lax.broadcasted_iota(jnp.int32, (tm, ROUTE_LANES), 1)
    rec = jnp.zeros((tm, ROUTE_LANES), F32)
    raw = []
    for k, hit in enumerate(hits):
        rec = jnp.where(rec_lane == k, jnp.where(hit, row_all, 0.0).sum(axis=-1, keepdims=True), rec)
        raw.append(jnp.where(hit, scores, 0.0).sum(axis=-1, keepdims=True))
    total = raw[0]
    for w in raw[1:]:
        total = total + w
    for k, w in enumerate(raw):
        rec = jnp.where(rec_lane == TOP_K + k, w / total * ROUTE_SCALE, rec)
    return rec, counts.astype(jnp.int32)


def _out_ln_kernel(*refs, n_in, alpha):
    a_refs = refs[:n_in]
    w_refs = refs[n_in:2 * n_in]
    x_ref, mod_ref, g_ref, b_ref, wr_ref, eb_ref, x1_ref, h2_ref, rec_ref, cnt_ref = refs[2 * n_in:]
    y = None
    for a_ref, w_ref in zip(a_refs, w_refs):
        part = _bdot(a_ref[...].astype(BF16), w_ref[...])
        y = part if y is None else y + part
    z = alpha * x_ref[...] + mod_ref[2:3, :] * y
    x1 = _layer_norm(z, g_ref[...], b_ref[...])
    x1_ref[...] = x1
    h2 = x1 * (1.0 + mod_ref[4:5, :]) + mod_ref[3:4, :]
    h2_ref[...] = h2.astype(BF16)
    rec, counts = _route(h2, wr_ref[...], eb_ref[...])
    rec_ref[...] = rec
    cnt_ref[...] = counts


def _out_ln_call(st, acts, w_out, x, mod, layer, ln_g, ln_b, w_router, e_bias, alpha):
    ws, r0 = [], 0
    for a in acts:
        ws.append(w_out[r0:r0 + a.shape[1]].astype(BF16))
        r0 += a.shape[1]
    assert r0 == w_out.shape[0]
    row = lambda v: v.reshape(1, -1)
    return pl.pallas_call(
        functools.partial(_out_ln_kernel, n_in=len(acts), alpha=alpha),
        grid=(st.n_tiles,),
        in_specs=[st.tok_spec(a.shape[1]) for a in acts] + [_full_spec(w.shape) for w in ws]
        + [st.tok_spec(st.d), st.mod_spec(layer), _full_spec((1, st.d)), _full_spec((1, st.d)),
           _full_spec(w_router.shape), _full_spec((1, N_EXPERTS))],
        out_specs=[st.tok_spec(st.d), st.tok_spec(st.d), st.tok_spec(ROUTE_LANES),
                   pl.BlockSpec((None, 1, N_EXPERTS), lambda t: (t, 0, 0))],
        out_shape=[jax.ShapeDtypeStruct((st.t, st.d), F32), jax.ShapeDtypeStruct((st.t, st.d), BF16),
                   jax.ShapeDtypeStruct((st.t, ROUTE_LANES), F32),
                   jax.ShapeDtypeStruct((st.n_tiles, 1, N_EXPERTS), jnp.int32)],
        compiler_params=_params("arbitrary"),
        name="out_ln",
    )(*acts, *ws, x, mod, row(ln_g), row(ln_b), w_router, row(e_bias))


MOE_BLOCK = 512
SEG_SIZES = tuple(TOKEN_TILE >> s for s in range((TOKEN_TILE // SEG_ALIGN).bit_length()))
ROW_CHUNK = 256
PLAN_WIDTH = 4 * N_EXPERTS
WAIT_SIZES = tuple(1 << b for b in range((LOCAL_ROWS - 1).bit_length() - 1, SEG_ALIGN.bit_length() - 2, -1))


def _moe_blocks(st):
    return (st.t * TOP_K + st.n_tiles * N_EXPERTS * (SEG_ALIGN - 1)) // MOE_BLOCK + N_EXPERTS


def _slot_plan(tile_counts, n_blocks):
    n_tiles = tile_counts.shape[0]
    cnt = tile_counts.reshape(n_tiles, N_EXPERTS)
    seg = (cnt + SEG_ALIGN - 1) // SEG_ALIGN * SEG_ALIGN
    local_start = jnp.cumsum(seg, axis=1) - seg
    tiles_before = jnp.cumsum(seg, axis=0) - seg
    total = seg.sum(axis=0)
    padded = (total + MOE_BLOCK - 1) // MOE_BLOCK * MOE_BLOCK
    pad_end = jnp.cumsum(padded)
    pad_start = pad_end - padded
    sorted_start = pad_start[None, :] + tiles_before
    tile_rows = jnp.broadcast_to(seg.sum(axis=1, keepdims=True), seg.shape)
    plan = jnp.concatenate([seg, local_start, sorted_start, tile_rows], axis=1).astype(jnp.int32)
    plan = plan.reshape(n_tiles, 1, PLAN_WIDTH)
    n_used = jnp.maximum(pad_end[-1] // MOE_BLOCK, 1)
    first_row = jnp.arange(n_blocks, dtype=jnp.int32) * MOE_BLOCK
    block_e = jnp.minimum((first_row[:, None] >= pad_end[None, :]).sum(-1), N_EXPERTS - 1).astype(jnp.int32)
    ends_expert = ((first_row[:, None] + MOE_BLOCK == pad_end[None, :]) & (padded[None, :] > 0)).any(-1)
    fill = (ends_expert | (first_row >= pad_end[-1])).astype(jnp.int32)
    return plan, block_e, n_used.astype(jnp.int32).reshape(1), fill


def _for_each_piece(plan_ref, fn):
    def per_expert(e, carry):
        seg = plan_ref[0, e]
        local = plan_ref[0, N_EXPERTS + e]
        dst = plan_ref[0, 2 * N_EXPERTS + e]
        done = 0
        for size in SEG_SIZES:
            take = (seg & size) != 0

            @pl.when(take)
            def _(done=done, size=size):
                fn(pl.multiple_of(local + done, SEG_ALIGN), pl.multiple_of(dst + done, SEG_ALIGN), size)

            done = done + jnp.where(take, size, 0)
        return carry

    lax.fori_loop(0, N_EXPERTS, per_expert, 0)


def _wait_tile_rows(plan_ref, make_copy):
    rows = plan_ref[0, 3 * N_EXPERTS]
    for size in WAIT_SIZES:
        @pl.when((rows & size) != 0)
        def _(size=size):
            make_copy(size).wait()


def _pack_halves(x):
    bits = pltpu.bitcast(x, jnp.uint32)
    half = x.shape[1] // 2
    return (bits[:, :half] >> 16) | (bits[:, half:] & jnp.uint32(0xFFFF0000))


def _unpack_halves(w):
    lo = pltpu.bitcast(w << 16, F32).astype(BF16)
    hi = pltpu.bitcast(w & jnp.uint32(0xFFFF0000), F32).astype(BF16)
    return lo, hi


def _dispatch_kernel(fill_ref, plan_ref, plan1_ref, plan2_ref, rec_ref, h_ref, xs_ref, loc_ref, zero_ref, zsem, sem,
                     *, n_blocks, n_tiles):
    i = pl.program_id(0)
    slot = i % 2

    @pl.when(i == 0)
    def _():
        zero_ref[...] = jnp.zeros_like(zero_ref)

        def fill(b):
            return pltpu.make_async_copy(zero_ref, xs_ref.at[pl.ds(pl.multiple_of(b * MOE_BLOCK, MOE_BLOCK), MOE_BLOCK)],
                                         zsem)

        def start(b, c):
            @pl.when(fill_ref[b] > 0)
            def _():
                fill(b).start()
            return c

        def wait(b, c):
            @pl.when(fill_ref[b] > 0)
            def _():
                fill(b).wait()
            return c

        lax.fori_loop(0, n_blocks, start, 0)
        lax.fori_loop(0, n_blocks, wait, 0)

    def piece(s):
        def copy(local, dst, size):
            return pltpu.make_async_copy(loc_ref.at[s, pl.ds(local, size)], xs_ref.at[pl.ds(dst, size)], sem.at[s])
        return copy

    def rows_done(s):
        return lambda size: piece(s)(0, 0, size)

    @pl.when(i >= 2)
    def _():
        _wait_tile_rows(plan2_ref, rows_done(slot))

    rows_of = rec_ref[...].T
    x = h_ref[...]
    for c0 in range(0, LOCAL_ROWS, ROW_CHUNK):
        local_row = (lax.broadcasted_iota(jnp.int32, (ROW_CHUNK, TOKEN_TILE), 0) + c0).astype(F32)
        onehot = jnp.zeros((ROW_CHUNK, TOKEN_TILE), F32)
        for k in range(TOP_K):
            onehot = jnp.where(local_row == rows_of[k:k + 1, :], 1.0, onehot)
        loc_ref[slot, c0:c0 + ROW_CHUNK, :] = _pack_halves(_bdot(onehot.astype(BF16), x))

    _for_each_piece(plan_ref, lambda *a: piece(slot)(*a).start())

    @pl.when(i == n_tiles - 1)
    def _():
        if n_tiles >= 2:
            _wait_tile_rows(plan1_ref, rows_done(1 - slot))
        _wait_tile_rows(plan_ref, rows_done(slot))


def _dispatch_call(st, h2, rec, plan, fill, n_blocks):
    half = st.d // 2
    plan_spec = lambda back: pl.BlockSpec((None, 1, PLAN_WIDTH), lambda t, *_: (jnp.maximum(t - back, 0), 0, 0),
                                          memory_space=pltpu.SMEM)
    grid_spec = pltpu.PrefetchScalarGridSpec(
        num_scalar_prefetch=1,
        grid=(st.n_tiles,),
        in_specs=[plan_spec(0), plan_spec(1), plan_spec(2),
                  pl.BlockSpec((TOKEN_TILE, ROUTE_LANES), lambda t, *_: (t, 0)),
                  pl.BlockSpec((TOKEN_TILE, st.d), lambda t, *_: (t, 0))],
        out_specs=pl.BlockSpec(memory_space=pl.ANY),
        scratch_shapes=[pltpu.VMEM((2, LOCAL_ROWS, half), jnp.uint32), pltpu.VMEM((MOE_BLOCK, half), jnp.uint32),
                        pltpu.SemaphoreType.DMA, pltpu.SemaphoreType.DMA((2,))],
    )
    return pl.pallas_call(
        functools.partial(_dispatch_kernel, n_blocks=n_blocks, n_tiles=st.n_tiles),
        grid_spec=grid_spec,
        out_shape=jax.ShapeDtypeStruct((n_blocks * MOE_BLOCK, half), jnp.uint32),
        compiler_params=_params("arbitrary"),
        name="moe_dispatch",
    )(fill, plan, plan, plan, rec, h2)


def _experts_kernel(be_ref, nu_ref, x_ref, wg_ref, wu_ref, wd_ref, y_ref, wg_s, wu_s, wd_s):
    i = pl.program_id(0)
    live = i < nu_ref[0]
    prev = be_ref[jnp.maximum(i, 1) - 1]

    @pl.when(jnp.logical_and(live, jnp.logical_or(i == 0, be_ref[i] != prev)))
    def _():
        wg_s[...] = wg_ref[...].astype(BF16)
        wu_s[...] = wu_ref[...].astype(BF16)
        wd_s[...] = wd_ref[...].astype(BF16)

    @pl.when(live)
    def _():
        half = wg_s.shape[0] // 2
        lo, hi = _unpack_halves(x_ref[...])
        gate = _bdot(lo, wg_s[:half, :]) + _bdot(hi, wg_s[half:, :])
        up = _bdot(lo, wu_s[:half, :]) + _bdot(hi, wu_s[half:, :])
        y = _bdot((_silu(gate) * up).astype(BF16), wd_s[...])
        y_ref[...] = _pack_halves(y.astype(BF16).astype(F32))

    @pl.when(jnp.logical_not(live))
    def _():
        y_ref[...] = jnp.zeros_like(y_ref)


def _experts_call(st, xs, block_e, n_used, w_gate, w_up, w_down, layer, n_blocks):
    half = st.d // 2
    w_in_spec = pl.BlockSpec((None, None, st.d, EXPERT_DIM), lambda i, be, nu: (layer, be[i], 0, 0))
    grid_spec = pltpu.PrefetchScalarGridSpec(
        num_scalar_prefetch=2,
        grid=(n_blocks,),
        in_specs=[pl.BlockSpec((MOE_BLOCK, half), lambda i, be, nu: (i, 0)),
                  w_in_spec, w_in_spec,
                  pl.BlockSpec((None, None, EXPERT_DIM, st.d), lambda i, be, nu: (layer, be[i], 0, 0))],
        out_specs=pl.BlockSpec((MOE_BLOCK, half), lambda i, be, nu: (i, 0)),
        scratch_shapes=[pltpu.VMEM((st.d, EXPERT_DIM), BF16), pltpu.VMEM((st.d, EXPERT_DIM), BF16),
                        pltpu.VMEM((EXPERT_DIM, st.d), BF16)],
    )
    return pl.pallas_call(
        _experts_kernel,
        grid_spec=grid_spec,
        out_shape=jax.ShapeDtypeStruct(xs.shape, jnp.uint32),
        compiler_params=_params("arbitrary"),
        name="moe_experts",
    )(block_e, n_used, xs, w_gate, w_up, w_down)


def _ffn_ln_kernel(plan_ref, plan_next_ref, rec_ref, h2_ref, x1_ref, sg_ref, su_ref, sd_ref, mod_ref, modn_ref, g_ref,
                   b_ref, ys_ref, x2_ref, hn_ref, loc_ref, sem, *, alpha, n_tiles):
    i = pl.program_id(0)
    slot = i % 2

    def piece(s):
        def copy(local, src, size):
            return pltpu.make_async_copy(ys_ref.at[pl.ds(src, size)], loc_ref.at[s, pl.ds(local, size)], sem.at[s])
        return copy

    @pl.when(i == 0)
    def _():
        loc_ref[...] = jnp.zeros_like(loc_ref)
        _for_each_piece(plan_ref, lambda *a: piece(0)(*a).start())

    @pl.when(i + 1 < n_tiles)
    def _():
        _for_each_piece(plan_next_ref, lambda *a: piece(1 - slot)(*a).start())

    h2 = h2_ref[...]
    a = _silu(_bdot(h2, sg_ref[...])) * _bdot(h2, su_ref[...])
    shared = _bdot(a.astype(BF16), sd_ref[...])

    _wait_tile_rows(plan_ref, lambda size: piece(slot)(0, 0, size))
    rec = rec_ref[...]
    half = loc_ref.shape[2]
    routed_lo = jnp.zeros((TOKEN_TILE, half), F32)
    routed_hi = jnp.zeros((TOKEN_TILE, half), F32)
    for c0 in range(0, LOCAL_ROWS, ROW_CHUNK):
        local_row = (lax.broadcasted_iota(jnp.int32, (TOKEN_TILE, ROW_CHUNK), 1) + c0).astype(F32)
        weight = jnp.zeros((TOKEN_TILE, ROW_CHUNK), F32)
        for k in range(TOP_K):
            weight = jnp.where(local_row == rec[:, k:k + 1], rec[:, TOP_K + k:TOP_K + k + 1], weight)
        lo, hi = _unpack_halves(loc_ref[slot, c0:c0 + ROW_CHUNK, :])
        weight = weight.astype(BF16)
        routed_lo = routed_lo + _bdot(weight, lo)
        routed_hi = routed_hi + _bdot(weight, hi)
    ff = jnp.concatenate([routed_lo, routed_hi], axis=1) + shared
    z = alpha * x1_ref[...] + mod_ref[5:6, :] * ff
    x2 = _layer_norm(z, g_ref[...], b_ref[...])
    x2_ref[...] = x2
    hn_ref[...] = (x2 * (1.0 + modn_ref[1:2, :]) + modn_ref[0:1, :]).astype(BF16)


def _ffn_ln_call(st, ys, plan, rec, h2, x1, s_gate, s_up, s_down, mod, layer, next_layer, ln_g, ln_b, alpha):
    row = lambda v: v.reshape(1, -1)
    sg, su, sd = s_gate.astype(BF16), s_up.astype(BF16), s_down.astype(BF16)
    last = st.n_tiles - 1
    return pl.pallas_call(
        functools.partial(_ffn_ln_kernel, alpha=alpha, n_tiles=st.n_tiles),
        grid=(st.n_tiles,),
        in_specs=[pl.BlockSpec((None, 1, PLAN_WIDTH), lambda t: (t, 0, 0), memory_space=pltpu.SMEM),
                  pl.BlockSpec((None, 1, PLAN_WIDTH), lambda t: (jnp.minimum(t + 1, last), 0, 0),
                               memory_space=pltpu.SMEM),
                  st.tok_spec(ROUTE_LANES), st.tok_spec(st.d), st.tok_spec(st.d),
                  _full_spec(sg.shape), _full_spec(su.shape), _full_spec(sd.shape),
                  st.mod_spec(layer), st.mod_spec(next_layer), _full_spec((1, st.d)), _full_spec((1, st.d)),
                  pl.BlockSpec(memory_space=pl.ANY)],
        out_specs=[st.tok_spec(st.d), st.tok_spec(st.d)],
        out_shape=[jax.ShapeDtypeStruct((st.t, st.d), F32), jax.ShapeDtypeStruct((st.t, st.d), BF16)],
        scratch_shapes=[pltpu.VMEM((2, LOCAL_ROWS, st.d // 2), jnp.uint32), pltpu.SemaphoreType.DMA((2,))],
        compiler_params=_params("arbitrary"),
        name="ffn_ln",
    )(plan, plan, rec, h2, x1, sg, su, sd, mod, mod, row(ln_g), row(ln_b), ys)


def _na_in_kernel(h_ref, w_ref, o_ref):
    h = h_ref[...]
    o_ref[:, :NA_WIDTH] = (_bdot(h, w_ref[:, :NA_WIDTH]) * ATTN_SCALE).astype(BF16)
    o_ref[:, NA_WIDTH:] = _bdot(h, w_ref[:, NA_WIDTH:]).astype(BF16)


def _na_in_call(st, h, w_in):
    return pl.pallas_call(
        _na_in_kernel,
        grid=(st.n_tiles,),
        in_specs=[st.tok_spec(st.d), _full_spec(w_in.shape)],
        out_specs=st.tok_spec(3 * NA_WIDTH),
        out_shape=jax.ShapeDtypeStruct((st.t, 3 * NA_WIDTH), BF16),
        compiler_params=_params("arbitrary"),
        name="na_in",
    )(h, w_in)


def _na_geometry(n_lat):
    rows = n_lat // GRID_W
    kh, kw, qr = min(NA_KH, rows), min(NA_KW, GRID_W), NA_Q_ROWS
    nbr = min(qr + kh - 1, rows)
    col = np.arange(GRID_W)
    col_start = np.clip(col - kw // 2, 0, GRID_W - kw)
    in_col = (col[None, :] >= col_start[:, None]) & (col[None, :] < col_start[:, None] + kw)
    dc = np.clip(col[None, :] - col[:, None] + NA_KW - 1, 0, 2 * NA_KW - 2)
    starts, variant_of, variants = [], [], {}
    for i in range(rows // qr):
        qrow = i * qr + np.arange(qr)
        rstart = np.clip(qrow - kh // 2, 0, rows - kh)
        bs = min(int(rstart[0]), rows - nbr)
        krow = bs + np.arange(nbr)
        in_row = (krow[None, :] >= rstart[:, None]) & (krow[None, :] < rstart[:, None] + kh)
        dr = np.clip(krow[None, :] - qrow[:, None] + NA_KH - 1, 0, 2 * NA_KH - 2)
        key = (in_row.tobytes(), dr.tobytes())
        if key not in variants:
            mask = (in_row[:, None, :, None] & in_col[None, :, None, :]).reshape(qr * GRID_W, nbr * GRID_W)
            variants[key] = (len(variants), dr, mask)
        starts.append(bs)
        variant_of.append(variants[key][0])
    ordered = sorted(variants.values(), key=lambda v: v[0])
    return nbr, np.asarray(starts, np.int32), np.asarray(variant_of, np.int32), [(v[1], v[2]) for v in ordered], dc


def _na_bias_tables(rpb, n_lat):
    nbr, starts, variant_of, variants, dc = _na_geometry(n_lat)
    n_dr, n_dc = 2 * NA_KH - 1, 2 * NA_KW - 1
    col_sel = jnp.asarray((dc[:, :, None] == np.arange(n_dc)).astype(np.float32))
    by_col = jnp.einsum("hrc,wuc->hrwu", rpb, col_sel, precision=lax.Precision.HIGHEST)
    tables = []
    for dr, mask in variants:
        row_sel = jnp.asarray((dr[:, :, None] == np.arange(n_dr)).astype(np.float32))
        bias = jnp.einsum("ajr,hrwu->hawju", row_sel, by_col, precision=lax.Precision.HIGHEST)
        bias = bias.reshape(NA_HEADS, NA_Q_ROWS * GRID_W, nbr * GRID_W)
        tables.append(jnp.where(jnp.asarray(mask)[None], bias, NEG_INF))
    return nbr, starts, variant_of, jnp.stack(tables)


NA_Q_TILE = NA_Q_ROWS * GRID_W


def _na_kernel(start_ref, var_ref, q_ref, k_ref, v_ref, bias_ref, o_ref, *, n_lat, band):
    i = pl.program_id(1)
    is_lat = i < n_lat // NA_Q_TILE

    @pl.when(is_lat)
    def _():
        off = pl.multiple_of(start_ref[i] * GRID_W, GRID_W)
        for h in range(NA_HEADS):
            sl = slice(h * HEAD_DIM, (h + 1) * HEAD_DIM)
            local = (k_ref[pl.ds(off, band), sl], v_ref[pl.ds(off, band), sl])
            ctx = (k_ref[n_lat:, sl], v_ref[n_lat:, sl])
            o_ref[:, sl] = _attend(q_ref[:, sl], [local, ctx], [bias_ref[h], None]).astype(BF16)

    @pl.when(jnp.logical_not(is_lat))
    def _():
        for h in range(NA_HEADS):
            sl = slice(h * HEAD_DIM, (h + 1) * HEAD_DIM)
            o_ref[:, sl] = _attend(q_ref[:, sl], [(k_ref[n_lat:, sl], v_ref[n_lat:, sl])], [None]).astype(BF16)


def _na_call(st, qkv, rpb):
    nbr, starts, variant_of, bias = _na_bias_tables(rpb, st.n_lat)
    band = nbr * GRID_W
    n_q = st.nt // NA_Q_TILE
    pad = n_q - starts.shape[0]
    starts = jnp.asarray(np.concatenate([starts, np.zeros(pad, np.int32)]))
    variant_of = jnp.asarray(np.concatenate([variant_of, np.zeros(pad, np.int32)]))
    qkv3 = qkv.reshape(st.bsz, st.nt, 3 * NA_WIDTH)
    grid_spec = pltpu.PrefetchScalarGridSpec(
        num_scalar_prefetch=2,
        grid=(st.bsz, n_q),
        in_specs=[pl.BlockSpec((None, NA_Q_TILE, NA_WIDTH), lambda b, i, s, v: (b, i, 0)),
                  pl.BlockSpec((None, st.nt, NA_WIDTH), lambda b, i, s, v: (b, 0, 1)),
                  pl.BlockSpec((None, st.nt, NA_WIDTH), lambda b, i, s, v: (b, 0, 2)),
                  pl.BlockSpec((None, NA_HEADS, NA_Q_TILE, band), lambda b, i, s, v: (v[i], 0, 0, 0))],
        out_specs=pl.BlockSpec((None, NA_Q_TILE, NA_WIDTH), lambda b, i, s, v: (b, i, 0)),
    )
    o = pl.pallas_call(
        functools.partial(_na_kernel, n_lat=st.n_lat, band=band),
        grid_spec=grid_spec,
        out_shape=jax.ShapeDtypeStruct((st.bsz, st.nt, NA_WIDTH), BF16),
        compiler_params=_params("arbitrary", "arbitrary"),
        name="na_attn",
    )(starts, variant_of, qkv3, qkv3, qkv3, bias)
    return o.reshape(st.t, NA_WIDTH)


def kernel(x, c, ctx, c_ctx, w_mod, b_mod, ln1_g, ln1_b, ln2_g, ln2_b, ab_w_in, ab_w_fnet, ab_q_norm, ab_k_norm,
           ab_w_out, na_w_in, na_rpb, na_w_out, moe_w_router, moe_bias, moe_w_gate, moe_w_up, moe_w_down,
           sh_w_gate, sh_w_up, sh_w_down):
    bsz, n_lat, d = x.shape
    n_ctx = ctx.shape[1]
    depth = w_mod.shape[0]
    st = _Stream(bsz, n_lat, n_ctx, d)
    alpha = (2 * depth) ** 0.25

    cc = jnp.concatenate([c, c_ctx[None, :], jnp.zeros((MOD_ROWS - bsz - 1, d), F32)], axis=0)
    mod = _mod_call(cc, w_mod, b_mod).reshape(depth, MOD_ROWS, 6, d)
    xs = jnp.concatenate([x, ctx], axis=1).reshape(st.t, d)
    h = _modulate_call(st, xs, mod, 0)
    rope = _rope_tables(n_lat, n_ctx)
    n_blocks = _moe_blocks(st)

    for l in range(depth):
        j = l // 2
        if l % 2 == 0:
            f, q, k, v = _ab_in_call(st, h, ab_w_in[j].astype(BF16), ab_q_norm[j], ab_k_norm[j], rope)
            acts = [_fnet_call(st, f, ab_w_fnet[j]), _gqa_call(st, q, k, v)]
            w_out = ab_w_out[j]
        else:
            qkv = _na_in_call(st, h, na_w_in[j].astype(BF16))
            acts = [_na_call(st, qkv, na_rpb[j])]
            w_out = na_w_out[j]
        x1, h2, rec, tile_counts = _out_ln_call(st, acts, w_out, xs, mod, l, ln1_g[l], ln1_b[l],
                                                moe_w_router[l], moe_bias[l], alpha)
        plan, block_e, n_used, fill = _slot_plan(tile_counts, n_blocks)
        rows = _dispatch_call(st, h2, rec, plan, fill, n_blocks)
        ys = _experts_call(st, rows, block_e, n_used, moe_w_gate, moe_w_up, moe_w_down, l, n_blocks)
        xs, h = _ffn_ln_call(st, ys, plan, rec, h2, x1, sh_w_gate[l], sh_w_up[l], sh_w_down[l], mod, l,
                             min(l + 1, depth - 1), ln2_g[l], ln2_b[l], alpha)
    return xs.reshape(bsz, st.nt, d)[:, :n_lat]
```

```python
import functools
import math

import numpy as np
import jax
import jax.numpy as jnp
from jax import lax
from jax.experimental import pallas as pl
from jax.experimental.pallas import tpu as pltpu

GRID_W = 64
HEAD_DIM = 128
FNET_GROUPS = 4
FNET_GROUP_DIM = 64
FNET_WIDTH = FNET_GROUPS * FNET_GROUP_DIM
GQA_Q_HEADS = 6
GQA_KV_HEADS = 2
GQA_GROUP = GQA_Q_HEADS // GQA_KV_HEADS
ROPE_THETA = 10000.0
NA_HEADS = 8
NA_WIDTH = NA_HEADS * HEAD_DIM
NA_KH = 8
NA_KW = 16
NA_Q_ROWS = 2
NEG_INF = -1e30
N_EXPERTS = 64
TOP_K = 8
EXPERT_DIM = 256
ROUTE_SCALE = 2.5
LN_EPS = 1e-6
RMS_EPS = 1e-6
ATTN_SCALE = HEAD_DIM ** -0.5
LOG2_E = math.log2(math.e)
SCORE_SCALE = ATTN_SCALE * LOG2_E

V7X_LANES = 128
V7X_SUBLANES = 8
V7X_VMEM_LIMIT_BYTES = 56 * 1024 * 1024

TOKEN_TILE = 256
MOD_ROWS = 8

F32 = jnp.float32
BF16 = jnp.bfloat16


def _params(*sem):
    return pltpu.CompilerParams(dimension_semantics=sem, vmem_limit_bytes=V7X_VMEM_LIMIT_BYTES)


def _bdot(a, b):
    return jnp.dot(a, b, preferred_element_type=F32)


def _bdot_t(a, b):
    return lax.dot_general(a, b, (((1,), (1,)), ((), ())), preferred_element_type=F32)


def _split(x):
    hi = x.astype(BF16)
    lo = (x - hi.astype(F32)).astype(BF16)
    return hi, lo


def _dot3(a, b):
    ah, al = _split(a)
    bh, bl = _split(b)
    return _bdot(ah, bh) + (_bdot(ah, bl) + _bdot(al, bh))


def _silu(x):
    return x * jax.nn.sigmoid(x)


def _layer_norm(z, g, b):
    mu = jnp.mean(z, axis=-1, keepdims=True)
    zc = z - mu
    var = jnp.mean(zc * zc, axis=-1, keepdims=True)
    return zc * lax.rsqrt(var + LN_EPS) * g + b


class _Stream:
    def __init__(self, bsz, n_lat, n_ctx, d):
        assert n_lat % TOKEN_TILE == 0 and n_ctx % TOKEN_TILE == 0
        assert bsz < MOD_ROWS
        self.bsz, self.n_lat, self.n_ctx, self.d = bsz, n_lat, n_ctx, d
        self.nt = n_lat + n_ctx
        self.t = bsz * self.nt
        self.tiles_per_sample = self.nt // TOKEN_TILE
        self.lat_tiles = n_lat // TOKEN_TILE
        self.n_tiles = self.t // TOKEN_TILE

    def mod_row(self, tile):
        return jnp.where(tile % self.tiles_per_sample < self.lat_tiles, tile // self.tiles_per_sample, self.bsz)

    def mod_spec(self, layer):
        return pl.BlockSpec((None, None, 6, self.d), lambda t: (layer, self.mod_row(t), 0, 0))

    def tok_spec(self, width):
        return pl.BlockSpec((TOKEN_TILE, width), lambda t: (t, 0))


def _full_spec(shape):
    nd = len(shape)
    return pl.BlockSpec(shape, lambda *_: (0,) * nd)


def _mod_kernel(cc_ref, w_ref, b_ref, o_ref):
    o_ref[...] = _dot3(_silu(cc_ref[...]), w_ref[...]) + b_ref[...]


def _mod_call(cc, w_mod, b_mod):
    depth, d, n = w_mod.shape
    tn = n // 4
    return pl.pallas_call(
        _mod_kernel,
        grid=(depth, n // tn),
        in_specs=[pl.BlockSpec((MOD_ROWS, d), lambda l, j: (0, 0)),
                  pl.BlockSpec((None, d, tn), lambda l, j: (l, 0, j)),
                  pl.BlockSpec((None, 1, tn), lambda l, j: (l, 0, j))],
        out_specs=pl.BlockSpec((None, MOD_ROWS, tn), lambda l, j: (l, 0, j)),
        out_shape=jax.ShapeDtypeStruct((depth, MOD_ROWS, n), F32),
        compiler_params=_params("arbitrary", "arbitrary"),
        name="mod",
    )(cc, w_mod, b_mod.reshape(depth, 1, n))


def _modulate_kernel(x_ref, mod_ref, h_ref):
    h_ref[...] = (x_ref[...] * (1.0 + mod_ref[1:2, :]) + mod_ref[0:1, :]).astype(BF16)


def _modulate_call(st, x, mod, layer):
    return pl.pallas_call(
        _modulate_kernel,
        grid=(st.n_tiles,),
        in_specs=[st.tok_spec(st.d), st.mod_spec(layer)],
        out_specs=st.tok_spec(st.d),
        out_shape=jax.ShapeDtypeStruct((st.t, st.d), BF16),
        compiler_params=_params("arbitrary"),
        name="modulate",
    )(x, mod)


def _rope_tables(n_lat, n_ctx):
    half = HEAD_DIM // 2
    nf = half // 2
    t = np.arange(n_lat)
    inv = ROPE_THETA ** (-(2.0 / half) * np.arange(nf, dtype=np.float64))
    ang_r = (t // GRID_W)[:, None] * inv
    ang_c = (t % GRID_W)[:, None] * inv
    zeros = np.zeros_like(ang_r)
    cos = np.concatenate([np.cos(ang_r), np.cos(ang_r), np.cos(ang_c), np.cos(ang_c)], axis=1)
    sin_fwd = np.concatenate([-np.sin(ang_r), zeros, -np.sin(ang_c), zeros], axis=1)
    sin_bwd = np.concatenate([zeros, np.sin(ang_r), zeros, np.sin(ang_c)], axis=1)
    pad = lambda a, v: np.concatenate([a, np.full((n_ctx, HEAD_DIM), v)], axis=0).astype(np.float32)
    return jnp.asarray(pad(cos, 1.0)), jnp.asarray(pad(sin_fwd, 0.0)), jnp.asarray(pad(sin_bwd, 0.0))


def _ab_in_kernel(h_ref, w_ref, qg_ref, kg_ref, cos_ref, sf_ref, sb_ref, f_ref, q_ref, k_ref, v_ref):
    acc = _bdot(h_ref[...], w_ref[...])
    cos, sf, sb = cos_ref[...], sf_ref[...], sb_ref[...]
    nf = HEAD_DIM // 4

    def norm_rope(xh, gain):
        ms = jnp.mean(xh * xh, axis=-1, keepdims=True)
        y = xh * lax.rsqrt(ms + RMS_EPS) * gain
        return y * cos + pltpu.roll(y, HEAD_DIM - nf, 1) * sf + pltpu.roll(y, nf, 1) * sb

    f_ref[...] = acc[:, :FNET_WIDTH]
    q0 = FNET_WIDTH
    k0 = q0 + GQA_Q_HEADS * HEAD_DIM
    v0 = k0 + GQA_KV_HEADS * HEAD_DIM
    for h in range(GQA_Q_HEADS):
        xh = acc[:, q0 + h * HEAD_DIM:q0 + (h + 1) * HEAD_DIM]
        q_ref[:, h * HEAD_DIM:(h + 1) * HEAD_DIM] = (norm_rope(xh, qg_ref[...]) * SCORE_SCALE).astype(BF16)
    for h in range(GQA_KV_HEADS):
        xh = acc[:, k0 + h * HEAD_DIM:k0 + (h + 1) * HEAD_DIM]
        k_ref[:, h * HEAD_DIM:(h + 1) * HEAD_DIM] = norm_rope(xh, kg_ref[...]).astype(BF16)
    v_ref[...] = acc[:, v0:].astype(BF16)


def _ab_in_call(st, h, w_in, q_gain, k_gain, rope):
    nq = GQA_Q_HEADS * HEAD_DIM
    nkv = GQA_KV_HEADS * HEAD_DIM
    pos_spec = pl.BlockSpec((TOKEN_TILE, HEAD_DIM), lambda t: (t % st.tiles_per_sample, 0))
    return pl.pallas_call(
        _ab_in_kernel,
        grid=(st.n_tiles,),
        in_specs=[st.tok_spec(st.d), _full_spec(w_in.shape), _full_spec((1, HEAD_DIM)), _full_spec((1, HEAD_DIM)),
                  pos_spec, pos_spec, pos_spec],
        out_specs=[st.tok_spec(FNET_WIDTH), st.tok_spec(nq), st.tok_spec(nkv), st.tok_spec(nkv)],
        out_shape=[jax.ShapeDtypeStruct((st.t, FNET_WIDTH), F32), jax.ShapeDtypeStruct((st.t, nq), BF16),
                   jax.ShapeDtypeStruct((st.t, nkv), BF16), jax.ShapeDtypeStruct((st.t, nkv), BF16)],
        compiler_params=_params("arbitrary"),
        name="ab_in",
    )(h, w_in, q_gain.reshape(1, HEAD_DIM), k_gain.reshape(1, HEAD_DIM), *rope)


def _attend(q, keys_values, biases):
    scores = []
    for (k, _), bias in zip(keys_values, biases):
        s = _bdot_t(q, k)
        scores.append(s if bias is None else s + bias)
    m = scores[0].max(axis=-1, keepdims=True)
    for s in scores[1:]:
        m = jnp.maximum(m, s.max(axis=-1, keepdims=True))
    num = None
    den = None
    for s, (_, v) in zip(scores, keys_values):
        p = jnp.exp2(s - m)
        pv = _bdot(p.astype(BF16), v)
        ps = p.sum(axis=-1, keepdims=True)
        num = pv if num is None else num + pv
        den = ps if den is None else den + ps
    return num / den


def _gqa_kernel(q_ref, k_ref, v_ref, o_ref, *, n_lat, lat_tiles):
    def run(k, v):
        for h in range(GQA_GROUP):
            sl = slice(h * HEAD_DIM, (h + 1) * HEAD_DIM)
            o_ref[:, sl] = _attend(q_ref[:, sl], [(k, v)], [None]).astype(BF16)

    is_lat = pl.program_id(2) < lat_tiles

    @pl.when(is_lat)
    def _():
        run(k_ref[...], v_ref[...])

    @pl.when(jnp.logical_not(is_lat))
    def _():
        run(k_ref[n_lat:, :], v_ref[n_lat:, :])


def _gqa_call(st, q, k, v):
    gw = GQA_GROUP * HEAD_DIM
    q3 = q.reshape(st.bsz, st.nt, GQA_Q_HEADS * HEAD_DIM)
    k3 = k.reshape(st.bsz, st.nt, GQA_KV_HEADS * HEAD_DIM)
    v3 = v.reshape(st.bsz, st.nt, GQA_KV_HEADS * HEAD_DIM)
    q_spec = pl.BlockSpec((None, TOKEN_TILE, gw), lambda b, g, i: (b, i, g))
    kv_spec = pl.BlockSpec((None, st.nt, HEAD_DIM), lambda b, g, i: (b, 0, g))
    o = pl.pallas_call(
        functools.partial(_gqa_kernel, n_lat=st.n_lat, lat_tiles=st.lat_tiles),
        grid=(st.bsz, GQA_KV_HEADS, st.tiles_per_sample),
        in_specs=[q_spec, kv_spec, kv_spec],
        out_specs=q_spec,
        out_shape=jax.ShapeDtypeStruct(q3.shape, BF16),
        compiler_params=_params("arbitrary", "arbitrary", "arbitrary"),
        name="gqa",
    )(q3, k3, v3)
    return o.reshape(st.t, GQA_Q_HEADS * HEAD_DIM)


def _fft_split(n):
    l1 = 1 << ((n.bit_length() - 1 + 1) // 2)
    assert n % l1 == 0 and n == l1 * (n // l1)
    return l1, n // l1


def _fft_tables(n):
    l1, l2 = _fft_split(n)
    a = np.arange(l1, dtype=np.float64)
    ang1 = 2.0 * np.pi * np.outer(a, a) / l1
    stage1 = np.concatenate([np.cos(ang1), -np.sin(ang1)], axis=0)
    b = np.arange(l2, dtype=np.float64)
    ang_t = 2.0 * np.pi * np.outer(b, a) / n
    tw_cos = np.cos(ang_t)[:, :, None]
    tw_sin = np.sin(ang_t)[:, :, None]
    ang2 = 2.0 * np.pi * np.outer(b, b) / l2
    c2, s2 = np.cos(ang2), np.sin(ang2)
    stage2 = np.block([[c2, s2], [-s2, c2]])
    f32 = lambda x: jnp.asarray(x.astype(np.float32))
    return f32(stage1), f32(tw_cos), f32(tw_sin), f32(stage2)


FNET_GROUPS_PER_SLAB = V7X_LANES // FNET_GROUP_DIM
FNET_SLABS = FNET_WIDTH // V7X_LANES
FNET_UNROLL = 4


def _fnet_channel_tables(n_positions):
    c = np.arange(FNET_GROUP_DIM, dtype=np.float64)
    ang = 2.0 * np.pi * np.outer(c, c) / FNET_GROUP_DIM
    eye = np.eye(FNET_GROUPS_PER_SLAB)
    scale = 1.0 / math.sqrt(n_positions * FNET_GROUP_DIM)
    m = np.concatenate([np.kron(eye, np.cos(ang)), np.kron(eye, np.sin(ang))], axis=0) * scale
    return jnp.asarray(m.astype(np.float32))


def _fnet_part(f_ref, o_ref, a_ref, row0, n, s1_ref, tc_ref, ts_ref, s2_ref, ch_ref, wf_ref):
    l1, l2 = _fft_split(n)
    stage1 = s1_ref[...]
    stage2 = s2_ref[...]
    chan = ch_ref[...]
    wf = wf_ref[...]

    def first(j, carry):
        xs = f_ref[pl.ds(row0 + j, l1, stride=l2), :]
        a = _dot3(stage1, xs)
        ar, ai = a[:l1], a[l1:]
        tc, ts = tc_ref[j], ts_ref[j]
        a_ref[0, pl.ds(pl.multiple_of(j * l1, l1), l1), :] = ar * tc + ai * ts
        a_ref[1, pl.ds(pl.multiple_of(j * l1, l1), l1), :] = ai * tc - ar * ts
        return carry

    lax.fori_loop(0, l2, first, 0, unroll=FNET_UNROLL)

    def second(j, carry):
        br = a_ref[0, pl.ds(j, l2, stride=l1), :]
        bi = a_ref[1, pl.ds(j, l2, stride=l1), :]
        p = _dot3(stage2, jnp.concatenate([br, bi], axis=0))
        re = _dot3(jnp.concatenate([p[:l2], p[l2:]], axis=1), chan)
        o_ref[pl.ds(row0 + j, l2, stride=l1), :] = _bdot(re.astype(BF16), wf)
        return carry

    lax.fori_loop(0, l1, second, 0, unroll=FNET_UNROLL)


def _fnet_kernel(f_ref, s1l, tcl, tsl, s2l, chl, s1c, tcc, tsc, s2c, chc, wf_ref, o_ref, a_ref, *, n_lat, n_ctx):
    _fnet_part(f_ref, o_ref, a_ref, 0, n_lat, s1l, tcl, tsl, s2l, chl, wf_ref)
    _fnet_part(f_ref, o_ref, a_ref, n_lat, n_ctx, s1c, tcc, tsc, s2c, chc, wf_ref)


def _fnet_call(st, f, w_fnet):
    gps = FNET_GROUPS_PER_SLAB
    eye = jnp.eye(gps, dtype=F32)
    wg = w_fnet.reshape(FNET_SLABS, gps, FNET_GROUP_DIM, FNET_GROUP_DIM)
    wf = (eye[None, :, None, :, None] * wg[:, :, :, None, :]).reshape(FNET_SLABS, V7X_LANES, V7X_LANES).astype(BF16)
    consts = (*_fft_tables(st.n_lat), _fnet_channel_tables(st.n_lat),
              *_fft_tables(st.n_ctx), _fnet_channel_tables(st.n_ctx))
    f3 = f.reshape(st.bsz, st.nt, FNET_WIDTH)
    blk = pl.BlockSpec((None, st.nt, V7X_LANES), lambda b, s: (b, 0, s))
    o = pl.pallas_call(
        functools.partial(_fnet_kernel, n_lat=st.n_lat, n_ctx=st.n_ctx),
        grid=(st.bsz, FNET_SLABS),
        in_specs=[blk] + [_full_spec(c.shape) for c in consts]
        + [pl.BlockSpec((None, V7X_LANES, V7X_LANES), lambda b, s: (s, 0, 0))],
        out_specs=blk,
        out_shape=jax.ShapeDtypeStruct(f3.shape, F32),
        scratch_shapes=[pltpu.VMEM((2, st.n_lat, V7X_LANES), F32)],
        compiler_params=_params("arbitrary", "arbitrary"),
        name="fnet",
    )(f3, *consts, wf)
    return o.reshape(st.t, FNET_WIDTH)


SEG_ALIGN = V7X_SUBLANES
PAIRS_PER_TILE = TOKEN_TILE * TOP_K
LOCAL_ROWS = PAIRS_PER_TILE + N_EXPERTS * SEG_ALIGN
ROUTE_LANES = V7X_LANES


def _route(h2, wr, e_bias):
    tm = h2.shape[0]
    scores = jax.nn.sigmoid(_dot3(h2, wr))
    sel = scores + e_bias
    lane = lax.broadcasted_iota(jnp.int32, sel.shape, 1).astype(F32)
    hits = []
    for _ in range(TOP_K):
        best = sel.max(axis=-1, keepdims=True)
        first = jnp.where(sel == best, lane, float(N_EXPERTS)).min(axis=-1, keepdims=True)
        hit = lane == first
        hits.append(hit)
        sel = jnp.where(hit, -jnp.inf, sel)
    chosen = hits[0]
    for hit in hits[1:]:
        chosen = jnp.logical_or(chosen, hit)
    chosen_f = jnp.where(chosen, 1.0, 0.0)
    counts = chosen_f.sum(axis=0, keepdims=True)
    seg_len = jnp.ceil(counts * (1.0 / SEG_ALIGN)) * SEG_ALIGN
    er = lax.broadcasted_iota(jnp.int32, (N_EXPERTS, N_EXPERTS), 0)
    ec = lax.broadcasted_iota(jnp.int32, (N_EXPERTS, N_EXPERTS), 1)
    lower_experts = jnp.where(er < ec, 1.0, 0.0).astype(BF16)
    seg_start = _bdot(jnp.broadcast_to(seg_len, (V7X_SUBLANES, N_EXPERTS)).astype(BF16), lower_experts)[0:1]
    r = lax.broadcasted_iota(jnp.int32, (tm, tm), 0)
    c = lax.broadcasted_iota(jnp.int32, (tm, tm), 1)
    earlier = jnp.where(c < r, 1.0, 0.0).astype(BF16)
    row_all = _bdot(earlier, chosen_f.astype(BF16)) + seg_start
    rec_lane = lax.broadcasted_iota(jnp.int32, (tm, ROUTE_LANES), 1)
    rec = jnp.zeros((tm, ROUTE_LANES), F32)
    raw = []
    for k, hit in enumerate(hits):
        rec = jnp.where(rec_lane == k, jnp.where(hit, row_all, 0.0).sum(axis=-1, keepdims=True), rec)
        raw.append(jnp.where(hit, scores, 0.0).sum(axis=-1, keepdims=True))
    total = raw[0]
    for w in raw[1:]:
        total = total + w
    for k, w in enumerate(raw):
        rec = jnp.where(rec_lane == TOP_K + k, w / total * ROUTE_SCALE, rec)
    return rec, counts.astype(jnp.int32)


def _out_ln_kernel(*refs, n_in, alpha):
    a_refs = refs[:n_in]
    w_refs = refs[n_in:2 * n_in]
    x_ref, mod_ref, g_ref, b_ref, wr_ref, eb_ref, x1_ref, h2_ref, rec_ref, cnt_ref = refs[2 * n_in:]
    y = None
    for a_ref, w_ref in zip(a_refs, w_refs):
        part = _bdot(a_ref[...].astype(BF16), w_ref[...])
        y = part if y is None else y + part
    z = alpha * x_ref[...] + mod_ref[2:3, :] * y
    x1 = _layer_norm(z, g_ref[...], b_ref[...])
    x1_ref[...] = x1
    h2 = x1 * (1.0 + mod_ref[4:5, :]) + mod_ref[3:4, :]
    h2_ref[...] = h2.astype(BF16)
    rec, counts = _route(h2, wr_ref[...], eb_ref[...])
    rec_ref[...] = rec
    cnt_ref[...] = counts


def _out_ln_call(st, acts, w_out, x, mod, layer, ln_g, ln_b, w_router, e_bias, alpha):
    ws, r0 = [], 0
    for a in acts:
        ws.append(w_out[r0:r0 + a.shape[1]].astype(BF16))
        r0 += a.shape[1]
    assert r0 == w_out.shape[0]
    row = lambda v: v.reshape(1, -1)
    return pl.pallas_call(
        functools.partial(_out_ln_kernel, n_in=len(acts), alpha=alpha),
        grid=(st.n_tiles,),
        in_specs=[st.tok_spec(a.shape[1]) for a in acts] + [_full_spec(w.shape) for w in ws]
        + [st.tok_spec(st.d), st.mod_spec(layer), _full_spec((1, st.d)), _full_spec((1, st.d)),
           _full_spec(w_router.shape), _full_spec((1, N_EXPERTS))],
        out_specs=[st.tok_spec(st.d), st.tok_spec(st.d), st.tok_spec(ROUTE_LANES),
                   pl.BlockSpec((None, 1, N_EXPERTS), lambda t: (t, 0, 0))],
        out_shape=[jax.ShapeDtypeStruct((st.t, st.d), F32), jax.ShapeDtypeStruct((st.t, st.d), BF16),
                   jax.ShapeDtypeStruct((st.t, ROUTE_LANES), F32),
                   jax.ShapeDtypeStruct((st.n_tiles, 1, N_EXPERTS), jnp.int32)],
        compiler_params=_params("arbitrary"),
        name="out_ln",
    )(*acts, *ws, x, mod, row(ln_g), row(ln_b), w_router, row(e_bias))


MOE_BLOCK = 512
SEG_SIZES = tuple(TOKEN_TILE >> s for s in range((TOKEN_TILE // SEG_ALIGN).bit_length()))
ROW_CHUNK = 256
PLAN_WIDTH = 4 * N_EXPERTS
WAIT_SIZES = tuple(1 << b for b in range((LOCAL_ROWS - 1).bit_length() - 1, SEG_ALIGN.bit_length() - 2, -1))


def _moe_blocks(st):
    return (st.t * TOP_K + st.n_tiles * N_EXPERTS * (SEG_ALIGN - 1)) // MOE_BLOCK + N_EXPERTS


def _slot_plan(tile_counts, n_blocks):
    n_tiles = tile_counts.shape[0]
    cnt = tile_counts.reshape(n_tiles, N_EXPERTS)
    seg = (cnt + SEG_ALIGN - 1) // SEG_ALIGN * SEG_ALIGN
    local_start = jnp.cumsum(seg, axis=1) - seg
    tiles_before = jnp.cumsum(seg, axis=0) - seg
    total = seg.sum(axis=0)
    padded = (total + MOE_BLOCK - 1) // MOE_BLOCK * MOE_BLOCK
    pad_end = jnp.cumsum(padded)
    pad_start = pad_end - padded
    sorted_start = pad_start[None, :] + tiles_before
    tile_rows = jnp.broadcast_to(seg.sum(axis=1, keepdims=True), seg.shape)
    plan = jnp.concatenate([seg, local_start, sorted_start, tile_rows], axis=1).astype(jnp.int32)
    plan = plan.reshape(n_tiles, 1, PLAN_WIDTH)
    n_used = jnp.maximum(pad_end[-1] // MOE_BLOCK, 1)
    first_row = jnp.arange(n_blocks, dtype=jnp.int32) * MOE_BLOCK
    block_e = jnp.minimum((first_row[:, None] >= pad_end[None, :]).sum(-1), N_EXPERTS - 1).astype(jnp.int32)
    ends_expert = ((first_row[:, None] + MOE_BLOCK == pad_end[None, :]) & (padded[None, :] > 0)).any(-1)
    fill = (ends_expert | (first_row >= pad_end[-1])).astype(jnp.int32)
    return plan, block_e, n_used.astype(jnp.int32).reshape(1), fill


def _for_each_piece(plan_ref, fn):
    def per_expert(e, carry):
        seg = plan_ref[0, e]
        local = plan_ref[0, N_EXPERTS + e]
        dst = plan_ref[0, 2 * N_EXPERTS + e]
        done = 0
        for size in SEG_SIZES:
            take = (seg & size) != 0

            @pl.when(take)
            def _(done=done, size=size):
                fn(pl.multiple_of(local + done, SEG_ALIGN), pl.multiple_of(dst + done, SEG_ALIGN), size)

            done = done + jnp.where(take, size, 0)
        return carry

    lax.fori_loop(0, N_EXPERTS, per_expert, 0)


def _wait_tile_rows(plan_ref, make_copy):
    rows = plan_ref[0, 3 * N_EXPERTS]
    for size in WAIT_SIZES:
        @pl.when((rows & size) != 0)
        def _(size=size):
            make_copy(size).wait()


def _pack_halves(x):
    bits = pltpu.bitcast(x, jnp.uint32)
    half = x.shape[1] // 2
    return (bits[:, :half] >> 16) | (bits[:, half:] & jnp.uint32(0xFFFF0000))


def _unpack_halves(w):
    lo = pltpu.bitcast(w << 16, F32).astype(BF16)
    hi = pltpu.bitcast(w & jnp.uint32(0xFFFF0000), F32).astype(BF16)
    return lo, hi


def _dispatch_kernel(fill_ref, plan_ref, plan1_ref, plan2_ref, rec_ref, h_ref, xs_ref, loc_ref, zero_ref, zsem, sem,
                     *, n_blocks, n_tiles):
    i = pl.program_id(0)
    slot = i % 2

    @pl.when(i == 0)
    def _():
        zero_ref[...] = jnp.zeros_like(zero_ref)

        def fill(b):
            return pltpu.make_async_copy(zero_ref, xs_ref.at[pl.ds(pl.multiple_of(b * MOE_BLOCK, MOE_BLOCK), MOE_BLOCK)],
                                         zsem)

        def start(b, c):
            @pl.when(fill_ref[b] > 0)
            def _():
                fill(b).start()
            return c

        def wait(b, c):
            @pl.when(fill_ref[b] > 0)
            def _():
                fill(b).wait()
            return c

        lax.fori_loop(0, n_blocks, start, 0)
        lax.fori_loop(0, n_blocks, wait, 0)

    def piece(s):
        def copy(local, dst, size):
            return pltpu.make_async_copy(loc_ref.at[s, pl.ds(local, size)], xs_ref.at[pl.ds(dst, size)], sem.at[s])
        return copy

    def rows_done(s):
        return lambda size: piece(s)(0, 0, size)

    @pl.when(i >= 2)
    def _():
        _wait_tile_rows(plan2_ref, rows_done(slot))

    rows_of = rec_ref[...].T
    x = h_ref[...]
    for c0 in range(0, LOCAL_ROWS, ROW_CHUNK):
        local_row = (lax.broadcasted_iota(jnp.int32, (ROW_CHUNK, TOKEN_TILE), 0) + c0).astype(F32)
        onehot = jnp.zeros((ROW_CHUNK, TOKEN_TILE), F32)
        for k in range(TOP_K):
            onehot = jnp.where(local_row == rows_of[k:k + 1, :], 1.0, onehot)
        loc_ref[slot, c0:c0 + ROW_CHUNK, :] = _pack_halves(_bdot(onehot.astype(BF16), x))

    _for_each_piece(plan_ref, lambda *a: piece(slot)(*a).start())

    @pl.when(i == n_tiles - 1)
    def _():
        if n_tiles >= 2:
            _wait_tile_rows(plan1_ref, rows_done(1 - slot))
        _wait_tile_rows(plan_ref, rows_done(slot))


def _dispatch_call(st, h2, rec, plan, fill, n_blocks):
    half = st.d // 2
    plan_spec = lambda back: pl.BlockSpec((None, 1, PLAN_WIDTH), lambda t, *_: (jnp.maximum(t - back, 0), 0, 0),
                                          memory_space=pltpu.SMEM)
    grid_spec = pltpu.PrefetchScalarGridSpec(
        num_scalar_prefetch=1,
        grid=(st.n_tiles,),
        in_specs=[plan_spec(0), plan_spec(1), plan_spec(2),
                  pl.BlockSpec((TOKEN_TILE, ROUTE_LANES), lambda t, *_: (t, 0)),
                  pl.BlockSpec((TOKEN_TILE, st.d), lambda t, *_: (t, 0))],
        out_specs=pl.BlockSpec(memory_space=pl.ANY),
        scratch_shapes=[pltpu.VMEM((2, LOCAL_ROWS, half), jnp.uint32), pltpu.VMEM((MOE_BLOCK, half), jnp.uint32),
                        pltpu.SemaphoreType.DMA, pltpu.SemaphoreType.DMA((2,))],
    )
    return pl.pallas_call(
        functools.partial(_dispatch_kernel, n_blocks=n_blocks, n_tiles=st.n_tiles),
        grid_spec=grid_spec,
        out_shape=jax.ShapeDtypeStruct((n_blocks * MOE_BLOCK, half), jnp.uint32),
        compiler_params=_params("arbitrary"),
        name="moe_dispatch",
    )(fill, plan, plan, plan, rec, h2)


EXPERT_BUFFERS = 3


def _experts_kernel(be_ref, nu_ref, xs_ref, wg_ref, wu_ref, wd_ref, ys_ref, xbuf, ybuf, wg_s, wu_s, wd_s, xsem, ysem,
                    *, n_blocks):
    i = pl.program_id(0)
    slot = i % EXPERT_BUFFERS
    live = i < nu_ref[0]

    def rows(b):
        return pl.ds(pl.multiple_of(b * MOE_BLOCK, MOE_BLOCK), MOE_BLOCK)

    def fetch(b, s):
        return pltpu.make_async_copy(xs_ref.at[rows(b)], xbuf.at[s], xsem.at[s])

    def store(b, s):
        return pltpu.make_async_copy(ybuf.at[s], ys_ref.at[rows(b)], ysem.at[s])

    @pl.when(i == 0)
    def _():
        for b in range(min(EXPERT_BUFFERS - 1, n_blocks)):
            fetch(b, b).start()

    ahead = i + EXPERT_BUFFERS - 1

    @pl.when(ahead < n_blocks)
    def _():
        fetch(ahead, ahead % EXPERT_BUFFERS).start()

    prev = be_ref[jnp.maximum(i, 1) - 1]

    @pl.when(jnp.logical_and(live, jnp.logical_or(i == 0, be_ref[i] != prev)))
    def _():
        wg_s[...] = wg_ref[...].astype(BF16)
        wu_s[...] = wu_ref[...].astype(BF16)
        wd_s[...] = wd_ref[...].astype(BF16)

    fetch(i, slot).wait()

    @pl.when(i >= EXPERT_BUFFERS)
    def _():
        store(i - EXPERT_BUFFERS, slot).wait()

    @pl.when(live)
    def _():
        half = wg_s.shape[0] // 2
        lo, hi = _unpack_halves(xbuf[slot])
        gate = _bdot(lo, wg_s[:half, :]) + _bdot(hi, wg_s[half:, :])
        up = _bdot(lo, wu_s[:half, :]) + _bdot(hi, wu_s[half:, :])
        y = _bdot((_silu(gate) * up).astype(BF16), wd_s[...])
        ybuf[slot] = _pack_halves(y.astype(BF16).astype(F32))

    @pl.when(jnp.logical_not(live))
    def _():
        ybuf[slot] = jnp.zeros(ybuf.shape[1:], ybuf.dtype)

    store(i, slot).start()

    @pl.when(i == n_blocks - 1)
    def _():
        for b in range(max(n_blocks - EXPERT_BUFFERS, 0), n_blocks):
            store(b, b % EXPERT_BUFFERS).wait()


def _experts_call(st, xs, block_e, n_used, w_gate, w_up, w_down, layer, n_blocks):
    half = st.d // 2
    w_in_spec = pl.BlockSpec((None, None, st.d, EXPERT_DIM), lambda i, be, nu: (layer, be[i], 0, 0))
    grid_spec = pltpu.PrefetchScalarGridSpec(
        num_scalar_prefetch=2,
        grid=(n_blocks,),
        in_specs=[pl.BlockSpec(memory_space=pl.ANY),
                  w_in_spec, w_in_spec,
                  pl.BlockSpec((None, None, EXPERT_DIM, st.d), lambda i, be, nu: (layer, be[i], 0, 0))],
        out_specs=pl.BlockSpec(memory_space=pl.ANY),
        scratch_shapes=[pltpu.VMEM((EXPERT_BUFFERS, MOE_BLOCK, half), jnp.uint32),
                        pltpu.VMEM((EXPERT_BUFFERS, MOE_BLOCK, half), jnp.uint32),
                        pltpu.VMEM((st.d, EXPERT_DIM), BF16), pltpu.VMEM((st.d, EXPERT_DIM), BF16),
                        pltpu.VMEM((EXPERT_DIM, st.d), BF16),
                        pltpu.SemaphoreType.DMA((EXPERT_BUFFERS,)), pltpu.SemaphoreType.DMA((EXPERT_BUFFERS,))],
    )
    return pl.pallas_call(
        functools.partial(_experts_kernel, n_blocks=n_blocks),
        grid_spec=grid_spec,
        out_shape=jax.ShapeDtypeStruct(xs.shape, jnp.uint32),
        compiler_params=_params("arbitrary"),
        name="moe_experts",
    )(block_e, n_used, xs, w_gate, w_up, w_down)


def _ffn_ln_kernel(plan_ref, plan_next_ref, rec_ref, h2_ref, x1_ref, sg_ref, su_ref, sd_ref, mod_ref, modn_ref, g_ref,
                   b_ref, ys_ref, x2_ref, hn_ref, loc_ref, sem, *, alpha, n_tiles):
    i = pl.program_id(0)
    slot = i % 2

    def piece(s):
        def copy(local, src, size):
            return pltpu.make_async_copy(ys_ref.at[pl.ds(src, size)], loc_ref.at[s, pl.ds(local, size)], sem.at[s])
        return copy

    @pl.when(i == 0)
    def _():
        loc_ref[...] = jnp.zeros_like(loc_ref)
        _for_each_piece(plan_ref, lambda *a: piece(0)(*a).start())

    @pl.when(i + 1 < n_tiles)
    def _():
        _for_each_piece(plan_next_ref, lambda *a: piece(1 - slot)(*a).start())

    h2 = h2_ref[...]
    a = _silu(_bdot(h2, sg_ref[...])) * _bdot(h2, su_ref[...])
    shared = _bdot(a.astype(BF16), sd_ref[...])

    _wait_tile_rows(plan_ref, lambda size: piece(slot)(0, 0, size))
    rec = rec_ref[...]
    half = loc_ref.shape[2]
    routed_lo = jnp.zeros((TOKEN_TILE, half), F32)
    routed_hi = jnp.zeros((TOKEN_TILE, half), F32)
    for c0 in range(0, LOCAL_ROWS, ROW_CHUNK):
        local_row = (lax.broadcasted_iota(jnp.int32, (TOKEN_TILE, ROW_CHUNK), 1) + c0).astype(F32)
        weight = jnp.zeros((TOKEN_TILE, ROW_CHUNK), F32)
        for k in range(TOP_K):
            weight = jnp.where(local_row == rec[:, k:k + 1], rec[:, TOP_K + k:TOP_K + k + 1], weight)
        lo, hi = _unpack_halves(loc_ref[slot, c0:c0 + ROW_CHUNK, :])
        weight = weight.astype(BF16)
        routed_lo = routed_lo + _bdot(weight, lo)
        routed_hi = routed_hi + _bdot(weight, hi)
    ff = jnp.concatenate([routed_lo, routed_hi], axis=1) + shared
    z = alpha * x1_ref[...] + mod_ref[5:6, :] * ff
    x2 = _layer_norm(z, g_ref[...], b_ref[...])
    x2_ref[...] = x2
    hn_ref[...] = (x2 * (1.0 + modn_ref[1:2, :]) + modn_ref[0:1, :]).astype(BF16)


def _ffn_ln_call(st, ys, plan, rec, h2, x1, s_gate, s_up, s_down, mod, layer, next_layer, ln_g, ln_b, alpha):
    row = lambda v: v.reshape(1, -1)
    sg, su, sd = s_gate.astype(BF16), s_up.astype(BF16), s_down.astype(BF16)
    last = st.n_tiles - 1
    return pl.pallas_call(
        functools.partial(_ffn_ln_kernel, alpha=alpha, n_tiles=st.n_tiles),
        grid=(st.n_tiles,),
        in_specs=[pl.BlockSpec((None, 1, PLAN_WIDTH), lambda t: (t, 0, 0), memory_space=pltpu.SMEM),
                  pl.BlockSpec((None, 1, PLAN_WIDTH), lambda t: (jnp.minimum(t + 1, last), 0, 0),
                               memory_space=pltpu.SMEM),
                  st.tok_spec(ROUTE_LANES), st.tok_spec(st.d), st.tok_spec(st.d),
                  _full_spec(sg.shape), _full_spec(su.shape), _full_spec(sd.shape),
                  st.mod_spec(layer), st.mod_spec(next_layer), _full_spec((1, st.d)), _full_spec((1, st.d)),
                  pl.BlockSpec(memory_space=pl.ANY)],
        out_specs=[st.tok_spec(st.d), st.tok_spec(st.d)],
        out_shape=[jax.ShapeDtypeStruct((st.t, st.d), F32), jax.ShapeDtypeStruct((st.t, st.d), BF16)],
        scratch_shapes=[pltpu.VMEM((2, LOCAL_ROWS, st.d // 2), jnp.uint32), pltpu.SemaphoreType.DMA((2,))],
        compiler_params=_params("arbitrary"),
        name="ffn_ln",
    )(plan, plan, rec, h2, x1, sg, su, sd, mod, mod, row(ln_g), row(ln_b), ys)


def _na_in_kernel(h_ref, w_ref, o_ref):
    h = h_ref[...]
    o_ref[:, :NA_WIDTH] = (_bdot(h, w_ref[:, :NA_WIDTH]) * SCORE_SCALE).astype(BF16)
    o_ref[:, NA_WIDTH:] = _bdot(h, w_ref[:, NA_WIDTH:]).astype(BF16)


def _na_in_call(st, h, w_in):
    return pl.pallas_call(
        _na_in_kernel,
        grid=(st.n_tiles,),
        in_specs=[st.tok_spec(st.d), _full_spec(w_in.shape)],
        out_specs=st.tok_spec(3 * NA_WIDTH),
        out_shape=jax.ShapeDtypeStruct((st.t, 3 * NA_WIDTH), BF16),
        compiler_params=_params("arbitrary"),
        name="na_in",
    )(h, w_in)


def _na_geometry(n_lat):
    rows = n_lat // GRID_W
    kh, kw, qr = min(NA_KH, rows), min(NA_KW, GRID_W), NA_Q_ROWS
    nbr = min(qr + kh - 1, rows)
    col = np.arange(GRID_W)
    col_start = np.clip(col - kw // 2, 0, GRID_W - kw)
    in_col = (col[None, :] >= col_start[:, None]) & (col[None, :] < col_start[:, None] + kw)
    dc = np.clip(col[None, :] - col[:, None] + NA_KW - 1, 0, 2 * NA_KW - 2)
    starts, variant_of, variants = [], [], {}
    for i in range(rows // qr):
        qrow = i * qr + np.arange(qr)
        rstart = np.clip(qrow - kh // 2, 0, rows - kh)
        bs = min(int(rstart[0]), rows - nbr)
        krow = bs + np.arange(nbr)
        in_row = (krow[None, :] >= rstart[:, None]) & (krow[None, :] < rstart[:, None] + kh)
        dr = np.clip(krow[None, :] - qrow[:, None] + NA_KH - 1, 0, 2 * NA_KH - 2)
        key = (in_row.tobytes(), dr.tobytes())
        if key not in variants:
            mask = (in_row[:, None, :, None] & in_col[None, :, None, :]).reshape(qr * GRID_W, nbr * GRID_W)
            variants[key] = (len(variants), dr, mask)
        starts.append(bs)
        variant_of.append(variants[key][0])
    ordered = sorted(variants.values(), key=lambda v: v[0])
    return nbr, np.asarray(starts, np.int32), np.asarray(variant_of, np.int32), [(v[1], v[2]) for v in ordered], dc


def _na_bias_tables(rpb, n_lat):
    nbr, starts, variant_of, variants, dc = _na_geometry(n_lat)
    n_dr, n_dc = 2 * NA_KH - 1, 2 * NA_KW - 1
    col_sel = jnp.asarray((dc[:, :, None] == np.arange(n_dc)).astype(np.float32))
    by_col = jnp.einsum("hrc,wuc->hrwu", rpb, col_sel, precision=lax.Precision.HIGHEST)
    tables = []
    for dr, mask in variants:
        row_sel = jnp.asarray((dr[:, :, None] == np.arange(n_dr)).astype(np.float32))
        bias = jnp.einsum("ajr,hrwu->hawju", row_sel, by_col, precision=lax.Precision.HIGHEST)
        bias = bias.reshape(NA_HEADS, NA_Q_ROWS * GRID_W, nbr * GRID_W)
        tables.append(jnp.where(jnp.asarray(mask)[None], bias * LOG2_E, NEG_INF))
    return nbr, starts, variant_of, jnp.stack(tables)


NA_Q_TILE = NA_Q_ROWS * GRID_W


def _na_kernel(start_ref, var_ref, q_ref, k_ref, v_ref, bias_ref, o_ref, *, n_lat, band):
    i = pl.program_id(1)
    is_lat = i < n_lat // NA_Q_TILE

    @pl.when(is_lat)
    def _():
        off = pl.multiple_of(start_ref[i] * GRID_W, GRID_W)
        for h in range(NA_HEADS):
            sl = slice(h * HEAD_DIM, (h + 1) * HEAD_DIM)
            local = (k_ref[pl.ds(off, band), sl], v_ref[pl.ds(off, band), sl])
            ctx = (k_ref[n_lat:, sl], v_ref[n_lat:, sl])
            o_ref[:, sl] = _attend(q_ref[:, sl], [local, ctx], [bias_ref[h], None]).astype(BF16)

    @pl.when(jnp.logical_not(is_lat))
    def _():
        for h in range(NA_HEADS):
            sl = slice(h * HEAD_DIM, (h + 1) * HEAD_DIM)
            o_ref[:, sl] = _attend(q_ref[:, sl], [(k_ref[n_lat:, sl], v_ref[n_lat:, sl])], [None]).astype(BF16)


def _na_call(st, qkv, rpb):
    nbr, starts, variant_of, bias = _na_bias_tables(rpb, st.n_lat)
    band = nbr * GRID_W
    n_q = st.nt // NA_Q_TILE
    pad = n_q - starts.shape[0]
    starts = jnp.asarray(np.concatenate([starts, np.zeros(pad, np.int32)]))
    variant_of = jnp.asarray(np.concatenate([variant_of, np.zeros(pad, np.int32)]))
    qkv3 = qkv.reshape(st.bsz, st.nt, 3 * NA_WIDTH)
    grid_spec = pltpu.PrefetchScalarGridSpec(
        num_scalar_prefetch=2,
        grid=(st.bsz, n_q),
        in_specs=[pl.BlockSpec((None, NA_Q_TILE, NA_WIDTH), lambda b, i, s, v: (b, i, 0)),
                  pl.BlockSpec((None, st.nt, NA_WIDTH), lambda b, i, s, v: (b, 0, 1)),
                  pl.BlockSpec((None, st.nt, NA_WIDTH), lambda b, i, s, v: (b, 0, 2)),
                  pl.BlockSpec((None, NA_HEADS, NA_Q_TILE, band), lambda b, i, s, v: (v[i], 0, 0, 0))],
        out_specs=pl.BlockSpec((None, NA_Q_TILE, NA_WIDTH), lambda b, i, s, v: (b, i, 0)),
    )
    o = pl.pallas_call(
        functools.partial(_na_kernel, n_lat=st.n_lat, band=band),
        grid_spec=grid_spec,
        out_shape=jax.ShapeDtypeStruct((st.bsz, st.nt, NA_WIDTH), BF16),
        compiler_params=_params("arbitrary", "arbitrary"),
        name="na_attn",
    )(starts, variant_of, qkv3, qkv3, qkv3, bias)
    return o.reshape(st.t, NA_WIDTH)


def kernel(x, c, ctx, c_ctx, w_mod, b_mod, ln1_g, ln1_b, ln2_g, ln2_b, ab_w_in, ab_w_fnet, ab_q_norm, ab_k_norm,
           ab_w_out, na_w_in, na_rpb, na_w_out, moe_w_router, moe_bias, moe_w_gate, moe_w_up, moe_w_down,
           sh_w_gate, sh_w_up, sh_w_down):
    bsz, n_lat, d = x.shape
    n_ctx = ctx.shape[1]
    depth = w_mod.shape[0]
    st = _Stream(bsz, n_lat, n_ctx, d)
    alpha = (2 * depth) ** 0.25

    cc = jnp.concatenate([c, c_ctx[None, :], jnp.zeros((MOD_ROWS - bsz - 1, d), F32)], axis=0)
    mod = _mod_call(cc, w_mod, b_mod).reshape(depth, MOD_ROWS, 6, d)
    xs = jnp.concatenate([x, ctx], axis=1).reshape(st.t, d)
    h = _modulate_call(st, xs, mod, 0)
    rope = _rope_tables(n_lat, n_ctx)
    n_blocks = _moe_blocks(st)

    for l in range(depth):
        j = l // 2
        if l % 2 == 0:
            f, q, k, v = _ab_in_call(st, h, ab_w_in[j].astype(BF16), ab_q_norm[j], ab_k_norm[j], rope)
            acts = [_fnet_call(st, f, ab_w_fnet[j]), _gqa_call(st, q, k, v)]
            w_out = ab_w_out[j]
        else:
            qkv = _na_in_call(st, h, na_w_in[j].astype(BF16))
            acts = [_na_call(st, qkv, na_rpb[j])]
            w_out = na_w_out[j]
        x1, h2, rec, tile_counts = _out_ln_call(st, acts, w_out, xs, mod, l, ln1_g[l], ln1_b[l],
                                                moe_w_router[l], moe_bias[l], alpha)
        plan, block_e, n_used, fill = _slot_plan(tile_counts, n_blocks)
        rows = _dispatch_call(st, h2, rec, plan, fill, n_blocks)
        ys = _experts_call(st, rows, block_e, n_used, moe_w_gate, moe_w_up, moe_w_down, l, n_blocks)
        xs, h = _ffn_ln_call(st, ys, plan, rec, h2, x1, sh_w_gate[l], sh_w_up[l], sh_w_down[l], mod, l,
                             min(l + 1, depth - 1), ln2_g[l], ln2_b[l], alpha)
    return xs.reshape(bsz, st.nt, d)[:, :n_lat]
```

```python
import functools
import math

import numpy as np
import jax
import jax.numpy as jnp
from jax import lax
from jax.experimental import pallas as pl
from jax.experimental.pallas import tpu as pltpu

GRID_W = 64
HEAD_DIM = 128
FNET_GROUPS = 4
FNET_GROUP_DIM = 64
FNET_WIDTH = FNET_GROUPS * FNET_GROUP_DIM
GQA_Q_HEADS = 6
GQA_KV_HEADS = 2
GQA_GROUP = GQA_Q_HEADS // GQA_KV_HEADS
ROPE_THETA = 10000.0
NA_HEADS = 8
NA_WIDTH = NA_HEADS * HEAD_DIM
NA_KH = 8
NA_KW = 16
NA_Q_ROWS = 2
NEG_INF = -1e30
N_EXPERTS = 64
TOP_K = 8
EXPERT_DIM = 256
ROUTE_SCALE = 2.5
LN_EPS = 1e-6
RMS_EPS = 1e-6
ATTN_SCALE = HEAD_DIM ** -0.5
LOG2_E = math.log2(math.e)
SCORE_SCALE = ATTN_SCALE * LOG2_E

V7X_LANES = 128
V7X_SUBLANES = 8
V7X_VMEM_LIMIT_BYTES = 56 * 1024 * 1024

TOKEN_TILE = 256
MOD_ROWS = 8

F32 = jnp.float32
BF16 = jnp.bfloat16


def _params(*sem):
    return pltpu.CompilerParams(dimension_semantics=sem, vmem_limit_bytes=V7X_VMEM_LIMIT_BYTES)


def _bdot(a, b):
    return jnp.dot(a, b, preferred_element_type=F32)


def _bdot_t(a, b):
    return lax.dot_general(a, b, (((1,), (1,)), ((), ())), preferred_element_type=F32)


def _split(x):
    hi = x.astype(BF16)
    lo = (x - hi.astype(F32)).astype(BF16)
    return hi, lo


def _dot3(a, b):
    ah, al = _split(a)
    bh, bl = _split(b)
    return _bdot(ah, bh) + (_bdot(ah, bl) + _bdot(al, bh))


def _silu(x):
    return x * jax.nn.sigmoid(x)


def _layer_norm(z, g, b):
    mu = jnp.mean(z, axis=-1, keepdims=True)
    zc = z - mu
    var = jnp.mean(zc * zc, axis=-1, keepdims=True)
    return zc * lax.rsqrt(var + LN_EPS) * g + b


class _Stream:
    def __init__(self, bsz, n_lat, n_ctx, d):
        assert n_lat % TOKEN_TILE == 0 and n_ctx % TOKEN_TILE == 0
        assert bsz < MOD_ROWS
        self.bsz, self.n_lat, self.n_ctx, self.d = bsz, n_lat, n_ctx, d
        self.nt = n_lat + n_ctx
        self.t = bsz * self.nt
        self.tiles_per_sample = self.nt // TOKEN_TILE
        self.lat_tiles = n_lat // TOKEN_TILE
        self.n_tiles = self.t // TOKEN_TILE

    def mod_row(self, tile):
        return jnp.where(tile % self.tiles_per_sample < self.lat_tiles, tile // self.tiles_per_sample, self.bsz)

    def mod_spec(self, layer):
        return pl.BlockSpec((None, None, 6, self.d), lambda t: (layer, self.mod_row(t), 0, 0))

    def tok_spec(self, width):
        return pl.BlockSpec((TOKEN_TILE, width), lambda t: (t, 0))


def _full_spec(shape):
    nd = len(shape)
    return pl.BlockSpec(shape, lambda *_: (0,) * nd)


def _mod_kernel(cc_ref, w_ref, b_ref, o_ref):
    o_ref[...] = _dot3(_silu(cc_ref[...]), w_ref[...]) + b_ref[...]


def _mod_call(cc, w_mod, b_mod):
    depth, d, n = w_mod.shape
    tn = n // 4
    return pl.pallas_call(
        _mod_kernel,
        grid=(depth, n // tn),
        in_specs=[pl.BlockSpec((MOD_ROWS, d), lambda l, j: (0, 0)),
                  pl.BlockSpec((None, d, tn), lambda l, j: (l, 0, j)),
                  pl.BlockSpec((None, 1, tn), lambda l, j: (l, 0, j))],
        out_specs=pl.BlockSpec((None, MOD_ROWS, tn), lambda l, j: (l, 0, j)),
        out_shape=jax.ShapeDtypeStruct((depth, MOD_ROWS, n), F32),
        compiler_params=_params("arbitrary", "arbitrary"),
        name="mod",
    )(cc, w_mod, b_mod.reshape(depth, 1, n))


def _modulate_kernel(x_ref, mod_ref, h_ref):
    h_ref[...] = (x_ref[...] * (1.0 + mod_ref[1:2, :]) + mod_ref[0:1, :]).astype(BF16)


def _modulate_call(st, x, mod, layer):
    return pl.pallas_call(
        _modulate_kernel,
        grid=(st.n_tiles,),
        in_specs=[st.tok_spec(st.d), st.mod_spec(layer)],
        out_specs=st.tok_spec(st.d),
        out_shape=jax.ShapeDtypeStruct((st.t, st.d), BF16),
        compiler_params=_params("arbitrary"),
        name="modulate",
    )(x, mod)


def _rope_tables(n_lat, n_ctx):
    half = HEAD_DIM // 2
    nf = half // 2
    t = np.arange(n_lat)
    inv = ROPE_THETA ** (-(2.0 / half) * np.arange(nf, dtype=np.float64))
    ang_r = (t // GRID_W)[:, None] * inv
    ang_c = (t % GRID_W)[:, None] * inv
    zeros = np.zeros_like(ang_r)
    cos = np.concatenate([np.cos(ang_r), np.cos(ang_r), np.cos(ang_c), np.cos(ang_c)], axis=1)
    sin_fwd = np.concatenate([-np.sin(ang_r), zeros, -np.sin(ang_c), zeros], axis=1)
    sin_bwd = np.concatenate([zeros, np.sin(ang_r), zeros, np.sin(ang_c)], axis=1)
    pad = lambda a, v: np.concatenate([a, np.full((n_ctx, HEAD_DIM), v)], axis=0).astype(np.float32)
    return jnp.asarray(pad(cos, 1.0)), jnp.asarray(pad(sin_fwd, 0.0)), jnp.asarray(pad(sin_bwd, 0.0))


def _ab_in_kernel(h_ref, w_ref, qg_ref, kg_ref, cos_ref, sf_ref, sb_ref, f_ref, q_ref, k_ref, v_ref):
    acc = _bdot(h_ref[...], w_ref[...])
    cos, sf, sb = cos_ref[...], sf_ref[...], sb_ref[...]
    nf = HEAD_DIM // 4

    def norm_rope(xh, gain):
        ms = jnp.mean(xh * xh, axis=-1, keepdims=True)
        y = xh * lax.rsqrt(ms + RMS_EPS) * gain
        return y * cos + pltpu.roll(y, HEAD_DIM - nf, 1) * sf + pltpu.roll(y, nf, 1) * sb

    f_ref[...] = acc[:, :FNET_WIDTH]
    q0 = FNET_WIDTH
    k0 = q0 + GQA_Q_HEADS * HEAD_DIM
    v0 = k0 + GQA_KV_HEADS * HEAD_DIM
    for h in range(GQA_Q_HEADS):
        xh = acc[:, q0 + h * HEAD_DIM:q0 + (h + 1) * HEAD_DIM]
        q_ref[:, h * HEAD_DIM:(h + 1) * HEAD_DIM] = (norm_rope(xh, qg_ref[...]) * SCORE_SCALE).astype(BF16)
    for h in range(GQA_KV_HEADS):
        xh = acc[:, k0 + h * HEAD_DIM:k0 + (h + 1) * HEAD_DIM]
        k_ref[:, h * HEAD_DIM:(h + 1) * HEAD_DIM] = norm_rope(xh, kg_ref[...]).astype(BF16)
    v_ref[...] = acc[:, v0:].astype(BF16)


def _ab_in_call(st, h, w_in, q_gain, k_gain, rope):
    nq = GQA_Q_HEADS * HEAD_DIM
    nkv = GQA_KV_HEADS * HEAD_DIM
    pos_spec = pl.BlockSpec((TOKEN_TILE, HEAD_DIM), lambda t: (t % st.tiles_per_sample, 0))
    return pl.pallas_call(
        _ab_in_kernel,
        grid=(st.n_tiles,),
        in_specs=[st.tok_spec(st.d), _full_spec(w_in.shape), _full_spec((1, HEAD_DIM)), _full_spec((1, HEAD_DIM)),
                  pos_spec, pos_spec, pos_spec],
        out_specs=[st.tok_spec(FNET_WIDTH), st.tok_spec(nq), st.tok_spec(nkv), st.tok_spec(nkv)],
        out_shape=[jax.ShapeDtypeStruct((st.t, FNET_WIDTH), F32), jax.ShapeDtypeStruct((st.t, nq), BF16),
                   jax.ShapeDtypeStruct((st.t, nkv), BF16), jax.ShapeDtypeStruct((st.t, nkv), BF16)],
        compiler_params=_params("arbitrary"),
        name="ab_in",
    )(h, w_in, q_gain.reshape(1, HEAD_DIM), k_gain.reshape(1, HEAD_DIM), *rope)


def _attend(q, keys_values, biases):
    scores = []
    for (k, _), bias in zip(keys_values, biases):
        s = _bdot_t(q, k)
        scores.append(s if bias is None else s + bias)
    m = scores[0].max(axis=-1, keepdims=True)
    for s in scores[1:]:
        m = jnp.maximum(m, s.max(axis=-1, keepdims=True))
    num = None
    den = None
    for s, (_, v) in zip(scores, keys_values):
        p = jnp.exp2(s - m)
        pv = _bdot(p.astype(BF16), v)
        ps = p.sum(axis=-1, keepdims=True)
        num = pv if num is None else num + pv
        den = ps if den is None else den + ps
    return num / den


def _gqa_kernel(q_ref, k_ref, v_ref, o_ref, *, n_lat, lat_tiles):
    def run(k, v):
        for h in range(GQA_GROUP):
            sl = slice(h * HEAD_DIM, (h + 1) * HEAD_DIM)
            o_ref[:, sl] = _attend(q_ref[:, sl], [(k, v)], [None]).astype(BF16)

    is_lat = pl.program_id(2) < lat_tiles

    @pl.when(is_lat)
    def _():
        run(k_ref[...], v_ref[...])

    @pl.when(jnp.logical_not(is_lat))
    def _():
        run(k_ref[n_lat:, :], v_ref[n_lat:, :])


def _gqa_call(st, q, k, v):
    gw = GQA_GROUP * HEAD_DIM
    q3 = q.reshape(st.bsz, st.nt, GQA_Q_HEADS * HEAD_DIM)
    k3 = k.reshape(st.bsz, st.nt, GQA_KV_HEADS * HEAD_DIM)
    v3 = v.reshape(st.bsz, st.nt, GQA_KV_HEADS * HEAD_DIM)
    q_spec = pl.BlockSpec((None, TOKEN_TILE, gw), lambda b, g, i: (b, i, g))
    kv_spec = pl.BlockSpec((None, st.nt, HEAD_DIM), lambda b, g, i: (b, 0, g))
    o = pl.pallas_call(
        functools.partial(_gqa_kernel, n_lat=st.n_lat, lat_tiles=st.lat_tiles),
        grid=(st.bsz, GQA_KV_HEADS, st.tiles_per_sample),
        in_specs=[q_spec, kv_spec, kv_spec],
        out_specs=q_spec,
        out_shape=jax.ShapeDtypeStruct(q3.shape, BF16),
        compiler_params=_params("arbitrary", "arbitrary", "arbitrary"),
        name="gqa",
    )(q3, k3, v3)
    return o.reshape(st.t, GQA_Q_HEADS * HEAD_DIM)


def _fft_split(n):
    l1 = 1 << ((n.bit_length() - 1 + 1) // 2)
    assert n % l1 == 0 and n == l1 * (n // l1)
    return l1, n // l1


def _fft_tables(n):
    l1, l2 = _fft_split(n)
    a = np.arange(l1, dtype=np.float64)
    ang1 = 2.0 * np.pi * np.outer(a, a) / l1
    stage1 = np.concatenate([np.cos(ang1), -np.sin(ang1)], axis=0)
    b = np.arange(l2, dtype=np.float64)
    ang_t = 2.0 * np.pi * np.outer(b, a) / n
    tw_cos = np.cos(ang_t)[:, :, None]
    tw_sin = np.sin(ang_t)[:, :, None]
    ang2 = 2.0 * np.pi * np.outer(b, b) / l2
    c2, s2 = np.cos(ang2), np.sin(ang2)
    stage2 = np.block([[c2, s2], [-s2, c2]])
    f32 = lambda x: jnp.asarray(x.astype(np.float32))
    return f32(stage1), f32(tw_cos), f32(tw_sin), f32(stage2)


FNET_GROUPS_PER_SLAB = V7X_LANES // FNET_GROUP_DIM
FNET_SLABS = FNET_WIDTH // V7X_LANES
FNET_UNROLL = 4


def _fnet_channel_tables(n_positions):
    c = np.arange(FNET_GROUP_DIM, dtype=np.float64)
    ang = 2.0 * np.pi * np.outer(c, c) / FNET_GROUP_DIM
    eye = np.eye(FNET_GROUPS_PER_SLAB)
    scale = 1.0 / math.sqrt(n_positions * FNET_GROUP_DIM)
    m = np.concatenate([np.kron(eye, np.cos(ang)), np.kron(eye, np.sin(ang))], axis=0) * scale
    return jnp.asarray(m.astype(np.float32))


def _fnet_part(f_ref, o_ref, a_ref, row0, n, s1_ref, tc_ref, ts_ref, s2_ref, ch_ref, wf_ref):
    l1, l2 = _fft_split(n)
    stage1 = s1_ref[...]
    stage2 = s2_ref[...]
    chan = ch_ref[...]
    wf = wf_ref[...]

    def first(j, carry):
        xs = f_ref[pl.ds(row0 + j, l1, stride=l2), :]
        a = _dot3(stage1, xs)
        ar, ai = a[:l1], a[l1:]
        tc, ts = tc_ref[j], ts_ref[j]
        a_ref[0, pl.ds(pl.multiple_of(j * l1, l1), l1), :] = ar * tc + ai * ts
        a_ref[1, pl.ds(pl.multiple_of(j * l1, l1), l1), :] = ai * tc - ar * ts
        return carry

    lax.fori_loop(0, l2, first, 0, unroll=FNET_UNROLL)

    def second(j, carry):
        br = a_ref[0, pl.ds(j, l2, stride=l1), :]
        bi = a_ref[1, pl.ds(j, l2, stride=l1), :]
        p = _dot3(stage2, jnp.concatenate([br, bi], axis=0))
        re = _dot3(jnp.concatenate([p[:l2], p[l2:]], axis=1), chan)
        o_ref[pl.ds(row0 + j, l2, stride=l1), :] = _bdot(re.astype(BF16), wf)
        return carry

    lax.fori_loop(0, l1, second, 0, unroll=FNET_UNROLL)


def _fnet_kernel(f_ref, s1l, tcl, tsl, s2l, chl, s1c, tcc, tsc, s2c, chc, wf_ref, o_ref, a_ref, *, n_lat, n_ctx):
    _fnet_part(f_ref, o_ref, a_ref, 0, n_lat, s1l, tcl, tsl, s2l, chl, wf_ref)
    _fnet_part(f_ref, o_ref, a_ref, n_lat, n_ctx, s1c, tcc, tsc, s2c, chc, wf_ref)


def _fnet_call(st, f, w_fnet):
    gps = FNET_GROUPS_PER_SLAB
    eye = jnp.eye(gps, dtype=F32)
    wg = w_fnet.reshape(FNET_SLABS, gps, FNET_GROUP_DIM, FNET_GROUP_DIM)
    wf = (eye[None, :, None, :, None] * wg[:, :, :, None, :]).reshape(FNET_SLABS, V7X_LANES, V7X_LANES).astype(BF16)
    consts = (*_fft_tables(st.n_lat), _fnet_channel_tables(st.n_lat),
              *_fft_tables(st.n_ctx), _fnet_channel_tables(st.n_ctx))
    f3 = f.reshape(st.bsz, st.nt, FNET_WIDTH)
    blk = pl.BlockSpec((None, st.nt, V7X_LANES), lambda b, s: (b, 0, s))
    o = pl.pallas_call(
        functools.partial(_fnet_kernel, n_lat=st.n_lat, n_ctx=st.n_ctx),
        grid=(st.bsz, FNET_SLABS),
        in_specs=[blk] + [_full_spec(c.shape) for c in consts]
        + [pl.BlockSpec((None, V7X_LANES, V7X_LANES), lambda b, s: (s, 0, 0))],
        out_specs=blk,
        out_shape=jax.ShapeDtypeStruct(f3.shape, F32),
        scratch_shapes=[pltpu.VMEM((2, st.n_lat, V7X_LANES), F32)],
        compiler_params=_params("arbitrary", "arbitrary"),
        name="fnet",
    )(f3, *consts, wf)
    return o.reshape(st.t, FNET_WIDTH)


SEG_ALIGN = V7X_SUBLANES
PAIRS_PER_TILE = TOKEN_TILE * TOP_K
LOCAL_ROWS = PAIRS_PER_TILE + N_EXPERTS * SEG_ALIGN
ROUTE_LANES = V7X_LANES


def _route(h2, wr, e_bias):
    tm = h2.shape[0]
    scores = jax.nn.sigmoid(_dot3(h2, wr))
    sel = scores + e_bias
    lane = lax.broadcasted_iota(jnp.int32, sel.shape, 1).astype(F32)
    hits = []
    for _ in range(TOP_K):
        best = sel.max(axis=-1, keepdims=True)
        first = jnp.where(sel == best, lane, float(N_EXPERTS)).min(axis=-1, keepdims=True)
        hit = lane == first
        hits.append(hit)
        sel = jnp.where(hit, -jnp.inf, sel)
    chosen = hits[0]
    for hit in hits[1:]:
        chosen = jnp.logical_or(chosen, hit)
    chosen_f = jnp.where(chosen, 1.0, 0.0)
    counts = chosen_f.sum(axis=0, keepdims=True)
    seg_len = jnp.ceil(counts * (1.0 / SEG_ALIGN)) * SEG_ALIGN
    er = lax.broadcasted_iota(jnp.int32, (N_EXPERTS, N_EXPERTS), 0)
    ec = lax.broadcasted_iota(jnp.int32, (N_EXPERTS, N_EXPERTS), 1)
    lower_experts = jnp.where(er < ec, 1.0, 0.0).astype(BF16)
    seg_start = _bdot(jnp.broadcast_to(seg_len, (V7X_SUBLANES, N_EXPERTS)).astype(BF16), lower_experts)[0:1]
    r = lax.broadcasted_iota(jnp.int32, (tm, tm), 0)
    c = lax.broadcasted_iota(jnp.int32, (tm, tm), 1)
    earlier = jnp.where(c < r, 1.0, 0.0).astype(BF16)
    row_all = _bdot(earlier, chosen_f.astype(BF16)) + seg_start
    rec_lane = lax.broadcasted_iota(jnp.int32, (tm, ROUTE_LANES), 1)
    rec = jnp.zeros((tm, ROUTE_LANES), F32)
    raw = []
    for k, hit in enumerate(hits):
        rec = jnp.where(rec_lane == k, jnp.where(hit, row_all, 0.0).sum(axis=-1, keepdims=True), rec)
        raw.append(jnp.where(hit, scores, 0.0).sum(axis=-1, keepdims=True))
    total = raw[0]
    for w in raw[1:]:
        total = total + w
    for k, w in enumerate(raw):
        rec = jnp.where(rec_lane == TOP_K + k, w / total * ROUTE_SCALE, rec)
    return rec, counts.astype(jnp.int32)


def _out_ln_kernel(*refs, n_in, alpha):
    a_refs = refs[:n_in]
    w_refs = refs[n_in:2 * n_in]
    x_ref, mod_ref, g_ref, b_ref, wr_ref, eb_ref, x1_ref, h2_ref, rec_ref, cnt_ref = refs[2 * n_in:]
    y = None
    for a_ref, w_ref in zip(a_refs, w_refs):
        part = _bdot(a_ref[...].astype(BF16), w_ref[...])
        y = part if y is None else y + part
    z = alpha * x_ref[...] + mod_ref[2:3, :] * y
    x1 = _layer_norm(z, g_ref[...], b_ref[...])
    x1_ref[...] = x1
    h2 = x1 * (1.0 + mod_ref[4:5, :]) + mod_ref[3:4, :]
    h2_ref[...] = h2.astype(BF16)
    rec, counts = _route(h2, wr_ref[...], eb_ref[...])
    rec_ref[...] = rec
    cnt_ref[...] = counts


def _out_ln_call(st, acts, w_out, x, mod, layer, ln_g, ln_b, w_router, e_bias, alpha):
    ws, r0 = [], 0
    for a in acts:
        ws.append(w_out[r0:r0 + a.shape[1]].astype(BF16))
        r0 += a.shape[1]
    assert r0 == w_out.shape[0]
    row = lambda v: v.reshape(1, -1)
    return pl.pallas_call(
        functools.partial(_out_ln_kernel, n_in=len(acts), alpha=alpha),
        grid=(st.n_tiles,),
        in_specs=[st.tok_spec(a.shape[1]) for a in acts] + [_full_spec(w.shape) for w in ws]
        + [st.tok_spec(st.d), st.mod_spec(layer), _full_spec((1, st.d)), _full_spec((1, st.d)),
           _full_spec(w_router.shape), _full_spec((1, N_EXPERTS))],
        out_specs=[st.tok_spec(st.d), st.tok_spec(st.d), st.tok_spec(ROUTE_LANES),
                   pl.BlockSpec((None, 1, N_EXPERTS), lambda t: (t, 0, 0))],
        out_shape=[jax.ShapeDtypeStruct((st.t, st.d), F32), jax.ShapeDtypeStruct((st.t, st.d), BF16),
                   jax.ShapeDtypeStruct((st.t, ROUTE_LANES), F32),
                   jax.ShapeDtypeStruct((st.n_tiles, 1, N_EXPERTS), jnp.int32)],
        compiler_params=_params("arbitrary"),
        name="out_ln",
    )(*acts, *ws, x, mod, row(ln_g), row(ln_b), w_router, row(e_bias))


MOE_BLOCK = 512
ROW_CHUNK = 256
ONEHOT_STRIP = 32
GROUP_UNROLL = 4
ROW_GROUPS = LOCAL_ROWS // SEG_ALIGN
PLAN_WIDTH = (ROW_GROUPS // V7X_LANES + 1) * V7X_LANES
WAIT_SIZES = tuple(1 << b for b in range((LOCAL_ROWS - 1).bit_length() - 1, SEG_ALIGN.bit_length() - 2, -1))


def _moe_blocks(st):
    return (st.t * TOP_K + st.n_tiles * N_EXPERTS * (SEG_ALIGN - 1)) // MOE_BLOCK + N_EXPERTS


def _slot_plan(tile_counts, n_blocks):
    n_tiles = tile_counts.shape[0]
    cnt = tile_counts.reshape(n_tiles, N_EXPERTS)
    seg = (cnt + SEG_ALIGN - 1) // SEG_ALIGN * SEG_ALIGN
    local_start = jnp.cumsum(seg, axis=1) - seg
    tiles_before = jnp.cumsum(seg, axis=0) - seg
    total = seg.sum(axis=0)
    padded = (total + MOE_BLOCK - 1) // MOE_BLOCK * MOE_BLOCK
    pad_end = jnp.cumsum(padded)
    pad_start = pad_end - padded
    sorted_start = pad_start[None, :] + tiles_before
    group_row = jnp.arange(ROW_GROUPS, dtype=jnp.int32) * SEG_ALIGN
    owner = ((local_start[:, None, :] <= group_row[None, :, None])
             & (group_row[None, :, None] < (local_start + seg)[:, None, :]))
    group_dst = jnp.where(owner, (sorted_start - local_start)[:, None, :], 0).sum(-1) + group_row[None, :]
    tile_rows = jnp.broadcast_to(seg.sum(axis=1, keepdims=True), (n_tiles, PLAN_WIDTH - ROW_GROUPS))
    plan = jnp.concatenate([group_dst, tile_rows], axis=1).astype(jnp.int32).reshape(n_tiles, 1, PLAN_WIDTH)
    n_used = jnp.maximum(pad_end[-1] // MOE_BLOCK, 1)
    first_row = jnp.arange(n_blocks, dtype=jnp.int32) * MOE_BLOCK
    ends_expert = ((first_row[:, None] + MOE_BLOCK == pad_end[None, :]) & (padded[None, :] > 0)).any(-1)
    fill = (ends_expert | (first_row >= pad_end[-1])).astype(jnp.int32)
    first_block = (pad_start // MOE_BLOCK).astype(jnp.int32)
    block_count = (padded // MOE_BLOCK).astype(jnp.int32)
    return plan, first_block, block_count, n_used.astype(jnp.int32).reshape(1), fill


def _for_each_group(plan_ref, fn):
    def one(j):
        fn(pl.multiple_of(j * SEG_ALIGN, SEG_ALIGN), pl.multiple_of(plan_ref[0, j], SEG_ALIGN))

    def several(q, carry):
        for u in range(GROUP_UNROLL):
            one(q * GROUP_UNROLL + u)
        return carry

    def single(j, carry):
        one(j)
        return carry

    n = plan_ref[0, ROW_GROUPS] // SEG_ALIGN
    lax.fori_loop(0, n // GROUP_UNROLL, several, 0)
    lax.fori_loop(n // GROUP_UNROLL * GROUP_UNROLL, n, single, 0)


def _wait_tile_rows(plan_ref, make_copy):
    rows = plan_ref[0, ROW_GROUPS]
    for size in WAIT_SIZES:
        @pl.when((rows & size) != 0)
        def _(size=size):
            make_copy(size).wait()


def _pack_halves(x):
    bits = pltpu.bitcast(x, jnp.uint32)
    half = x.shape[1] // 2
    return (bits[:, :half] >> 16) | (bits[:, half:] & jnp.uint32(0xFFFF0000))


def _unpack_halves(w):
    lo = pltpu.bitcast(w << 16, F32).astype(BF16)
    hi = pltpu.bitcast(w & jnp.uint32(0xFFFF0000), F32).astype(BF16)
    return lo, hi


def _dispatch_kernel(fill_ref, plan_ref, plan1_ref, plan2_ref, rec_ref, h_ref, xs_ref, loc_ref, zero_ref, onehot_ref,
                     zsem, sem,
                     *, n_blocks, n_tiles):
    i = pl.program_id(0)
    slot = i % 2

    @pl.when(i == 0)
    def _():
        zero_ref[...] = jnp.zeros_like(zero_ref)

        def fill(b):
            return pltpu.make_async_copy(zero_ref, xs_ref.at[pl.ds(pl.multiple_of(b * MOE_BLOCK, MOE_BLOCK), MOE_BLOCK)],
                                         zsem)

        def start(b, c):
            @pl.when(fill_ref[b] > 0)
            def _():
                fill(b).start()
            return c

        def wait(b, c):
            @pl.when(fill_ref[b] > 0)
            def _():
                fill(b).wait()
            return c

        lax.fori_loop(0, n_blocks, start, 0)
        lax.fori_loop(0, n_blocks, wait, 0)

    def piece(s):
        def copy(local, dst, size):
            return pltpu.make_async_copy(loc_ref.at[s, pl.ds(local, size)], xs_ref.at[pl.ds(dst, size)], sem.at[s])
        return copy

    def rows_done(s):
        return lambda size: piece(s)(0, 0, size)

    @pl.when(i >= 2)
    def _():
        _wait_tile_rows(plan2_ref, rows_done(slot))

    rows_of = rec_ref[...].T
    x = h_ref[...]
    for c0 in range(0, LOCAL_ROWS, ROW_CHUNK):
        for r0 in range(c0, c0 + ROW_CHUNK, ONEHOT_STRIP):
            local_row = (lax.broadcasted_iota(jnp.int32, (ONEHOT_STRIP, TOKEN_TILE), 0) + r0).astype(F32)
            onehot = jnp.zeros((ONEHOT_STRIP, TOKEN_TILE), F32)
            for k in range(TOP_K):
                onehot = jnp.where(local_row == rows_of[k:k + 1, :], 1.0, onehot)
            onehot_ref[r0 - c0:r0 - c0 + ONEHOT_STRIP, :] = onehot.astype(BF16)
        loc_ref[slot, c0:c0 + ROW_CHUNK, :] = _pack_halves(_bdot(onehot_ref[...], x))

    _for_each_group(plan_ref, lambda local, dst: piece(slot)(local, dst, SEG_ALIGN).start())

    @pl.when(i == n_tiles - 1)
    def _():
        if n_tiles >= 2:
            _wait_tile_rows(plan1_ref, rows_done(1 - slot))
        _wait_tile_rows(plan_ref, rows_done(slot))


def _dispatch_call(st, h2, rec, plan, fill, n_blocks):
    half = st.d // 2
    plan_spec = lambda back: pl.BlockSpec((None, 1, PLAN_WIDTH), lambda t, *_: (jnp.maximum(t - back, 0), 0, 0),
                                          memory_space=pltpu.SMEM)
    grid_spec = pltpu.PrefetchScalarGridSpec(
        num_scalar_prefetch=1,
        grid=(st.n_tiles,),
        in_specs=[plan_spec(0), plan_spec(1), plan_spec(2),
                  pl.BlockSpec((TOKEN_TILE, ROUTE_LANES), lambda t, *_: (t, 0)),
                  pl.BlockSpec((TOKEN_TILE, st.d), lambda t, *_: (t, 0))],
        out_specs=pl.BlockSpec(memory_space=pl.ANY),
        scratch_shapes=[pltpu.VMEM((2, LOCAL_ROWS, half), jnp.uint32), pltpu.VMEM((MOE_BLOCK, half), jnp.uint32),
                        pltpu.VMEM((ROW_CHUNK, TOKEN_TILE), BF16),
                        pltpu.SemaphoreType.DMA, pltpu.SemaphoreType.DMA((2,))],
    )
    return pl.pallas_call(
        functools.partial(_dispatch_kernel, n_blocks=n_blocks, n_tiles=st.n_tiles),
        grid_spec=grid_spec,
        out_shape=jax.ShapeDtypeStruct((n_blocks * MOE_BLOCK, half), jnp.uint32),
        compiler_params=_params("arbitrary"),
        name="moe_dispatch",
    )(fill, plan, plan, plan, rec, h2)


EXPERT_BUFFERS = 3


def _experts_kernel(first_ref, count_ref, nu_ref, xs_ref, wg_ref, wu_ref, wd_ref, ys_ref, xbuf, ybuf, wg_s, wu_s, wd_s,
                    xsem, ysem, *, n_blocks):
    e = pl.program_id(0)
    n_used = nu_ref[0]

    def rows(b):
        return pl.ds(pl.multiple_of(b * MOE_BLOCK, MOE_BLOCK), MOE_BLOCK)

    def fetch(b):
        s = b % EXPERT_BUFFERS
        return pltpu.make_async_copy(xs_ref.at[rows(b)], xbuf.at[s], xsem.at[s])

    def store(b):
        s = b % EXPERT_BUFFERS
        return pltpu.make_async_copy(ybuf.at[s], ys_ref.at[rows(b)], ysem.at[s])

    @pl.when(e == 0)
    def _():
        for b in range(EXPERT_BUFFERS - 1):
            @pl.when(b < n_used)
            def _(b=b):
                fetch(b).start()

    @pl.when(count_ref[e] > 0)
    def _():
        wg_s[...] = wg_ref[...].astype(BF16)
        wu_s[...] = wu_ref[...].astype(BF16)
        wd_s[...] = wd_ref[...].astype(BF16)

    def block(b, carry):
        @pl.when(b + EXPERT_BUFFERS - 1 < n_used)
        def _():
            fetch(b + EXPERT_BUFFERS - 1).start()

        fetch(b).wait()

        @pl.when(b >= EXPERT_BUFFERS)
        def _():
            store(b - EXPERT_BUFFERS).wait()

        slot = b % EXPERT_BUFFERS
        half = wg_s.shape[0] // 2
        lo, hi = _unpack_halves(xbuf[slot])
        gate = _bdot(lo, wg_s[:half, :]) + _bdot(hi, wg_s[half:, :])
        up = _bdot(lo, wu_s[:half, :]) + _bdot(hi, wu_s[half:, :])
        y = _bdot((_silu(gate) * up).astype(BF16), wd_s[...])
        ybuf[slot] = _pack_halves(y.astype(BF16).astype(F32))
        store(b).start()
        return carry

    lax.fori_loop(first_ref[e], first_ref[e] + count_ref[e], block, 0)

    @pl.when(e == N_EXPERTS - 1)
    def _():
        def drain(b, carry):
            store(b).wait()
            return carry

        lax.fori_loop(jnp.maximum(n_used - EXPERT_BUFFERS, 0), n_used, drain, 0)
        ybuf[0] = jnp.zeros(ybuf.shape[1:], ybuf.dtype)

        def zero_tail(b):
            return pltpu.make_async_copy(ybuf.at[0], ys_ref.at[rows(b)], ysem.at[0])

        def start(b, carry):
            zero_tail(b).start()
            return carry

        def wait(b, carry):
            zero_tail(b).wait()
            return carry

        lax.fori_loop(n_used, n_blocks, start, 0)
        lax.fori_loop(n_used, n_blocks, wait, 0)


def _experts_call(st, xs, first_block, block_count, n_used, w_gate, w_up, w_down, layer, n_blocks):
    half = st.d // 2
    w_in_spec = pl.BlockSpec((None, None, st.d, EXPERT_DIM), lambda e, *_: (layer, e, 0, 0))
    grid_spec = pltpu.PrefetchScalarGridSpec(
        num_scalar_prefetch=3,
        grid=(N_EXPERTS,),
        in_specs=[pl.BlockSpec(memory_space=pl.ANY),
                  w_in_spec, w_in_spec,
                  pl.BlockSpec((None, None, EXPERT_DIM, st.d), lambda e, *_: (layer, e, 0, 0))],
        out_specs=pl.BlockSpec(memory_space=pl.ANY),
        scratch_shapes=[pltpu.VMEM((EXPERT_BUFFERS, MOE_BLOCK, half), jnp.uint32),
                        pltpu.VMEM((EXPERT_BUFFERS, MOE_BLOCK, half), jnp.uint32),
                        pltpu.VMEM((st.d, EXPERT_DIM), BF16), pltpu.VMEM((st.d, EXPERT_DIM), BF16),
                        pltpu.VMEM((EXPERT_DIM, st.d), BF16),
                        pltpu.SemaphoreType.DMA((EXPERT_BUFFERS,)), pltpu.SemaphoreType.DMA((EXPERT_BUFFERS,))],
    )
    return pl.pallas_call(
        functools.partial(_experts_kernel, n_blocks=n_blocks),
        grid_spec=grid_spec,
        out_shape=jax.ShapeDtypeStruct(xs.shape, jnp.uint32),
        compiler_params=_params("arbitrary"),
        name="moe_experts",
    )(first_block, block_count, n_used, xs, w_gate, w_up, w_down)


def _ffn_ln_kernel(plan_ref, plan_next_ref, rec_ref, h2_ref, x1_ref, sg_ref, su_ref, sd_ref, mod_ref, modn_ref, g_ref,
                   b_ref, ys_ref, x2_ref, hn_ref, loc_ref, weight_ref, sem, *, alpha, n_tiles):
    i = pl.program_id(0)
    slot = i % 2

    def piece(s):
        def copy(local, src, size):
            return pltpu.make_async_copy(ys_ref.at[pl.ds(src, size)], loc_ref.at[s, pl.ds(local, size)], sem.at[s])
        return copy

    @pl.when(i == 0)
    def _():
        loc_ref[...] = jnp.zeros_like(loc_ref)
        _for_each_group(plan_ref, lambda local, src: piece(0)(local, src, SEG_ALIGN).start())

    @pl.when(i + 1 < n_tiles)
    def _():
        _for_each_group(plan_next_ref, lambda local, src: piece(1 - slot)(local, src, SEG_ALIGN).start())

    h2 = h2_ref[...]
    a = _silu(_bdot(h2, sg_ref[...])) * _bdot(h2, su_ref[...])
    shared = _bdot(a.astype(BF16), sd_ref[...])

    _wait_tile_rows(plan_ref, lambda size: piece(slot)(0, 0, size))
    rec = rec_ref[...]
    half = loc_ref.shape[2]
    routed_lo = jnp.zeros((TOKEN_TILE, half), F32)
    routed_hi = jnp.zeros((TOKEN_TILE, half), F32)
    for c0 in range(0, LOCAL_ROWS, ROW_CHUNK):
        for t0 in range(0, TOKEN_TILE, ONEHOT_STRIP):
            local_row = (lax.broadcasted_iota(jnp.int32, (ONEHOT_STRIP, ROW_CHUNK), 1) + c0).astype(F32)
            part = rec[t0:t0 + ONEHOT_STRIP, :]
            weight = jnp.zeros((ONEHOT_STRIP, ROW_CHUNK), F32)
            for k in range(TOP_K):
                weight = jnp.where(local_row == part[:, k:k + 1], part[:, TOP_K + k:TOP_K + k + 1], weight)
            weight_ref[t0:t0 + ONEHOT_STRIP, :] = weight.astype(BF16)
        lo, hi = _unpack_halves(loc_ref[slot, c0:c0 + ROW_CHUNK, :])
        weight = weight_ref[...]
        routed_lo = routed_lo + _bdot(weight, lo)
        routed_hi = routed_hi + _bdot(weight, hi)
    ff = jnp.concatenate([routed_lo, routed_hi], axis=1) + shared
    z = alpha * x1_ref[...] + mod_ref[5:6, :] * ff
    x2 = _layer_norm(z, g_ref[...], b_ref[...])
    x2_ref[...] = x2
    hn_ref[...] = (x2 * (1.0 + modn_ref[1:2, :]) + modn_ref[0:1, :]).astype(BF16)


def _ffn_ln_call(st, ys, plan, rec, h2, x1, s_gate, s_up, s_down, mod, layer, next_layer, ln_g, ln_b, alpha):
    row = lambda v: v.reshape(1, -1)
    sg, su, sd = s_gate.astype(BF16), s_up.astype(BF16), s_down.astype(BF16)
    last = st.n_tiles - 1
    return pl.pallas_call(
        functools.partial(_ffn_ln_kernel, alpha=alpha, n_tiles=st.n_tiles),
        grid=(st.n_tiles,),
        in_specs=[pl.BlockSpec((None, 1, PLAN_WIDTH), lambda t: (t, 0, 0), memory_space=pltpu.SMEM),
                  pl.BlockSpec((None, 1, PLAN_WIDTH), lambda t: (jnp.minimum(t + 1, last), 0, 0),
                               memory_space=pltpu.SMEM),
                  st.tok_spec(ROUTE_LANES), st.tok_spec(st.d), st.tok_spec(st.d),
                  _full_spec(sg.shape), _full_spec(su.shape), _full_spec(sd.shape),
                  st.mod_spec(layer), st.mod_spec(next_layer), _full_spec((1, st.d)), _full_spec((1, st.d)),
                  pl.BlockSpec(memory_space=pl.ANY)],
        out_specs=[st.tok_spec(st.d), st.tok_spec(st.d)],
        out_shape=[jax.ShapeDtypeStruct((st.t, st.d), F32), jax.ShapeDtypeStruct((st.t, st.d), BF16)],
        scratch_shapes=[pltpu.VMEM((2, LOCAL_ROWS, st.d // 2), jnp.uint32),
                        pltpu.VMEM((TOKEN_TILE, ROW_CHUNK), BF16), pltpu.SemaphoreType.DMA((2,))],
        compiler_params=_params("arbitrary"),
        name="ffn_ln",
    )(plan, plan, rec, h2, x1, sg, su, sd, mod, mod, row(ln_g), row(ln_b), ys)


def _na_in_kernel(h_ref, w_ref, o_ref):
    h = h_ref[...]
    o_ref[:, :NA_WIDTH] = (_bdot(h, w_ref[:, :NA_WIDTH]) * SCORE_SCALE).astype(BF16)
    o_ref[:, NA_WIDTH:] = _bdot(h, w_ref[:, NA_WIDTH:]).astype(BF16)


def _na_in_call(st, h, w_in):
    return pl.pallas_call(
        _na_in_kernel,
        grid=(st.n_tiles,),
        in_specs=[st.tok_spec(st.d), _full_spec(w_in.shape)],
        out_specs=st.tok_spec(3 * NA_WIDTH),
        out_shape=jax.ShapeDtypeStruct((st.t, 3 * NA_WIDTH), BF16),
        compiler_params=_params("arbitrary"),
        name="na_in",
    )(h, w_in)


def _na_geometry(n_lat):
    rows = n_lat // GRID_W
    kh, kw, qr = min(NA_KH, rows), min(NA_KW, GRID_W), NA_Q_ROWS
    nbr = min(qr + kh - 1, rows)
    col = np.arange(GRID_W)
    col_start = np.clip(col - kw // 2, 0, GRID_W - kw)
    in_col = (col[None, :] >= col_start[:, None]) & (col[None, :] < col_start[:, None] + kw)
    dc = np.clip(col[None, :] - col[:, None] + NA_KW - 1, 0, 2 * NA_KW - 2)
    starts, variant_of, variants = [], [], {}
    for i in range(rows // qr):
        qrow = i * qr + np.arange(qr)
        rstart = np.clip(qrow - kh // 2, 0, rows - kh)
        bs = min(int(rstart[0]), rows - nbr)
        krow = bs + np.arange(nbr)
        in_row = (krow[None, :] >= rstart[:, None]) & (krow[None, :] < rstart[:, None] + kh)
        dr = np.clip(krow[None, :] - qrow[:, None] + NA_KH - 1, 0, 2 * NA_KH - 2)
        key = (in_row.tobytes(), dr.tobytes())
        if key not in variants:
            mask = (in_row[:, None, :, None] & in_col[None, :, None, :]).reshape(qr * GRID_W, nbr * GRID_W)
            variants[key] = (len(variants), dr, mask)
        starts.append(bs)
        variant_of.append(variants[key][0])
    ordered = sorted(variants.values(), key=lambda v: v[0])
    return nbr, np.asarray(starts, np.int32), np.asarray(variant_of, np.int32), [(v[1], v[2]) for v in ordered], dc


def _na_bias_tables(rpb, n_lat):
    nbr, starts, variant_of, variants, dc = _na_geometry(n_lat)
    n_dr, n_dc = 2 * NA_KH - 1, 2 * NA_KW - 1
    col_sel = jnp.asarray((dc[:, :, None] == np.arange(n_dc)).astype(np.float32))
    by_col = jnp.einsum("hrc,wuc->hrwu", rpb, col_sel, precision=lax.Precision.HIGHEST)
    tables = []
    for dr, mask in variants:
        row_sel = jnp.asarray((dr[:, :, None] == np.arange(n_dr)).astype(np.float32))
        bias = jnp.einsum("ajr,hrwu->hawju", row_sel, by_col, precision=lax.Precision.HIGHEST)
        bias = bias.reshape(NA_HEADS, NA_Q_ROWS * GRID_W, nbr * GRID_W)
        tables.append(jnp.where(jnp.asarray(mask)[None], bias * LOG2_E, NEG_INF))
    return nbr, starts, variant_of, jnp.stack(tables)


NA_Q_TILE = NA_Q_ROWS * GRID_W


def _na_kernel(start_ref, var_ref, q_ref, k_ref, v_ref, bias_ref, o_ref, *, n_lat, band):
    i = pl.program_id(1)
    is_lat = i < n_lat // NA_Q_TILE

    @pl.when(is_lat)
    def _():
        off = pl.multiple_of(start_ref[i] * GRID_W, GRID_W)
        for h in range(NA_HEADS):
            sl = slice(h * HEAD_DIM, (h + 1) * HEAD_DIM)
            local = (k_ref[pl.ds(off, band), sl], v_ref[pl.ds(off, band), sl])
            ctx = (k_ref[n_lat:, sl], v_ref[n_lat:, sl])
            o_ref[:, sl] = _attend(q_ref[:, sl], [local, ctx], [bias_ref[h], None]).astype(BF16)

    @pl.when(jnp.logical_not(is_lat))
    def _():
        for h in range(NA_HEADS):
            sl = slice(h * HEAD_DIM, (h + 1) * HEAD_DIM)
            o_ref[:, sl] = _attend(q_ref[:, sl], [(k_ref[n_lat:, sl], v_ref[n_lat:, sl])], [None]).astype(BF16)


def _na_call(st, qkv, rpb):
    nbr, starts, variant_of, bias = _na_bias_tables(rpb, st.n_lat)
    band = nbr * GRID_W
    n_q = st.nt // NA_Q_TILE
    pad = n_q - starts.shape[0]
    starts = jnp.asarray(np.concatenate([starts, np.zeros(pad, np.int32)]))
    variant_of = jnp.asarray(np.concatenate([variant_of, np.zeros(pad, np.int32)]))
    qkv3 = qkv.reshape(st.bsz, st.nt, 3 * NA_WIDTH)
    grid_spec = pltpu.PrefetchScalarGridSpec(
        num_scalar_prefetch=2,
        grid=(st.bsz, n_q),
        in_specs=[pl.BlockSpec((None, NA_Q_TILE, NA_WIDTH), lambda b, i, s, v: (b, i, 0)),
                  pl.BlockSpec((None, st.nt, NA_WIDTH), lambda b, i, s, v: (b, 0, 1)),
                  pl.BlockSpec((None, st.nt, NA_WIDTH), lambda b, i, s, v: (b, 0, 2)),
                  pl.BlockSpec((None, NA_HEADS, NA_Q_TILE, band), lambda b, i, s, v: (v[i], 0, 0, 0))],
        out_specs=pl.BlockSpec((None, NA_Q_TILE, NA_WIDTH), lambda b, i, s, v: (b, i, 0)),
    )
    o = pl.pallas_call(
        functools.partial(_na_kernel, n_lat=st.n_lat, band=band),
        grid_spec=grid_spec,
        out_shape=jax.ShapeDtypeStruct((st.bsz, st.nt, NA_WIDTH), BF16),
        compiler_params=_params("arbitrary", "arbitrary"),
        name="na_attn",
    )(starts, variant_of, qkv3, qkv3, qkv3, bias)
    return o.reshape(st.t, NA_WIDTH)


def kernel(x, c, ctx, c_ctx, w_mod, b_mod, ln1_g, ln1_b, ln2_g, ln2_b, ab_w_in, ab_w_fnet, ab_q_norm, ab_k_norm,
           ab_w_out, na_w_in, na_rpb, na_w_out, moe_w_router, moe_bias, moe_w_gate, moe_w_up, moe_w_down,
           sh_w_gate, sh_w_up, sh_w_down):
    bsz, n_lat, d = x.shape
    n_ctx = ctx.shape[1]
    depth = w_mod.shape[0]
    st = _Stream(bsz, n_lat, n_ctx, d)
    alpha = (2 * depth) ** 0.25

    cc = jnp.concatenate([c, c_ctx[None, :], jnp.zeros((MOD_ROWS - bsz - 1, d), F32)], axis=0)
    mod = _mod_call(cc, w_mod, b_mod).reshape(depth, MOD_ROWS, 6, d)
    xs = jnp.concatenate([x, ctx], axis=1).reshape(st.t, d)
    h = _modulate_call(st, xs, mod, 0)
    rope = _rope_tables(n_lat, n_ctx)
    n_blocks = _moe_blocks(st)

    for l in range(depth):
        j = l // 2
        if l % 2 == 0:
            f, q, k, v = _ab_in_call(st, h, ab_w_in[j].astype(BF16), ab_q_norm[j], ab_k_norm[j], rope)
            acts = [_fnet_call(st, f, ab_w_fnet[j]), _gqa_call(st, q, k, v)]
            w_out = ab_w_out[j]
        else:
            qkv = _na_in_call(st, h, na_w_in[j].astype(BF16))
            acts = [_na_call(st, qkv, na_rpb[j])]
            w_out = na_w_out[j]
        x1, h2, rec, tile_counts = _out_ln_call(st, acts, w_out, xs, mod, l, ln1_g[l], ln1_b[l],
                                                moe_w_router[l], moe_bias[l], alpha)
        plan, first_block, block_count, n_used, fill = _slot_plan(tile_counts, n_blocks)
        rows = _dispatch_call(st, h2, rec, plan, fill, n_blocks)
        ys = _experts_call(st, rows, first_block, block_count, n_used, moe_w_gate, moe_w_up, moe_w_down, l, n_blocks)
        xs, h = _ffn_ln_call(st, ys, plan, rec, h2, x1, sh_w_gate[l], sh_w_up[l], sh_w_down[l], mod, l,
                             min(l + 1, depth - 1), ln2_g[l], ln2_b[l], alpha)
    return xs.reshape(bsz, st.nt, d)[:, :n_lat]
```

```python
import functools
import math

import numpy as np
import jax
import jax.numpy as jnp
from jax import lax
from jax.experimental import pallas as pl
from jax.experimental.pallas import tpu as pltpu

GRID_W = 64
HEAD_DIM = 128
FNET_GROUPS = 4
FNET_GROUP_DIM = 64
FNET_WIDTH = FNET_GROUPS * FNET_GROUP_DIM
GQA_Q_HEADS = 6
GQA_KV_HEADS = 2
GQA_GROUP = GQA_Q_HEADS // GQA_KV_HEADS
ROPE_THETA = 10000.0
NA_HEADS = 8
NA_WIDTH = NA_HEADS * HEAD_DIM
NA_KH = 8
NA_KW = 16
NA_Q_ROWS = 2
NEG_INF = -1e30
N_EXPERTS = 64
TOP_K = 8
EXPERT_DIM = 256
ROUTE_SCALE = 2.5
LN_EPS = 1e-6
RMS_EPS = 1e-6
ATTN_SCALE = HEAD_DIM ** -0.5
LOG2_E = math.log2(math.e)
SCORE_SCALE = ATTN_SCALE * LOG2_E

V7X_LANES = 128
V7X_SUBLANES = 8
V7X_VMEM_LIMIT_BYTES = 56 * 1024 * 1024

TOKEN_TILE = 256
MOD_ROWS = 8

F32 = jnp.float32
BF16 = jnp.bfloat16


def _params(*sem):
    return pltpu.CompilerParams(dimension_semantics=sem, vmem_limit_bytes=V7X_VMEM_LIMIT_BYTES)


def _bdot(a, b):
    return jnp.dot(a, b, preferred_element_type=F32)


def _bdot_t(a, b):
    return lax.dot_general(a, b, (((1,), (1,)), ((), ())), preferred_element_type=F32)


def _split(x):
    hi = x.astype(BF16)
    lo = (x - hi.astype(F32)).astype(BF16)
    return hi, lo


def _dot3(a, b):
    ah, al = _split(a)
    bh, bl = _split(b)
    return _bdot(ah, bh) + (_bdot(ah, bl) + _bdot(al, bh))


def _silu(x):
    return x * jax.nn.sigmoid(x)


def _layer_norm(z, g, b):
    mu = jnp.mean(z, axis=-1, keepdims=True)
    zc = z - mu
    var = jnp.mean(zc * zc, axis=-1, keepdims=True)
    return zc * lax.rsqrt(var + LN_EPS) * g + b


class _Stream:
    def __init__(self, bsz, n_lat, n_ctx, d):
        assert n_lat % TOKEN_TILE == 0 and n_ctx % TOKEN_TILE == 0
        assert bsz < MOD_ROWS
        self.bsz, self.n_lat, self.n_ctx, self.d = bsz, n_lat, n_ctx, d
        self.nt = n_lat + n_ctx
        self.t = bsz * self.nt
        self.tiles_per_sample = self.nt // TOKEN_TILE
        self.lat_tiles = n_lat // TOKEN_TILE
        self.n_tiles = self.t // TOKEN_TILE

    def mod_row(self, tile):
        return jnp.where(tile % self.tiles_per_sample < self.lat_tiles, tile // self.tiles_per_sample, self.bsz)

    def mod_spec(self, layer):
        return pl.BlockSpec((None, None, 6, self.d), lambda t: (layer, self.mod_row(t), 0, 0))

    def tok_spec(self, width):
        return pl.BlockSpec((TOKEN_TILE, width), lambda t: (t, 0))


def _full_spec(shape):
    nd = len(shape)
    return pl.BlockSpec(shape, lambda *_: (0,) * nd)


def _mod_kernel(cc_ref, w_ref, b_ref, o_ref):
    o_ref[...] = _dot3(_silu(cc_ref[...]), w_ref[...]) + b_ref[...]


def _mod_call(cc, w_mod, b_mod):
    depth, d, n = w_mod.shape
    tn = n // 4
    return pl.pallas_call(
        _mod_kernel,
        grid=(depth, n // tn),
        in_specs=[pl.BlockSpec((MOD_ROWS, d), lambda l, j: (0, 0)),
                  pl.BlockSpec((None, d, tn), lambda l, j: (l, 0, j)),
                  pl.BlockSpec((None, 1, tn), lambda l, j: (l, 0, j))],
        out_specs=pl.BlockSpec((None, MOD_ROWS, tn), lambda l, j: (l, 0, j)),
        out_shape=jax.ShapeDtypeStruct((depth, MOD_ROWS, n), F32),
        compiler_params=_params("arbitrary", "arbitrary"),
        name="mod",
    )(cc, w_mod, b_mod.reshape(depth, 1, n))


def _modulate_kernel(x_ref, mod_ref, h_ref):
    h_ref[...] = (x_ref[...] * (1.0 + mod_ref[1:2, :]) + mod_ref[0:1, :]).astype(BF16)


def _modulate_call(st, x, mod, layer):
    return pl.pallas_call(
        _modulate_kernel,
        grid=(st.n_tiles,),
        in_specs=[st.tok_spec(st.d), st.mod_spec(layer)],
        out_specs=st.tok_spec(st.d),
        out_shape=jax.ShapeDtypeStruct((st.t, st.d), BF16),
        compiler_params=_params("arbitrary"),
        name="modulate",
    )(x, mod)


def _rope_tables(n_lat, n_ctx):
    half = HEAD_DIM // 2
    nf = half // 2
    t = np.arange(n_lat)
    inv = ROPE_THETA ** (-(2.0 / half) * np.arange(nf, dtype=np.float64))
    ang_r = (t // GRID_W)[:, None] * inv
    ang_c = (t % GRID_W)[:, None] * inv
    zeros = np.zeros_like(ang_r)
    cos = np.concatenate([np.cos(ang_r), np.cos(ang_r), np.cos(ang_c), np.cos(ang_c)], axis=1)
    sin_fwd = np.concatenate([-np.sin(ang_r), zeros, -np.sin(ang_c), zeros], axis=1)
    sin_bwd = np.concatenate([zeros, np.sin(ang_r), zeros, np.sin(ang_c)], axis=1)
    pad = lambda a, v: np.concatenate([a, np.full((n_ctx, HEAD_DIM), v)], axis=0).astype(np.float32)
    return jnp.asarray(pad(cos, 1.0)), jnp.asarray(pad(sin_fwd, 0.0)), jnp.asarray(pad(sin_bwd, 0.0))


def _ab_in_kernel(h_ref, w_ref, qg_ref, kg_ref, cos_ref, sf_ref, sb_ref, f_ref, q_ref, k_ref, v_ref):
    acc = _bdot(h_ref[...], w_ref[...])
    cos, sf, sb = cos_ref[...], sf_ref[...], sb_ref[...]
    nf = HEAD_DIM // 4

    def norm_rope(xh, gain):
        ms = jnp.mean(xh * xh, axis=-1, keepdims=True)
        y = xh * lax.rsqrt(ms + RMS_EPS) * gain
        return y * cos + pltpu.roll(y, HEAD_DIM - nf, 1) * sf + pltpu.roll(y, nf, 1) * sb

    f_ref[...] = acc[:, :FNET_WIDTH]
    q0 = FNET_WIDTH
    k0 = q0 + GQA_Q_HEADS * HEAD_DIM
    v0 = k0 + GQA_KV_HEADS * HEAD_DIM
    for h in range(GQA_Q_HEADS):
        xh = acc[:, q0 + h * HEAD_DIM:q0 + (h + 1) * HEAD_DIM]
        q_ref[:, h * HEAD_DIM:(h + 1) * HEAD_DIM] = (norm_rope(xh, qg_ref[...]) * SCORE_SCALE).astype(BF16)
    for h in range(GQA_KV_HEADS):
        xh = acc[:, k0 + h * HEAD_DIM:k0 + (h + 1) * HEAD_DIM]
        k_ref[:, h * HEAD_DIM:(h + 1) * HEAD_DIM] = norm_rope(xh, kg_ref[...]).astype(BF16)
    v_ref[...] = acc[:, v0:].astype(BF16)


def _ab_in_call(st, h, w_in, q_gain, k_gain, rope):
    nq = GQA_Q_HEADS * HEAD_DIM
    nkv = GQA_KV_HEADS * HEAD_DIM
    pos_spec = pl.BlockSpec((TOKEN_TILE, HEAD_DIM), lambda t: (t % st.tiles_per_sample, 0))
    return pl.pallas_call(
        _ab_in_kernel,
        grid=(st.n_tiles,),
        in_specs=[st.tok_spec(st.d), _full_spec(w_in.shape), _full_spec((1, HEAD_DIM)), _full_spec((1, HEAD_DIM)),
                  pos_spec, pos_spec, pos_spec],
        out_specs=[st.tok_spec(FNET_WIDTH), st.tok_spec(nq), st.tok_spec(nkv), st.tok_spec(nkv)],
        out_shape=[jax.ShapeDtypeStruct((st.t, FNET_WIDTH), F32), jax.ShapeDtypeStruct((st.t, nq), BF16),
                   jax.ShapeDtypeStruct((st.t, nkv), BF16), jax.ShapeDtypeStruct((st.t, nkv), BF16)],
        compiler_params=_params("arbitrary"),
        name="ab_in",
    )(h, w_in, q_gain.reshape(1, HEAD_DIM), k_gain.reshape(1, HEAD_DIM), *rope)


def _attend(q, keys_values, biases):
    scores = []
    for (k, _), bias in zip(keys_values, biases):
        s = _bdot_t(q, k)
        scores.append(s if bias is None else s + bias)
    m = scores[0].max(axis=-1, keepdims=True)
    for s in scores[1:]:
        m = jnp.maximum(m, s.max(axis=-1, keepdims=True))
    num = None
    den = None
    for s, (_, v) in zip(scores, keys_values):
        p = jnp.exp2(s - m)
        pv = _bdot(p.astype(BF16), v)
        ps = p.sum(axis=-1, keepdims=True)
        num = pv if num is None else num + pv
        den = ps if den is None else den + ps
    return num / den


def _gqa_kernel(q_ref, k_ref, v_ref, o_ref, *, n_lat, lat_tiles):
    def run(k, v):
        for h in range(GQA_GROUP):
            sl = slice(h * HEAD_DIM, (h + 1) * HEAD_DIM)
            o_ref[:, sl] = _attend(q_ref[:, sl], [(k, v)], [None]).astype(BF16)

    is_lat = pl.program_id(2) < lat_tiles

    @pl.when(is_lat)
    def _():
        run(k_ref[...], v_ref[...])

    @pl.when(jnp.logical_not(is_lat))
    def _():
        run(k_ref[n_lat:, :], v_ref[n_lat:, :])


def _gqa_call(st, q, k, v):
    gw = GQA_GROUP * HEAD_DIM
    q3 = q.reshape(st.bsz, st.nt, GQA_Q_HEADS * HEAD_DIM)
    k3 = k.reshape(st.bsz, st.nt, GQA_KV_HEADS * HEAD_DIM)
    v3 = v.reshape(st.bsz, st.nt, GQA_KV_HEADS * HEAD_DIM)
    q_spec = pl.BlockSpec((None, TOKEN_TILE, gw), lambda b, g, i: (b, i, g))
    kv_spec = pl.BlockSpec((None, st.nt, HEAD_DIM), lambda b, g, i: (b, 0, g))
    o = pl.pallas_call(
        functools.partial(_gqa_kernel, n_lat=st.n_lat, lat_tiles=st.lat_tiles),
        grid=(st.bsz, GQA_KV_HEADS, st.tiles_per_sample),
        in_specs=[q_spec, kv_spec, kv_spec],
        out_specs=q_spec,
        out_shape=jax.ShapeDtypeStruct(q3.shape, BF16),
        compiler_params=_params("arbitrary", "arbitrary", "arbitrary"),
        name="gqa",
    )(q3, k3, v3)
    return o.reshape(st.t, GQA_Q_HEADS * HEAD_DIM)


def _fft_split(n):
    l1 = 1 << ((n.bit_length() - 1 + 1) // 2)
    assert n % l1 == 0 and n == l1 * (n // l1)
    return l1, n // l1


def _fft_tables(n):
    l1, l2 = _fft_split(n)
    a = np.arange(l1, dtype=np.float64)
    ang1 = 2.0 * np.pi * np.outer(a, a) / l1
    stage1 = np.concatenate([np.cos(ang1), -np.sin(ang1)], axis=0)
    b = np.arange(l2, dtype=np.float64)
    ang_t = 2.0 * np.pi * np.outer(b, a) / n
    tw_cos = np.cos(ang_t)[:, :, None]
    tw_sin = np.sin(ang_t)[:, :, None]
    ang2 = 2.0 * np.pi * np.outer(b, b) / l2
    c2, s2 = np.cos(ang2), np.sin(ang2)
    stage2 = np.block([[c2, s2], [-s2, c2]])
    f32 = lambda x: jnp.asarray(x.astype(np.float32))
    return f32(stage1), f32(tw_cos), f32(tw_sin), f32(stage2)


FNET_GROUPS_PER_SLAB = V7X_LANES // FNET_GROUP_DIM
FNET_SLABS = FNET_WIDTH // V7X_LANES
FNET_UNROLL = 4


def _fnet_channel_tables(n_positions):
    c = np.arange(FNET_GROUP_DIM, dtype=np.float64)
    ang = 2.0 * np.pi * np.outer(c, c) / FNET_GROUP_DIM
    eye = np.eye(FNET_GROUPS_PER_SLAB)
    scale = 1.0 / math.sqrt(n_positions * FNET_GROUP_DIM)
    m = np.concatenate([np.kron(eye, np.cos(ang)), np.kron(eye, np.sin(ang))], axis=0) * scale
    return jnp.asarray(m.astype(np.float32))


def _fnet_part(f_ref, o_ref, a_ref, row0, n, s1_ref, tc_ref, ts_ref, s2_ref, ch_ref, wf_ref):
    l1, l2 = _fft_split(n)
    stage1 = s1_ref[...]
    stage2 = s2_ref[...]
    chan = ch_ref[...]
    wf = wf_ref[...]

    def first(j, carry):
        xs = f_ref[pl.ds(row0 + j, l1, stride=l2), :]
        a = _dot3(stage1, xs)
        ar, ai = a[:l1], a[l1:]
        tc, ts = tc_ref[j], ts_ref[j]
        a_ref[0, pl.ds(pl.multiple_of(j * l1, l1), l1), :] = ar * tc + ai * ts
        a_ref[1, pl.ds(pl.multiple_of(j * l1, l1), l1), :] = ai * tc - ar * ts
        return carry

    lax.fori_loop(0, l2, first, 0, unroll=FNET_UNROLL)

    def second(j, carry):
        br = a_ref[0, pl.ds(j, l2, stride=l1), :]
        bi = a_ref[1, pl.ds(j, l2, stride=l1), :]
        p = _dot3(stage2, jnp.concatenate([br, bi], axis=0))
        re = _dot3(jnp.concatenate([p[:l2], p[l2:]], axis=1), chan)
        o_ref[pl.ds(row0 + j, l2, stride=l1), :] = _bdot(re.astype(BF16), wf)
        return carry

    lax.fori_loop(0, l1, second, 0, unroll=FNET_UNROLL)


def _fnet_kernel(f_ref, s1l, tcl, tsl, s2l, chl, s1c, tcc, tsc, s2c, chc, wf_ref, o_ref, a_ref, *, n_lat, n_ctx):
    _fnet_part(f_ref, o_ref, a_ref, 0, n_lat, s1l, tcl, tsl, s2l, chl, wf_ref)
    _fnet_part(f_ref, o_ref, a_ref, n_lat, n_ctx, s1c, tcc, tsc, s2c, chc, wf_ref)


def _fnet_call(st, f, w_fnet):
    gps = FNET_GROUPS_PER_SLAB
    eye = jnp.eye(gps, dtype=F32)
    wg = w_fnet.reshape(FNET_SLABS, gps, FNET_GROUP_DIM, FNET_GROUP_DIM)
    wf = (eye[None, :, None, :, None] * wg[:, :, :, None, :]).reshape(FNET_SLABS, V7X_LANES, V7X_LANES).astype(BF16)
    consts = (*_fft_tables(st.n_lat), _fnet_channel_tables(st.n_lat),
              *_fft_tables(st.n_ctx), _fnet_channel_tables(st.n_ctx))
    f3 = f.reshape(st.bsz, st.nt, FNET_WIDTH)
    blk = pl.BlockSpec((None, st.nt, V7X_LANES), lambda b, s: (b, 0, s))
    o = pl.pallas_call(
        functools.partial(_fnet_kernel, n_lat=st.n_lat, n_ctx=st.n_ctx),
        grid=(st.bsz, FNET_SLABS),
        in_specs=[blk] + [_full_spec(c.shape) for c in consts]
        + [pl.BlockSpec((None, V7X_LANES, V7X_LANES), lambda b, s: (s, 0, 0))],
        out_specs=blk,
        out_shape=jax.ShapeDtypeStruct(f3.shape, F32),
        scratch_shapes=[pltpu.VMEM((2, st.n_lat, V7X_LANES), F32)],
        compiler_params=_params("arbitrary", "arbitrary"),
        name="fnet",
    )(f3, *consts, wf)
    return o.reshape(st.t, FNET_WIDTH)


SEG_ALIGN = V7X_SUBLANES
PAIRS_PER_TILE = TOKEN_TILE * TOP_K
LOCAL_ROWS = PAIRS_PER_TILE + N_EXPERTS * SEG_ALIGN
ROUTE_LANES = V7X_LANES


def _route(h2, wr, e_bias):
    tm = h2.shape[0]
    scores = jax.nn.sigmoid(_dot3(h2, wr))
    sel = scores + e_bias
    lane = lax.broadcasted_iota(jnp.int32, sel.shape, 1).astype(F32)
    hits = []
    for _ in range(TOP_K):
        best = sel.max(axis=-1, keepdims=True)
        first = jnp.where(sel == best, lane, float(N_EXPERTS)).min(axis=-1, keepdims=True)
        hit = lane == first
        hits.append(hit)
        sel = jnp.where(hit, -jnp.inf, sel)
    chosen = hits[0]
    for hit in hits[1:]:
        chosen = jnp.logical_or(chosen, hit)
    chosen_f = jnp.where(chosen, 1.0, 0.0)
    counts = chosen_f.sum(axis=0, keepdims=True)
    seg_len = jnp.ceil(counts * (1.0 / SEG_ALIGN)) * SEG_ALIGN
    er = lax.broadcasted_iota(jnp.int32, (N_EXPERTS, N_EXPERTS), 0)
    ec = lax.broadcasted_iota(jnp.int32, (N_EXPERTS, N_EXPERTS), 1)
    lower_experts = jnp.where(er < ec, 1.0, 0.0).astype(BF16)
    seg_start = _bdot(jnp.broadcast_to(seg_len, (V7X_SUBLANES, N_EXPERTS)).astype(BF16), lower_experts)[0:1]
    r = lax.broadcasted_iota(jnp.int32, (tm, tm), 0)
    c = lax.broadcasted_iota(jnp.int32, (tm, tm), 1)
    earlier = jnp.where(c < r, 1.0, 0.0).astype(BF16)
    row_all = _bdot(earlier, chosen_f.astype(BF16)) + seg_start
    rec_lane = lax.broadcasted_iota(jnp.int32, (tm, ROUTE_LANES), 1)
    rec = jnp.zeros((tm, ROUTE_LANES), F32)
    raw = []
    for k, hit in enumerate(hits):
        rec = jnp.where(rec_lane == k, jnp.where(hit, row_all, 0.0).sum(axis=-1, keepdims=True), rec)
        raw.append(jnp.where(hit, scores, 0.0).sum(axis=-1, keepdims=True))
    total = raw[0]
    for w in raw[1:]:
        total = total + w
    for k, w in enumerate(raw):
        rec = jnp.where(rec_lane == TOP_K + k, w / total * ROUTE_SCALE, rec)
    return rec, counts.astype(jnp.int32)


def _out_ln_kernel(*refs, n_in, alpha):
    a_refs = refs[:n_in]
    w_refs = refs[n_in:2 * n_in]
    x_ref, mod_ref, g_ref, b_ref, wr_ref, eb_ref, x1_ref, h2_ref, rec_ref, cnt_ref = refs[2 * n_in:]
    y = None
    for a_ref, w_ref in zip(a_refs, w_refs):
        part = _bdot(a_ref[...].astype(BF16), w_ref[...])
        y = part if y is None else y + part
    z = alpha * x_ref[...] + mod_ref[2:3, :] * y
    x1 = _layer_norm(z, g_ref[...], b_ref[...])
    x1_ref[...] = x1
    h2 = x1 * (1.0 + mod_ref[4:5, :]) + mod_ref[3:4, :]
    h2_ref[...] = h2.astype(BF16)
    rec, counts = _route(h2, wr_ref[...], eb_ref[...])
    rec_ref[...] = rec
    cnt_ref[...] = counts


def _out_ln_call(st, acts, w_out, x, mod, layer, ln_g, ln_b, w_router, e_bias, alpha):
    ws, r0 = [], 0
    for a in acts:
        ws.append(w_out[r0:r0 + a.shape[1]].astype(BF16))
        r0 += a.shape[1]
    assert r0 == w_out.shape[0]
    row = lambda v: v.reshape(1, -1)
    return pl.pallas_call(
        functools.partial(_out_ln_kernel, n_in=len(acts), alpha=alpha),
        grid=(st.n_tiles,),
        in_specs=[st.tok_spec(a.shape[1]) for a in acts] + [_full_spec(w.shape) for w in ws]
        + [st.tok_spec(st.d), st.mod_spec(layer), _full_spec((1, st.d)), _full_spec((1, st.d)),
           _full_spec(w_router.shape), _full_spec((1, N_EXPERTS))],
        out_specs=[st.tok_spec(st.d), st.tok_spec(st.d), st.tok_spec(ROUTE_LANES),
                   pl.BlockSpec((None, 1, N_EXPERTS), lambda t: (t, 0, 0))],
        out_shape=[jax.ShapeDtypeStruct((st.t, st.d), F32), jax.ShapeDtypeStruct((st.t, st.d), BF16),
                   jax.ShapeDtypeStruct((st.t, ROUTE_LANES), F32),
                   jax.ShapeDtypeStruct((st.n_tiles, 1, N_EXPERTS), jnp.int32)],
        compiler_params=_params("arbitrary"),
        name="out_ln",
    )(*acts, *ws, x, mod, row(ln_g), row(ln_b), w_router, row(e_bias))


MOE_BLOCK = 512
ROW_CHUNK = 256
ONEHOT_STRIP = 32
GROUP_UNROLL = 4
ROW_GROUPS = LOCAL_ROWS // SEG_ALIGN
PLAN_WIDTH = (ROW_GROUPS // V7X_LANES + 1) * V7X_LANES
WAIT_SIZES = tuple(1 << b for b in range((LOCAL_ROWS - 1).bit_length() - 1, SEG_ALIGN.bit_length() - 2, -1))


def _moe_blocks(st):
    return (st.t * TOP_K + st.n_tiles * N_EXPERTS * (SEG_ALIGN - 1)) // MOE_BLOCK + N_EXPERTS


def _slot_plan(tile_counts, n_blocks):
    n_tiles = tile_counts.shape[0]
    cnt = tile_counts.reshape(n_tiles, N_EXPERTS)
    seg = (cnt + SEG_ALIGN - 1) // SEG_ALIGN * SEG_ALIGN
    local_start = jnp.cumsum(seg, axis=1) - seg
    tiles_before = jnp.cumsum(seg, axis=0) - seg
    total = seg.sum(axis=0)
    padded = (total + MOE_BLOCK - 1) // MOE_BLOCK * MOE_BLOCK
    pad_end = jnp.cumsum(padded)
    pad_start = pad_end - padded
    sorted_start = pad_start[None, :] + tiles_before
    group_row = jnp.arange(ROW_GROUPS, dtype=jnp.int32) * SEG_ALIGN
    owner = ((local_start[:, None, :] <= group_row[None, :, None])
             & (group_row[None, :, None] < (local_start + seg)[:, None, :]))
    group_dst = jnp.where(owner, (sorted_start - local_start)[:, None, :], 0).sum(-1) + group_row[None, :]
    tile_rows = jnp.broadcast_to(seg.sum(axis=1, keepdims=True), (n_tiles, PLAN_WIDTH - ROW_GROUPS))
    plan = jnp.concatenate([group_dst, tile_rows], axis=1).astype(jnp.int32).reshape(n_tiles, 1, PLAN_WIDTH)
    n_used = jnp.maximum(pad_end[-1] // MOE_BLOCK, 1)
    first_row = jnp.arange(n_blocks, dtype=jnp.int32) * MOE_BLOCK
    ends_expert = ((first_row[:, None] + MOE_BLOCK == pad_end[None, :]) & (padded[None, :] > 0)).any(-1)
    fill = (ends_expert | (first_row >= pad_end[-1])).astype(jnp.int32)
    first_block = (pad_start // MOE_BLOCK).astype(jnp.int32)
    block_count = (padded // MOE_BLOCK).astype(jnp.int32)
    return plan, first_block, block_count, n_used.astype(jnp.int32).reshape(1), fill


def _for_each_group(plan_ref, fn):
    def one(j):
        fn(pl.multiple_of(j * SEG_ALIGN, SEG_ALIGN), pl.multiple_of(plan_ref[0, j], SEG_ALIGN))

    def several(q, carry):
        for u in range(GROUP_UNROLL):
            one(q * GROUP_UNROLL + u)
        return carry

    def single(j, carry):
        one(j)
        return carry

    n = plan_ref[0, ROW_GROUPS] // SEG_ALIGN
    lax.fori_loop(0, n // GROUP_UNROLL, several, 0)
    lax.fori_loop(n // GROUP_UNROLL * GROUP_UNROLL, n, single, 0)


def _wait_tile_rows(plan_ref, make_copy):
    rows = plan_ref[0, ROW_GROUPS]
    for size in WAIT_SIZES:
        @pl.when((rows & size) != 0)
        def _(size=size):
            make_copy(size).wait()


def _pack_halves(x):
    bits = pltpu.bitcast(x, jnp.uint32)
    half = x.shape[1] // 2
    return (bits[:, :half] >> 16) | (bits[:, half:] & jnp.uint32(0xFFFF0000))


def _unpack_halves(w):
    lo = pltpu.bitcast(w << 16, F32).astype(BF16)
    hi = pltpu.bitcast(w & jnp.uint32(0xFFFF0000), F32).astype(BF16)
    return lo, hi


def _dispatch_kernel(fill_ref, plan_ref, plan1_ref, plan2_ref, rec_ref, h_ref, xs_ref, loc_ref, zero_ref, onehot_ref,
                     zsem, sem,
                     *, n_blocks, n_tiles):
    i = pl.program_id(0)
    slot = i % 2

    @pl.when(i == 0)
    def _():
        zero_ref[...] = jnp.zeros_like(zero_ref)

        def fill(b):
            return pltpu.make_async_copy(zero_ref, xs_ref.at[pl.ds(pl.multiple_of(b * MOE_BLOCK, MOE_BLOCK), MOE_BLOCK)],
                                         zsem)

        def start(b, c):
            @pl.when(fill_ref[b] > 0)
            def _():
                fill(b).start()
            return c

        def wait(b, c):
            @pl.when(fill_ref[b] > 0)
            def _():
                fill(b).wait()
            return c

        lax.fori_loop(0, n_blocks, start, 0)
        lax.fori_loop(0, n_blocks, wait, 0)

    def piece(s):
        def copy(local, dst, size):
            return pltpu.make_async_copy(loc_ref.at[s, pl.ds(local, size)], xs_ref.at[pl.ds(dst, size)], sem.at[s])
        return copy

    def rows_done(s):
        return lambda size: piece(s)(0, 0, size)

    @pl.when(i >= 2)
    def _():
        _wait_tile_rows(plan2_ref, rows_done(slot))

    rows_of = rec_ref[...].T
    x = h_ref[...]
    for c0 in range(0, LOCAL_ROWS, ROW_CHUNK):
        for r0 in range(c0, c0 + ROW_CHUNK, ONEHOT_STRIP):
            local_row = (lax.broadcasted_iota(jnp.int32, (ONEHOT_STRIP, TOKEN_TILE), 0) + r0).astype(F32)
            onehot = jnp.zeros((ONEHOT_STRIP, TOKEN_TILE), F32)
            for k in range(TOP_K):
                onehot = jnp.where(local_row == rows_of[k:k + 1, :], 1.0, onehot)
            onehot_ref[r0 - c0:r0 - c0 + ONEHOT_STRIP, :] = onehot.astype(BF16)
        loc_ref[slot, c0:c0 + ROW_CHUNK, :] = _pack_halves(_bdot(onehot_ref[...], x))

    _for_each_group(plan_ref, lambda local, dst: piece(slot)(local, dst, SEG_ALIGN).start())

    @pl.when(i == n_tiles - 1)
    def _():
        if n_tiles >= 2:
            _wait_tile_rows(plan1_ref, rows_done(1 - slot))
        _wait_tile_rows(plan_ref, rows_done(slot))


def _dispatch_call(st, h2, rec, plan, fill, n_blocks):
    half = st.d // 2
    plan_spec = lambda back: pl.BlockSpec((None, 1, PLAN_WIDTH), lambda t, *_: (jnp.maximum(t - back, 0), 0, 0),
                                          memory_space=pltpu.SMEM)
    grid_spec = pltpu.PrefetchScalarGridSpec(
        num_scalar_prefetch=1,
        grid=(st.n_tiles,),
        in_specs=[plan_spec(0), plan_spec(1), plan_spec(2),
                  pl.BlockSpec((TOKEN_TILE, ROUTE_LANES), lambda t, *_: (t, 0)),
                  pl.BlockSpec((TOKEN_TILE, st.d), lambda t, *_: (t, 0))],
        out_specs=pl.BlockSpec(memory_space=pl.ANY),
        scratch_shapes=[pltpu.VMEM((2, LOCAL_ROWS, half), jnp.uint32), pltpu.VMEM((MOE_BLOCK, half), jnp.uint32),
                        pltpu.VMEM((ROW_CHUNK, TOKEN_TILE), BF16),
                        pltpu.SemaphoreType.DMA, pltpu.SemaphoreType.DMA((2,))],
    )
    return pl.pallas_call(
        functools.partial(_dispatch_kernel, n_blocks=n_blocks, n_tiles=st.n_tiles),
        grid_spec=grid_spec,
        out_shape=jax.ShapeDtypeStruct((n_blocks * MOE_BLOCK, half), jnp.uint32),
        compiler_params=_params("arbitrary"),
        name="moe_dispatch",
    )(fill, plan, plan, plan, rec, h2)


EXPERT_BUFFERS = 3


def _experts_kernel(first_ref, count_ref, nu_ref, xs_ref, wg_ref, wu_ref, wd_ref, ys_ref, xbuf, ybuf, wg_s, wu_s, wd_s,
                    xsem, ysem, *, n_blocks):
    e = pl.program_id(0)
    n_used = nu_ref[0]

    def rows(b):
        return pl.ds(pl.multiple_of(b * MOE_BLOCK, MOE_BLOCK), MOE_BLOCK)

    def fetch(b):
        s = b % EXPERT_BUFFERS
        return pltpu.make_async_copy(xs_ref.at[rows(b)], xbuf.at[s], xsem.at[s])

    def store(b):
        s = b % EXPERT_BUFFERS
        return pltpu.make_async_copy(ybuf.at[s], ys_ref.at[rows(b)], ysem.at[s])

    @pl.when(e == 0)
    def _():
        for b in range(EXPERT_BUFFERS - 1):
            @pl.when(b < n_used)
            def _(b=b):
                fetch(b).start()

    @pl.when(count_ref[e] > 0)
    def _():
        wg_s[...] = wg_ref[...].astype(BF16)
        wu_s[...] = wu_ref[...].astype(BF16)
        wd_s[...] = wd_ref[...].astype(BF16)

    def block(b, carry):
        @pl.when(b + EXPERT_BUFFERS - 1 < n_used)
        def _():
            fetch(b + EXPERT_BUFFERS - 1).start()

        fetch(b).wait()

        @pl.when(b >= EXPERT_BUFFERS)
        def _():
            store(b - EXPERT_BUFFERS).wait()

        slot = b % EXPERT_BUFFERS
        half = wg_s.shape[0] // 2
        lo, hi = _unpack_halves(xbuf[slot])
        gate = _bdot(lo, wg_s[:half, :]) + _bdot(hi, wg_s[half:, :])
        up = _bdot(lo, wu_s[:half, :]) + _bdot(hi, wu_s[half:, :])
        y = _bdot((_silu(gate) * up).astype(BF16), wd_s[...])
        ybuf[slot] = _pack_halves(y.astype(BF16).astype(F32))
        store(b).start()
        return carry

    lax.fori_loop(first_ref[e], first_ref[e] + count_ref[e], block, 0)

    @pl.when(e == N_EXPERTS - 1)
    def _():
        def drain(b, carry):
            store(b).wait()
            return carry

        lax.fori_loop(jnp.maximum(n_used - EXPERT_BUFFERS, 0), n_used, drain, 0)
        ybuf[0] = jnp.zeros(ybuf.shape[1:], ybuf.dtype)

        def zero_tail(b):
            return pltpu.make_async_copy(ybuf.at[0], ys_ref.at[rows(b)], ysem.at[0])

        def start(b, carry):
            zero_tail(b).start()
            return carry

        def wait(b, carry):
            zero_tail(b).wait()
            return carry

        lax.fori_loop(n_used, n_blocks, start, 0)
        lax.fori_loop(n_used, n_blocks, wait, 0)


def _experts_call(st, xs, first_block, block_count, n_used, w_gate, w_up, w_down, layer, n_blocks):
    half = st.d // 2
    w_in_spec = pl.BlockSpec((None, None, st.d, EXPERT_DIM), lambda e, *_: (layer, e, 0, 0))
    grid_spec = pltpu.PrefetchScalarGridSpec(
        num_scalar_prefetch=3,
        grid=(N_EXPERTS,),
        in_specs=[pl.BlockSpec(memory_space=pl.ANY),
                  w_in_spec, w_in_spec,
                  pl.BlockSpec((None, None, EXPERT_DIM, st.d), lambda e, *_: (layer, e, 0, 0))],
        out_specs=pl.BlockSpec(memory_space=pl.ANY),
        scratch_shapes=[pltpu.VMEM((EXPERT_BUFFERS, MOE_BLOCK, half), jnp.uint32),
                        pltpu.VMEM((EXPERT_BUFFERS, MOE_BLOCK, half), jnp.uint32),
                        pltpu.VMEM((st.d, EXPERT_DIM), BF16), pltpu.VMEM((st.d, EXPERT_DIM), BF16),
                        pltpu.VMEM((EXPERT_DIM, st.d), BF16),
                        pltpu.SemaphoreType.DMA((EXPERT_BUFFERS,)), pltpu.SemaphoreType.DMA((EXPERT_BUFFERS,))],
    )
    return pl.pallas_call(
        functools.partial(_experts_kernel, n_blocks=n_blocks),
        grid_spec=grid_spec,
        out_shape=jax.ShapeDtypeStruct(xs.shape, jnp.uint32),
        compiler_params=_params("arbitrary"),
        name="moe_experts",
    )(first_block, block_count, n_used, xs, w_gate, w_up, w_down)


def _ffn_ln_kernel(plan_ref, plan_next_ref, rec_ref, h2_ref, x1_ref, sg_ref, su_ref, sd_ref, mod_ref, modn_ref, g_ref,
                   b_ref, ys_ref, x2_ref, hn_ref, loc_ref, weight_ref, sem, *, alpha, n_tiles):
    i = pl.program_id(0)
    slot = i % 2

    def piece(s):
        def copy(local, src, size):
            return pltpu.make_async_copy(ys_ref.at[pl.ds(src, size)], loc_ref.at[s, pl.ds(local, size)], sem.at[s])
        return copy

    @pl.when(i == 0)
    def _():
        loc_ref[...] = jnp.zeros_like(loc_ref)
        _for_each_group(plan_ref, lambda local, src: piece(0)(local, src, SEG_ALIGN).start())

    @pl.when(i + 1 < n_tiles)
    def _():
        _for_each_group(plan_next_ref, lambda local, src: piece(1 - slot)(local, src, SEG_ALIGN).start())

    h2 = h2_ref[...]
    a = _silu(_bdot(h2, sg_ref[...])) * _bdot(h2, su_ref[...])
    shared = _bdot(a.astype(BF16), sd_ref[...])

    _wait_tile_rows(plan_ref, lambda size: piece(slot)(0, 0, size))
    rec = rec_ref[...]
    half = loc_ref.shape[2]
    routed_lo = jnp.zeros((TOKEN_TILE, half), F32)
    routed_hi = jnp.zeros((TOKEN_TILE, half), F32)
    for c0 in range(0, LOCAL_ROWS, ROW_CHUNK):
        for t0 in range(0, TOKEN_TILE, ONEHOT_STRIP):
            local_row = (lax.broadcasted_iota(jnp.int32, (ONEHOT_STRIP, ROW_CHUNK), 1) + c0).astype(F32)
            part = rec[t0:t0 + ONEHOT_STRIP, :]
            weight = jnp.zeros((ONEHOT_STRIP, ROW_CHUNK), F32)
            for k in range(TOP_K):
                weight = jnp.where(local_row == part[:, k:k + 1], part[:, TOP_K + k:TOP_K + k + 1], weight)
            weight_ref[t0:t0 + ONEHOT_STRIP, :] = weight.astype(BF16)
        lo, hi = _unpack_halves(loc_ref[slot, c0:c0 + ROW_CHUNK, :])
        weight = weight_ref[...]
        routed_lo = routed_lo + _bdot(weight, lo)
        routed_hi = routed_hi + _bdot(weight, hi)
    ff = jnp.concatenate([routed_lo, routed_hi], axis=1) + shared
    z = alpha * x1_ref[...] + mod_ref[5:6, :] * ff
    x2 = _layer_norm(z, g_ref[...], b_ref[...])
    x2_ref[...] = x2
    hn_ref[...] = (x2 * (1.0 + modn_ref[1:2, :]) + modn_ref[0:1, :]).astype(BF16)


def _ffn_ln_call(st, ys, plan, rec, h2, x1, s_gate, s_up, s_down, mod, layer, next_layer, ln_g, ln_b, alpha):
    row = lambda v: v.reshape(1, -1)
    sg, su, sd = s_gate.astype(BF16), s_up.astype(BF16), s_down.astype(BF16)
    last = st.n_tiles - 1
    return pl.pallas_call(
        functools.partial(_ffn_ln_kernel, alpha=alpha, n_tiles=st.n_tiles),
        grid=(st.n_tiles,),
        in_specs=[pl.BlockSpec((None, 1, PLAN_WIDTH), lambda t: (t, 0, 0), memory_space=pltpu.SMEM),
                  pl.BlockSpec((None, 1, PLAN_WIDTH), lambda t: (jnp.minimum(t + 1, last), 0, 0),
                               memory_space=pltpu.SMEM),
                  st.tok_spec(ROUTE_LANES), st.tok_spec(st.d), st.tok_spec(st.d),
                  _full_spec(sg.shape), _full_spec(su.shape), _full_spec(sd.shape),
                  st.mod_spec(layer), st.mod_spec(next_layer), _full_spec((1, st.d)), _full_spec((1, st.d)),
                  pl.BlockSpec(memory_space=pl.ANY)],
        out_specs=[st.tok_spec(st.d), st.tok_spec(st.d)],
        out_shape=[jax.ShapeDtypeStruct((st.t, st.d), F32), jax.ShapeDtypeStruct((st.t, st.d), BF16)],
        scratch_shapes=[pltpu.VMEM((2, LOCAL_ROWS, st.d // 2), jnp.uint32),
                        pltpu.VMEM((TOKEN_TILE, ROW_CHUNK), BF16), pltpu.SemaphoreType.DMA((2,))],
        compiler_params=_params("arbitrary"),
        name="ffn_ln",
    )(plan, plan, rec, h2, x1, sg, su, sd, mod, mod, row(ln_g), row(ln_b), ys)


def _na_in_kernel(h_ref, w_ref, o_ref):
    h = h_ref[...]
    o_ref[:, :NA_WIDTH] = (_bdot(h, w_ref[:, :NA_WIDTH]) * SCORE_SCALE).astype(BF16)
    o_ref[:, NA_WIDTH:] = _bdot(h, w_ref[:, NA_WIDTH:]).astype(BF16)


def _na_in_call(st, h, w_in):
    return pl.pallas_call(
        _na_in_kernel,
        grid=(st.n_tiles,),
        in_specs=[st.tok_spec(st.d), _full_spec(w_in.shape)],
        out_specs=st.tok_spec(3 * NA_WIDTH),
        out_shape=jax.ShapeDtypeStruct((st.t, 3 * NA_WIDTH), BF16),
        compiler_params=_params("arbitrary"),
        name="na_in",
    )(h, w_in)


def _na_geometry(n_lat):
    rows = n_lat // GRID_W
    kh, kw, qr = min(NA_KH, rows), min(NA_KW, GRID_W), NA_Q_ROWS
    nbr = min(qr + kh - 1, rows)
    col = np.arange(GRID_W)
    col_start = np.clip(col - kw // 2, 0, GRID_W - kw)
    in_col = (col[None, :] >= col_start[:, None]) & (col[None, :] < col_start[:, None] + kw)
    dc = np.clip(col[None, :] - col[:, None] + NA_KW - 1, 0, 2 * NA_KW - 2)
    starts, variant_of, variants = [], [], {}
    for i in range(rows // qr):
        qrow = i * qr + np.arange(qr)
        rstart = np.clip(qrow - kh // 2, 0, rows - kh)
        bs = min(int(rstart[0]), rows - nbr)
        krow = bs + np.arange(nbr)
        in_row = (krow[None, :] >= rstart[:, None]) & (krow[None, :] < rstart[:, None] + kh)
        dr = np.clip(krow[None, :] - qrow[:, None] + NA_KH - 1, 0, 2 * NA_KH - 2)
        key = (in_row.tobytes(), dr.tobytes())
        if key not in variants:
            mask = (in_row[:, None, :, None] & in_col[None, :, None, :]).reshape(qr * GRID_W, nbr * GRID_W)
            variants[key] = (len(variants), dr, mask)
        starts.append(bs)
        variant_of.append(variants[key][0])
    ordered = sorted(variants.values(), key=lambda v: v[0])
    return nbr, np.asarray(starts, np.int32), np.asarray(variant_of, np.int32), [(v[1], v[2]) for v in ordered], dc


def _na_bias_kernel(tab_ref, place_ref, mask_ref, o_ref):
    tab = tab_ref[...]
    hi = tab.astype(BF16)
    rest = tab - hi.astype(F32)
    mid = rest.astype(BF16)
    lo = (rest - mid.astype(F32)).astype(BF16)
    for a in range(NA_Q_ROWS):
        place = place_ref[a]
        rows = _bdot(hi, place) + (_bdot(mid, place) + _bdot(lo, place))
        sl = slice(a * GRID_W, (a + 1) * GRID_W)
        o_ref[sl, :] = jnp.where(mask_ref[sl, :] > 0.0, rows * LOG2_E, NEG_INF)


def _na_bias_tables(rpb, n_lat, n_ctx):
    nbr, starts, variant_of, variants, dc = _na_geometry(n_lat)
    n_dr, n_dc = 2 * NA_KH - 1, 2 * NA_KW - 1
    band = nbr * GRID_W
    n_keys = band + n_ctx
    col_sel = jnp.asarray((dc[:, :, None] == np.arange(n_dc)).astype(np.float32))
    tab = jnp.einsum("hrc,wuc->hwru", rpb, col_sel, precision=lax.Precision.HIGHEST)
    tab = tab.reshape(NA_HEADS, GRID_W, n_dr * GRID_W)
    place = np.zeros((len(variants), NA_Q_ROWS, n_dr, GRID_W, n_keys), np.float32)
    masks = np.ones((len(variants), NA_Q_ROWS * GRID_W, n_keys), np.float32)
    u = np.arange(GRID_W)
    for v, (dr, mask) in enumerate(variants):
        masks[v, :, :band] = mask
        for a in range(NA_Q_ROWS):
            for j in range(nbr):
                place[v, a, dr[a, j], u, j * GRID_W + u] = 1.0
    place = jnp.asarray(place.reshape(len(variants), NA_Q_ROWS, n_dr * GRID_W, n_keys), BF16)
    bias = pl.pallas_call(
        _na_bias_kernel,
        grid=(len(variants), NA_HEADS),
        in_specs=[pl.BlockSpec((None, GRID_W, n_dr * GRID_W), lambda v, h: (h, 0, 0)),
                  pl.BlockSpec((None, NA_Q_ROWS, n_dr * GRID_W, n_keys), lambda v, h: (v, 0, 0, 0)),
                  pl.BlockSpec((None, NA_Q_ROWS * GRID_W, n_keys), lambda v, h: (v, 0, 0))],
        out_specs=pl.BlockSpec((None, None, NA_Q_ROWS * GRID_W, n_keys), lambda v, h: (v, h, 0, 0)),
        out_shape=jax.ShapeDtypeStruct((len(variants), NA_HEADS, NA_Q_ROWS * GRID_W, n_keys), F32),
        compiler_params=_params("arbitrary", "arbitrary"),
        name="na_bias",
    )(tab, place, jnp.asarray(masks))
    return nbr, starts, variant_of, bias


NA_Q_TILE = NA_Q_ROWS * GRID_W


def _na_kernel(start_ref, var_ref, q_ref, k_ref, v_ref, bias_ref, o_ref, *, n_lat, band):
    i = pl.program_id(1)
    is_lat = i < n_lat // NA_Q_TILE

    @pl.when(is_lat)
    def _():
        off = pl.multiple_of(start_ref[i] * GRID_W, GRID_W)
        for h in range(NA_HEADS):
            sl = slice(h * HEAD_DIM, (h + 1) * HEAD_DIM)
            keys = jnp.concatenate([k_ref[pl.ds(off, band), sl], k_ref[n_lat:, sl]], axis=0)
            values = jnp.concatenate([v_ref[pl.ds(off, band), sl], v_ref[n_lat:, sl]], axis=0)
            o_ref[:, sl] = _attend(q_ref[:, sl], [(keys, values)], [bias_ref[h]]).astype(BF16)

    @pl.when(jnp.logical_not(is_lat))
    def _():
        for h in range(NA_HEADS):
            sl = slice(h * HEAD_DIM, (h + 1) * HEAD_DIM)
            o_ref[:, sl] = _attend(q_ref[:, sl], [(k_ref[n_lat:, sl], v_ref[n_lat:, sl])], [None]).astype(BF16)


def _na_call(st, qkv, rpb):
    nbr, starts, variant_of, bias = _na_bias_tables(rpb, st.n_lat, st.n_ctx)
    band = nbr * GRID_W
    n_q = st.nt // NA_Q_TILE
    pad = n_q - starts.shape[0]
    starts = jnp.asarray(np.concatenate([starts, np.zeros(pad, np.int32)]))
    variant_of = jnp.asarray(np.concatenate([variant_of, np.zeros(pad, np.int32)]))
    qkv3 = qkv.reshape(st.bsz, st.nt, 3 * NA_WIDTH)
    grid_spec = pltpu.PrefetchScalarGridSpec(
        num_scalar_prefetch=2,
        grid=(st.bsz, n_q),
        in_specs=[pl.BlockSpec((None, NA_Q_TILE, NA_WIDTH), lambda b, i, s, v: (b, i, 0)),
                  pl.BlockSpec((None, st.nt, NA_WIDTH), lambda b, i, s, v: (b, 0, 1)),
                  pl.BlockSpec((None, st.nt, NA_WIDTH), lambda b, i, s, v: (b, 0, 2)),
                  pl.BlockSpec((None, NA_HEADS, NA_Q_TILE, band + st.n_ctx), lambda b, i, s, v: (v[i], 0, 0, 0))],
        out_specs=pl.BlockSpec((None, NA_Q_TILE, NA_WIDTH), lambda b, i, s, v: (b, i, 0)),
    )
    o = pl.pallas_call(
        functools.partial(_na_kernel, n_lat=st.n_lat, band=band),
        grid_spec=grid_spec,
        out_shape=jax.ShapeDtypeStruct((st.bsz, st.nt, NA_WIDTH), BF16),
        compiler_params=_params("arbitrary", "arbitrary"),
        name="na_attn",
    )(starts, variant_of, qkv3, qkv3, qkv3, bias)
    return o.reshape(st.t, NA_WIDTH)


def kernel(x, c, ctx, c_ctx, w_mod, b_mod, ln1_g, ln1_b, ln2_g, ln2_b, ab_w_in, ab_w_fnet, ab_q_norm, ab_k_norm,
           ab_w_out, na_w_in, na_rpb, na_w_out, moe_w_router, moe_bias, moe_w_gate, moe_w_up, moe_w_down,
           sh_w_gate, sh_w_up, sh_w_down):
    bsz, n_lat, d = x.shape
    n_ctx = ctx.shape[1]
    depth = w_mod.shape[0]
    st = _Stream(bsz, n_lat, n_ctx, d)
    alpha = (2 * depth) ** 0.25

    cc = jnp.concatenate([c, c_ctx[None, :], jnp.zeros((MOD_ROWS - bsz - 1, d), F32)], axis=0)
    mod = _mod_call(cc, w_mod, b_mod).reshape(depth, MOD_ROWS, 6, d)
    xs = jnp.concatenate([x, ctx], axis=1).reshape(st.t, d)
    h = _modulate_call(st, xs, mod, 0)
    rope = _rope_tables(n_lat, n_ctx)
    n_blocks = _moe_blocks(st)

    for l in range(depth):
        j = l // 2
        if l % 2 == 0:
            f, q, k, v = _ab_in_call(st, h, ab_w_in[j].astype(BF16), ab_q_norm[j], ab_k_norm[j], rope)
            acts = [_fnet_call(st, f, ab_w_fnet[j]), _gqa_call(st, q, k, v)]
            w_out = ab_w_out[j]
        else:
            qkv = _na_in_call(st, h, na_w_in[j].astype(BF16))
            acts = [_na_call(st, qkv, na_rpb[j])]
            w_out = na_w_out[j]
        x1, h2, rec, tile_counts = _out_ln_call(st, acts, w_out, xs, mod, l, ln1_g[l], ln1_b[l],
                                                moe_w_router[l], moe_bias[l], alpha)
        plan, first_block, block_count, n_used, fill = _slot_plan(tile_counts, n_blocks)
        rows = _dispatch_call(st, h2, rec, plan, fill, n_blocks)
        ys = _experts_call(st, rows, first_block, block_count, n_used, moe_w_gate, moe_w_up, moe_w_down, l, n_blocks)
        xs, h = _ffn_ln_call(st, ys, plan, rec, h2, x1, sh_w_gate[l], sh_w_up[l], sh_w_down[l], mod, l,
                             min(l + 1, depth - 1), ln2_g[l], ln2_b[l], alpha)
    return xs.reshape(bsz, st.nt, d)[:, :n_lat]
```

```python
import functools
import math

import numpy as np
import jax
import jax.numpy as jnp
from jax import lax
from jax.experimental import pallas as pl
from jax.experimental.pallas import tpu as pltpu

GRID_W = 64
HEAD_DIM = 128
FNET_GROUPS = 4
FNET_GROUP_DIM = 64
FNET_WIDTH = FNET_GROUPS * FNET_GROUP_DIM
GQA_Q_HEADS = 6
GQA_KV_HEADS = 2
GQA_GROUP = GQA_Q_HEADS // GQA_KV_HEADS
ROPE_THETA = 10000.0
NA_HEADS = 8
NA_WIDTH = NA_HEADS * HEAD_DIM
NA_KH = 8
NA_KW = 16
NA_Q_ROWS = 2
NEG_INF = -1e30
N_EXPERTS = 64
TOP_K = 8
EXPERT_DIM = 256
ROUTE_SCALE = 2.5
LN_EPS = 1e-6
RMS_EPS = 1e-6
ATTN_SCALE = HEAD_DIM ** -0.5
LOG2_E = math.log2(math.e)
SCORE_SCALE = ATTN_SCALE * LOG2_E

V7X_LANES = 128
V7X_SUBLANES = 8
V7X_VMEM_LIMIT_BYTES = 56 * 1024 * 1024

TOKEN_TILE = 256
MOD_ROWS = 8

F32 = jnp.float32
BF16 = jnp.bfloat16


def _params(*sem):
    return pltpu.CompilerParams(dimension_semantics=sem, vmem_limit_bytes=V7X_VMEM_LIMIT_BYTES)


def _bdot(a, b):
    return jnp.dot(a, b, preferred_element_type=F32)


def _bdot_t(a, b):
    return lax.dot_general(a, b, (((1,), (1,)), ((), ())), preferred_element_type=F32)


def _bdot_tn(a, b):
    return lax.dot_general(a, b, (((0,), (0,)), ((), ())), preferred_element_type=F32)


def _split(x):
    hi = x.astype(BF16)
    lo = (x - hi.astype(F32)).astype(BF16)
    return hi, lo


def _dot3(a, b):
    ah, al = _split(a)
    bh, bl = _split(b)
    return _bdot(ah, bh) + (_bdot(ah, bl) + _bdot(al, bh))


def _silu(x):
    return x * jax.nn.sigmoid(x)


def _layer_norm(z, g, b):
    mu = jnp.mean(z, axis=-1, keepdims=True)
    zc = z - mu
    var = jnp.mean(zc * zc, axis=-1, keepdims=True)
    return zc * lax.rsqrt(var + LN_EPS) * g + b


class _Stream:
    def __init__(self, bsz, n_lat, n_ctx, d):
        assert n_lat % TOKEN_TILE == 0 and n_ctx % TOKEN_TILE == 0
        assert bsz < MOD_ROWS
        self.bsz, self.n_lat, self.n_ctx, self.d = bsz, n_lat, n_ctx, d
        self.nt = n_lat + n_ctx
        self.t = bsz * self.nt
        self.tiles_per_sample = self.nt // TOKEN_TILE
        self.lat_tiles = n_lat // TOKEN_TILE
        self.n_tiles = self.t // TOKEN_TILE

    def mod_row(self, tile):
        return jnp.where(tile % self.tiles_per_sample < self.lat_tiles, tile // self.tiles_per_sample, self.bsz)

    def mod_spec(self, layer):
        return pl.BlockSpec((None, None, 6, self.d), lambda t: (layer, self.mod_row(t), 0, 0))

    def tok_spec(self, width):
        return pl.BlockSpec((TOKEN_TILE, width), lambda t: (t, 0))


def _full_spec(shape):
    nd = len(shape)
    return pl.BlockSpec(shape, lambda *_: (0,) * nd)


def _mod_kernel(cc_ref, w_ref, b_ref, o_ref):
    o_ref[...] = _dot3(_silu(cc_ref[...]), w_ref[...]) + b_ref[...]


def _mod_call(cc, w_mod, b_mod):
    depth, d, n = w_mod.shape
    tn = n // 4
    return pl.pallas_call(
        _mod_kernel,
        grid=(depth, n // tn),
        in_specs=[pl.BlockSpec((MOD_ROWS, d), lambda l, j: (0, 0)),
                  pl.BlockSpec((None, d, tn), lambda l, j: (l, 0, j)),
                  pl.BlockSpec((None, 1, tn), lambda l, j: (l, 0, j))],
        out_specs=pl.BlockSpec((None, MOD_ROWS, tn), lambda l, j: (l, 0, j)),
        out_shape=jax.ShapeDtypeStruct((depth, MOD_ROWS, n), F32),
        compiler_params=_params("arbitrary", "arbitrary"),
        name="mod",
    )(cc, w_mod, b_mod.reshape(depth, 1, n))


def _modulate_kernel(x_ref, mod_ref, h_ref):
    h_ref[...] = (x_ref[...] * (1.0 + mod_ref[1:2, :]) + mod_ref[0:1, :]).astype(BF16)


def _modulate_call(st, x, mod, layer):
    return pl.pallas_call(
        _modulate_kernel,
        grid=(st.n_tiles,),
        in_specs=[st.tok_spec(st.d), st.mod_spec(layer)],
        out_specs=st.tok_spec(st.d),
        out_shape=jax.ShapeDtypeStruct((st.t, st.d), BF16),
        compiler_params=_params("arbitrary"),
        name="modulate",
    )(x, mod)


def _rope_tables(n_lat, n_ctx):
    half = HEAD_DIM // 2
    nf = half // 2
    t = np.arange(n_lat)
    inv = ROPE_THETA ** (-(2.0 / half) * np.arange(nf, dtype=np.float64))
    ang_r = (t // GRID_W)[:, None] * inv
    ang_c = (t % GRID_W)[:, None] * inv
    zeros = np.zeros_like(ang_r)
    cos = np.concatenate([np.cos(ang_r), np.cos(ang_r), np.cos(ang_c), np.cos(ang_c)], axis=1)
    sin_fwd = np.concatenate([-np.sin(ang_r), zeros, -np.sin(ang_c), zeros], axis=1)
    sin_bwd = np.concatenate([zeros, np.sin(ang_r), zeros, np.sin(ang_c)], axis=1)
    pad = lambda a, v: np.concatenate([a, np.full((n_ctx, HEAD_DIM), v)], axis=0).astype(np.float32)
    return jnp.asarray(pad(cos, 1.0)), jnp.asarray(pad(sin_fwd, 0.0)), jnp.asarray(pad(sin_bwd, 0.0))


def _ab_in_kernel(h_ref, w_ref, qg_ref, kg_ref, cos_ref, sf_ref, sb_ref, f_ref, q_ref, k_ref, v_ref):
    acc = _bdot(h_ref[...], w_ref[...])
    cos, sf, sb = cos_ref[...], sf_ref[...], sb_ref[...]
    nf = HEAD_DIM // 4

    def norm_rope(xh, gain):
        ms = jnp.mean(xh * xh, axis=-1, keepdims=True)
        y = xh * lax.rsqrt(ms + RMS_EPS) * gain
        return y * cos + pltpu.roll(y, HEAD_DIM - nf, 1) * sf + pltpu.roll(y, nf, 1) * sb

    f_ref[...] = acc[:, :FNET_WIDTH]
    q0 = FNET_WIDTH
    k0 = q0 + GQA_Q_HEADS * HEAD_DIM
    v0 = k0 + GQA_KV_HEADS * HEAD_DIM
    for h in range(GQA_Q_HEADS):
        xh = acc[:, q0 + h * HEAD_DIM:q0 + (h + 1) * HEAD_DIM]
        q_ref[:, h * HEAD_DIM:(h + 1) * HEAD_DIM] = (norm_rope(xh, qg_ref[...]) * SCORE_SCALE).astype(BF16)
    for h in range(GQA_KV_HEADS):
        xh = acc[:, k0 + h * HEAD_DIM:k0 + (h + 1) * HEAD_DIM]
        k_ref[:, h * HEAD_DIM:(h + 1) * HEAD_DIM] = norm_rope(xh, kg_ref[...]).astype(BF16)
    v_ref[...] = acc[:, v0:].astype(BF16)


def _ab_in_call(st, h, w_in, q_gain, k_gain, rope):
    nq = GQA_Q_HEADS * HEAD_DIM
    nkv = GQA_KV_HEADS * HEAD_DIM
    pos_spec = pl.BlockSpec((TOKEN_TILE, HEAD_DIM), lambda t: (t % st.tiles_per_sample, 0))
    return pl.pallas_call(
        _ab_in_kernel,
        grid=(st.n_tiles,),
        in_specs=[st.tok_spec(st.d), _full_spec(w_in.shape), _full_spec((1, HEAD_DIM)), _full_spec((1, HEAD_DIM)),
                  pos_spec, pos_spec, pos_spec],
        out_specs=[st.tok_spec(FNET_WIDTH), st.tok_spec(nq), st.tok_spec(nkv), st.tok_spec(nkv)],
        out_shape=[jax.ShapeDtypeStruct((st.t, FNET_WIDTH), F32), jax.ShapeDtypeStruct((st.t, nq), BF16),
                   jax.ShapeDtypeStruct((st.t, nkv), BF16), jax.ShapeDtypeStruct((st.t, nkv), BF16)],
        compiler_params=_params("arbitrary"),
        name="ab_in",
    )(h, w_in, q_gain.reshape(1, HEAD_DIM), k_gain.reshape(1, HEAD_DIM), *rope)


def _attend(q, keys_values, biases):
    scores = []
    for (k, _), bias in zip(keys_values, biases):
        s = _bdot_t(q, k)
        scores.append(s if bias is None else s + bias)
    m = scores[0].max(axis=-1, keepdims=True)
    for s in scores[1:]:
        m = jnp.maximum(m, s.max(axis=-1, keepdims=True))
    num = None
    den = None
    for s, (_, v) in zip(scores, keys_values):
        p = jnp.exp2(s - m)
        pv = _bdot(p.astype(BF16), v)
        ps = p.sum(axis=-1, keepdims=True)
        num = pv if num is None else num + pv
        den = ps if den is None else den + ps
    return num / den


def _gqa_kernel(q_ref, k_ref, v_ref, o_ref, *, n_lat, lat_tiles):
    def run(k, v):
        for h in range(GQA_GROUP):
            sl = slice(h * HEAD_DIM, (h + 1) * HEAD_DIM)
            o_ref[:, sl] = _attend(q_ref[:, sl], [(k, v)], [None]).astype(BF16)

    is_lat = pl.program_id(2) < lat_tiles

    @pl.when(is_lat)
    def _():
        run(k_ref[...], v_ref[...])

    @pl.when(jnp.logical_not(is_lat))
    def _():
        run(k_ref[n_lat:, :], v_ref[n_lat:, :])


def _gqa_call(st, q, k, v):
    gw = GQA_GROUP * HEAD_DIM
    q3 = q.reshape(st.bsz, st.nt, GQA_Q_HEADS * HEAD_DIM)
    k3 = k.reshape(st.bsz, st.nt, GQA_KV_HEADS * HEAD_DIM)
    v3 = v.reshape(st.bsz, st.nt, GQA_KV_HEADS * HEAD_DIM)
    q_spec = pl.BlockSpec((None, TOKEN_TILE, gw), lambda b, g, i: (b, i, g))
    kv_spec = pl.BlockSpec((None, st.nt, HEAD_DIM), lambda b, g, i: (b, 0, g))
    o = pl.pallas_call(
        functools.partial(_gqa_kernel, n_lat=st.n_lat, lat_tiles=st.lat_tiles),
        grid=(st.bsz, GQA_KV_HEADS, st.tiles_per_sample),
        in_specs=[q_spec, kv_spec, kv_spec],
        out_specs=q_spec,
        out_shape=jax.ShapeDtypeStruct(q3.shape, BF16),
        compiler_params=_params("arbitrary", "arbitrary", "arbitrary"),
        name="gqa",
    )(q3, k3, v3)
    return o.reshape(st.t, GQA_Q_HEADS * HEAD_DIM)


def _fft_split(n):
    l1 = 1 << ((n.bit_length() - 1 + 1) // 2)
    assert n % l1 == 0 and n == l1 * (n // l1)
    return l1, n // l1


def _fft_tables(n):
    l1, l2 = _fft_split(n)
    a = np.arange(l1, dtype=np.float64)
    ang1 = 2.0 * np.pi * np.outer(a, a) / l1
    stage1 = np.concatenate([np.cos(ang1), -np.sin(ang1)], axis=0)
    b = np.arange(l2, dtype=np.float64)
    ang_t = 2.0 * np.pi * np.outer(b, a) / n
    tw_cos = np.cos(ang_t)[:, :, None]
    tw_sin = np.sin(ang_t)[:, :, None]
    ang2 = 2.0 * np.pi * np.outer(b, b) / l2
    c2, s2 = np.cos(ang2), np.sin(ang2)
    stage2 = np.block([[c2, s2], [-s2, c2]])
    f32 = lambda x: jnp.asarray(x.astype(np.float32))
    return f32(stage1), f32(tw_cos), f32(tw_sin), f32(stage2)


FNET_GROUPS_PER_SLAB = V7X_LANES // FNET_GROUP_DIM
FNET_SLABS = FNET_WIDTH // V7X_LANES
FNET_UNROLL = 4


def _fnet_channel_tables(n_positions):
    c = np.arange(FNET_GROUP_DIM, dtype=np.float64)
    ang = 2.0 * np.pi * np.outer(c, c) / FNET_GROUP_DIM
    eye = np.eye(FNET_GROUPS_PER_SLAB)
    scale = 1.0 / math.sqrt(n_positions * FNET_GROUP_DIM)
    m = np.concatenate([np.kron(eye, np.cos(ang)), np.kron(eye, np.sin(ang))], axis=0) * scale
    return jnp.asarray(m.astype(np.float32))


def _fnet_part(f_ref, o_ref, a_ref, row0, n, s1_ref, tc_ref, ts_ref, s2_ref, ch_ref, wf_ref):
    l1, l2 = _fft_split(n)
    stage1 = s1_ref[...]
    stage2 = s2_ref[...]
    chan = ch_ref[...]
    wf = wf_ref[...]

    def first(j, carry):
        xs = f_ref[pl.ds(row0 + j, l1, stride=l2), :]
        a = _dot3(stage1, xs)
        ar, ai = a[:l1], a[l1:]
        tc, ts = tc_ref[j], ts_ref[j]
        a_ref[0, pl.ds(pl.multiple_of(j * l1, l1), l1), :] = ar * tc + ai * ts
        a_ref[1, pl.ds(pl.multiple_of(j * l1, l1), l1), :] = ai * tc - ar * ts
        return carry

    lax.fori_loop(0, l2, first, 0, unroll=FNET_UNROLL)

    def second(j, carry):
        br = a_ref[0, pl.ds(j, l2, stride=l1), :]
        bi = a_ref[1, pl.ds(j, l2, stride=l1), :]
        p = _dot3(stage2, jnp.concatenate([br, bi], axis=0))
        re = _dot3(jnp.concatenate([p[:l2], p[l2:]], axis=1), chan)
        o_ref[pl.ds(row0 + j, l2, stride=l1), :] = _bdot(re.astype(BF16), wf)
        return carry

    lax.fori_loop(0, l1, second, 0, unroll=FNET_UNROLL)


def _fnet_kernel(f_ref, s1l, tcl, tsl, s2l, chl, s1c, tcc, tsc, s2c, chc, wf_ref, o_ref, a_ref, *, n_lat, n_ctx):
    _fnet_part(f_ref, o_ref, a_ref, 0, n_lat, s1l, tcl, tsl, s2l, chl, wf_ref)
    _fnet_part(f_ref, o_ref, a_ref, n_lat, n_ctx, s1c, tcc, tsc, s2c, chc, wf_ref)


def _fnet_call(st, f, w_fnet):
    gps = FNET_GROUPS_PER_SLAB
    eye = jnp.eye(gps, dtype=F32)
    wg = w_fnet.reshape(FNET_SLABS, gps, FNET_GROUP_DIM, FNET_GROUP_DIM)
    wf = (eye[None, :, None, :, None] * wg[:, :, :, None, :]).reshape(FNET_SLABS, V7X_LANES, V7X_LANES).astype(BF16)
    consts = (*_fft_tables(st.n_lat), _fnet_channel_tables(st.n_lat),
              *_fft_tables(st.n_ctx), _fnet_channel_tables(st.n_ctx))
    f3 = f.reshape(st.bsz, st.nt, FNET_WIDTH)
    blk = pl.BlockSpec((None, st.nt, V7X_LANES), lambda b, s: (b, 0, s))
    o = pl.pallas_call(
        functools.partial(_fnet_kernel, n_lat=st.n_lat, n_ctx=st.n_ctx),
        grid=(st.bsz, FNET_SLABS),
        in_specs=[blk] + [_full_spec(c.shape) for c in consts]
        + [pl.BlockSpec((None, V7X_LANES, V7X_LANES), lambda b, s: (s, 0, 0))],
        out_specs=blk,
        out_shape=jax.ShapeDtypeStruct(f3.shape, F32),
        scratch_shapes=[pltpu.VMEM((2, st.n_lat, V7X_LANES), F32)],
        compiler_params=_params("arbitrary", "arbitrary"),
        name="fnet",
    )(f3, *consts, wf)
    return o.reshape(st.t, FNET_WIDTH)


SEG_ALIGN = V7X_SUBLANES
PAIRS_PER_TILE = TOKEN_TILE * TOP_K
LOCAL_ROWS = PAIRS_PER_TILE + N_EXPERTS * SEG_ALIGN
ROUTE_LANES = V7X_LANES


def _route(h2, wr, e_bias):
    tm = h2.shape[0]
    scores = jax.nn.sigmoid(_dot3(h2, wr))
    sel = scores + e_bias
    lane = lax.broadcasted_iota(jnp.int32, sel.shape, 1).astype(F32)
    hits = []
    for _ in range(TOP_K):
        best = sel.max(axis=-1, keepdims=True)
        first = jnp.where(sel == best, lane, float(N_EXPERTS)).min(axis=-1, keepdims=True)
        hit = lane == first
        hits.append(hit)
        sel = jnp.where(hit, -jnp.inf, sel)
    chosen = hits[0]
    for hit in hits[1:]:
        chosen = jnp.logical_or(chosen, hit)
    chosen_f = jnp.where(chosen, 1.0, 0.0)
    counts = chosen_f.sum(axis=0, keepdims=True)
    seg_len = jnp.ceil(counts * (1.0 / SEG_ALIGN)) * SEG_ALIGN
    er = lax.broadcasted_iota(jnp.int32, (N_EXPERTS, N_EXPERTS), 0)
    ec = lax.broadcasted_iota(jnp.int32, (N_EXPERTS, N_EXPERTS), 1)
    lower_experts = jnp.where(er < ec, 1.0, 0.0).astype(BF16)
    seg_start = _bdot(jnp.broadcast_to(seg_len, (V7X_SUBLANES, N_EXPERTS)).astype(BF16), lower_experts)[0:1]
    r = lax.broadcasted_iota(jnp.int32, (tm, tm), 0)
    c = lax.broadcasted_iota(jnp.int32, (tm, tm), 1)
    earlier = jnp.where(c < r, 1.0, 0.0).astype(BF16)
    row_all = _bdot(earlier, chosen_f.astype(BF16)) + seg_start
    rec_lane = lax.broadcasted_iota(jnp.int32, (tm, ROUTE_LANES), 1)
    rec = jnp.zeros((tm, ROUTE_LANES), F32)
    raw = []
    for k, hit in enumerate(hits):
        rec = jnp.where(rec_lane == k, jnp.where(hit, row_all, 0.0).sum(axis=-1, keepdims=True), rec)
        raw.append(jnp.where(hit, scores, 0.0).sum(axis=-1, keepdims=True))
    total = raw[0]
    for w in raw[1:]:
        total = total + w
    for k, w in enumerate(raw):
        rec = jnp.where(rec_lane == TOP_K + k, w / total * ROUTE_SCALE, rec)
    return rec, counts.astype(jnp.int32)


def _out_ln_kernel(*refs, n_in, alpha):
    a_refs = refs[:n_in]
    w_refs = refs[n_in:2 * n_in]
    x_ref, mod_ref, g_ref, b_ref, wr_ref, eb_ref, x1_ref, h2_ref, rec_ref, cnt_ref = refs[2 * n_in:]
    y = None
    for a_ref, w_ref in zip(a_refs, w_refs):
        part = _bdot(a_ref[...].astype(BF16), w_ref[...])
        y = part if y is None else y + part
    z = alpha * x_ref[...] + mod_ref[2:3, :] * y
    x1 = _layer_norm(z, g_ref[...], b_ref[...])
    x1_ref[...] = x1
    h2 = x1 * (1.0 + mod_ref[4:5, :]) + mod_ref[3:4, :]
    h2_ref[...] = h2.astype(BF16)
    rec, counts = _route(h2, wr_ref[...], eb_ref[...])
    rec_ref[...] = rec
    cnt_ref[...] = counts


def _out_ln_call(st, acts, w_out, x, mod, layer, ln_g, ln_b, w_router, e_bias, alpha):
    ws, r0 = [], 0
    for a in acts:
        ws.append(w_out[r0:r0 + a.shape[1]].astype(BF16))
        r0 += a.shape[1]
    assert r0 == w_out.shape[0]
    row = lambda v: v.reshape(1, -1)
    return pl.pallas_call(
        functools.partial(_out_ln_kernel, n_in=len(acts), alpha=alpha),
        grid=(st.n_tiles,),
        in_specs=[st.tok_spec(a.shape[1]) for a in acts] + [_full_spec(w.shape) for w in ws]
        + [st.tok_spec(st.d), st.mod_spec(layer), _full_spec((1, st.d)), _full_spec((1, st.d)),
           _full_spec(w_router.shape), _full_spec((1, N_EXPERTS))],
        out_specs=[st.tok_spec(st.d), st.tok_spec(st.d), st.tok_spec(ROUTE_LANES),
                   pl.BlockSpec((None, 1, N_EXPERTS), lambda t: (t, 0, 0))],
        out_shape=[jax.ShapeDtypeStruct((st.t, st.d), F32), jax.ShapeDtypeStruct((st.t, st.d), BF16),
                   jax.ShapeDtypeStruct((st.t, ROUTE_LANES), F32),
                   jax.ShapeDtypeStruct((st.n_tiles, 1, N_EXPERTS), jnp.int32)],
        compiler_params=_params("arbitrary"),
        name="out_ln",
    )(*acts, *ws, x, mod, row(ln_g), row(ln_b), w_router, row(e_bias))


MOE_BLOCK = 512
ROW_CHUNK = 256
ONEHOT_STRIP = 32
GROUP_UNROLL = 4
ROW_GROUPS = LOCAL_ROWS // SEG_ALIGN
PLAN_WIDTH = (ROW_GROUPS // V7X_LANES + 1) * V7X_LANES
WAIT_SIZES = tuple(1 << b for b in range((LOCAL_ROWS - 1).bit_length() - 1, SEG_ALIGN.bit_length() - 2, -1))


def _moe_blocks(st):
    return (st.t * TOP_K + st.n_tiles * N_EXPERTS * (SEG_ALIGN - 1)) // MOE_BLOCK + N_EXPERTS


def _slot_plan(tile_counts, n_blocks):
    n_tiles = tile_counts.shape[0]
    cnt = tile_counts.reshape(n_tiles, N_EXPERTS)
    seg = (cnt + SEG_ALIGN - 1) // SEG_ALIGN * SEG_ALIGN
    local_start = jnp.cumsum(seg, axis=1) - seg
    tiles_before = jnp.cumsum(seg, axis=0) - seg
    total = seg.sum(axis=0)
    padded = (total + MOE_BLOCK - 1) // MOE_BLOCK * MOE_BLOCK
    pad_end = jnp.cumsum(padded)
    pad_start = pad_end - padded
    sorted_start = pad_start[None, :] + tiles_before
    group_row = jnp.arange(ROW_GROUPS, dtype=jnp.int32) * SEG_ALIGN
    owner = ((local_start[:, None, :] <= group_row[None, :, None])
             & (group_row[None, :, None] < (local_start + seg)[:, None, :]))
    group_dst = jnp.where(owner, (sorted_start - local_start)[:, None, :], 0).sum(-1) + group_row[None, :]
    tile_rows = jnp.broadcast_to(seg.sum(axis=1, keepdims=True), (n_tiles, PLAN_WIDTH - ROW_GROUPS))
    plan = jnp.concatenate([group_dst, tile_rows], axis=1).astype(jnp.int32).reshape(n_tiles, 1, PLAN_WIDTH)
    n_used = jnp.maximum(pad_end[-1] // MOE_BLOCK, 1)
    first_row = jnp.arange(n_blocks, dtype=jnp.int32) * MOE_BLOCK
    ends_expert = ((first_row[:, None] + MOE_BLOCK == pad_end[None, :]) & (padded[None, :] > 0)).any(-1)
    fill = (ends_expert | (first_row >= pad_end[-1])).astype(jnp.int32)
    first_block = (pad_start // MOE_BLOCK).astype(jnp.int32)
    block_count = (padded // MOE_BLOCK).astype(jnp.int32)
    return plan, first_block, block_count, n_used.astype(jnp.int32).reshape(1), fill


def _for_each_group(plan_ref, fn):
    def one(j):
        fn(pl.multiple_of(j * SEG_ALIGN, SEG_ALIGN), pl.multiple_of(plan_ref[0, j], SEG_ALIGN))

    def several(q, carry):
        for u in range(GROUP_UNROLL):
            one(q * GROUP_UNROLL + u)
        return carry

    def single(j, carry):
        one(j)
        return carry

    n = plan_ref[0, ROW_GROUPS] // SEG_ALIGN
    lax.fori_loop(0, n // GROUP_UNROLL, several, 0)
    lax.fori_loop(n // GROUP_UNROLL * GROUP_UNROLL, n, single, 0)


def _wait_tile_rows(plan_ref, make_copy):
    rows = plan_ref[0, ROW_GROUPS]
    for size in WAIT_SIZES:
        @pl.when((rows & size) != 0)
        def _(size=size):
            make_copy(size).wait()


def _pack_halves(x):
    bits = pltpu.bitcast(x, jnp.uint32)
    half = x.shape[1] // 2
    return (bits[:, :half] >> 16) | (bits[:, half:] & jnp.uint32(0xFFFF0000))


def _unpack_halves(w):
    lo = pltpu.bitcast(w << 16, F32).astype(BF16)
    hi = pltpu.bitcast(w & jnp.uint32(0xFFFF0000), F32).astype(BF16)
    return lo, hi


def _dispatch_kernel(fill_ref, plan_ref, plan1_ref, plan2_ref, rec_ref, h_ref, xs_ref, loc_ref, zero_ref, onehot_ref,
                     zsem, sem,
                     *, n_blocks, n_tiles):
    i = pl.program_id(0)
    slot = i % 2

    @pl.when(i == 0)
    def _():
        zero_ref[...] = jnp.zeros_like(zero_ref)

        def fill(b):
            return pltpu.make_async_copy(zero_ref, xs_ref.at[pl.ds(pl.multiple_of(b * MOE_BLOCK, MOE_BLOCK), MOE_BLOCK)],
                                         zsem)

        def start(b, c):
            @pl.when(fill_ref[b] > 0)
            def _():
                fill(b).start()
            return c

        def wait(b, c):
            @pl.when(fill_ref[b] > 0)
            def _():
                fill(b).wait()
            return c

        lax.fori_loop(0, n_blocks, start, 0)
        lax.fori_loop(0, n_blocks, wait, 0)

    def piece(s):
        def copy(local, dst, size):
            return pltpu.make_async_copy(loc_ref.at[s, pl.ds(local, size)], xs_ref.at[pl.ds(dst, size)], sem.at[s])
        return copy

    def rows_done(s):
        return lambda size: piece(s)(0, 0, size)

    @pl.when(i >= 2)
    def _():
        _wait_tile_rows(plan2_ref, rows_done(slot))

    rows_of = rec_ref[...].T
    x = h_ref[...]
    for c0 in range(0, LOCAL_ROWS, ROW_CHUNK):
        wanted = [jnp.broadcast_to(jnp.clip(rows_of[k:k + 1, :] - c0, -1.0, float(ROW_CHUNK)).astype(BF16),
                                   (ONEHOT_STRIP, TOKEN_TILE)) for k in range(TOP_K)]
        for r0 in range(0, ROW_CHUNK, ONEHOT_STRIP):
            row = (lax.broadcasted_iota(jnp.int32, (ONEHOT_STRIP, TOKEN_TILE), 0) + r0).astype(F32).astype(BF16)
            onehot = jnp.zeros((ONEHOT_STRIP, TOKEN_TILE), BF16)
            for k in range(TOP_K):
                onehot = jnp.where(row == wanted[k], jnp.ones_like(onehot), onehot)
            onehot_ref[r0:r0 + ONEHOT_STRIP, :] = onehot
        loc_ref[slot, c0:c0 + ROW_CHUNK, :] = _pack_halves(_bdot(onehot_ref[...], x))

    _for_each_group(plan_ref, lambda local, dst: piece(slot)(local, dst, SEG_ALIGN).start())

    @pl.when(i == n_tiles - 1)
    def _():
        if n_tiles >= 2:
            _wait_tile_rows(plan1_ref, rows_done(1 - slot))
        _wait_tile_rows(plan_ref, rows_done(slot))


def _dispatch_call(st, h2, rec, plan, fill, n_blocks):
    half = st.d // 2
    plan_spec = lambda back: pl.BlockSpec((None, 1, PLAN_WIDTH), lambda t, *_: (jnp.maximum(t - back, 0), 0, 0),
                                          memory_space=pltpu.SMEM)
    grid_spec = pltpu.PrefetchScalarGridSpec(
        num_scalar_prefetch=1,
        grid=(st.n_tiles,),
        in_specs=[plan_spec(0), plan_spec(1), plan_spec(2),
                  pl.BlockSpec((TOKEN_TILE, ROUTE_LANES), lambda t, *_: (t, 0)),
                  pl.BlockSpec((TOKEN_TILE, st.d), lambda t, *_: (t, 0))],
        out_specs=pl.BlockSpec(memory_space=pl.ANY),
        scratch_shapes=[pltpu.VMEM((2, LOCAL_ROWS, half), jnp.uint32), pltpu.VMEM((MOE_BLOCK, half), jnp.uint32),
                        pltpu.VMEM((ROW_CHUNK, TOKEN_TILE), BF16),
                        pltpu.SemaphoreType.DMA, pltpu.SemaphoreType.DMA((2,))],
    )
    return pl.pallas_call(
        functools.partial(_dispatch_kernel, n_blocks=n_blocks, n_tiles=st.n_tiles),
        grid_spec=grid_spec,
        out_shape=jax.ShapeDtypeStruct((n_blocks * MOE_BLOCK, half), jnp.uint32),
        compiler_params=_params("arbitrary"),
        name="moe_dispatch",
    )(fill, plan, plan, plan, rec, h2)


EXPERT_BUFFERS = 3


def _experts_kernel(first_ref, count_ref, nu_ref, xs_ref, wg_ref, wu_ref, wd_ref, ys_ref, xbuf, ybuf, wg_s, wu_s, wd_s,
                    xsem, ysem, *, n_blocks):
    e = pl.program_id(0)
    n_used = nu_ref[0]

    def rows(b):
        return pl.ds(pl.multiple_of(b * MOE_BLOCK, MOE_BLOCK), MOE_BLOCK)

    def fetch(b):
        s = b % EXPERT_BUFFERS
        return pltpu.make_async_copy(xs_ref.at[rows(b)], xbuf.at[s], xsem.at[s])

    def store(b):
        s = b % EXPERT_BUFFERS
        return pltpu.make_async_copy(ybuf.at[s], ys_ref.at[rows(b)], ysem.at[s])

    @pl.when(e == 0)
    def _():
        for b in range(EXPERT_BUFFERS - 1):
            @pl.when(b < n_used)
            def _(b=b):
                fetch(b).start()

    @pl.when(count_ref[e] > 0)
    def _():
        wg_s[...] = wg_ref[...].astype(BF16)
        wu_s[...] = wu_ref[...].astype(BF16)
        wd_s[...] = wd_ref[...].astype(BF16)

    def block(b, carry):
        @pl.when(b + EXPERT_BUFFERS - 1 < n_used)
        def _():
            fetch(b + EXPERT_BUFFERS - 1).start()

        fetch(b).wait()

        @pl.when(b >= EXPERT_BUFFERS)
        def _():
            store(b - EXPERT_BUFFERS).wait()

        slot = b % EXPERT_BUFFERS
        half = wg_s.shape[0] // 2
        lo, hi = _unpack_halves(xbuf[slot])
        gate = _bdot(lo, wg_s[:half, :]) + _bdot(hi, wg_s[half:, :])
        up = _bdot(lo, wu_s[:half, :]) + _bdot(hi, wu_s[half:, :])
        y = _bdot((_silu(gate) * up).astype(BF16), wd_s[...])
        ybuf[slot] = _pack_halves(y.astype(BF16).astype(F32))
        store(b).start()
        return carry

    lax.fori_loop(first_ref[e], first_ref[e] + count_ref[e], block, 0)

    @pl.when(e == N_EXPERTS - 1)
    def _():
        def drain(b, carry):
            store(b).wait()
            return carry

        lax.fori_loop(jnp.maximum(n_used - EXPERT_BUFFERS, 0), n_used, drain, 0)
        ybuf[0] = jnp.zeros(ybuf.shape[1:], ybuf.dtype)

        def zero_tail(b):
            return pltpu.make_async_copy(ybuf.at[0], ys_ref.at[rows(b)], ysem.at[0])

        def start(b, carry):
            zero_tail(b).start()
            return carry

        def wait(b, carry):
            zero_tail(b).wait()
            return carry

        lax.fori_loop(n_used, n_blocks, start, 0)
        lax.fori_loop(n_used, n_blocks, wait, 0)


def _experts_call(st, xs, first_block, block_count, n_used, w_gate, w_up, w_down, layer, n_blocks):
    half = st.d // 2
    w_in_spec = pl.BlockSpec((None, None, st.d, EXPERT_DIM), lambda e, *_: (layer, e, 0, 0))
    grid_spec = pltpu.PrefetchScalarGridSpec(
        num_scalar_prefetch=3,
        grid=(N_EXPERTS,),
        in_specs=[pl.BlockSpec(memory_space=pl.ANY),
                  w_in_spec, w_in_spec,
                  pl.BlockSpec((None, None, EXPERT_DIM, st.d), lambda e, *_: (layer, e, 0, 0))],
        out_specs=pl.BlockSpec(memory_space=pl.ANY),
        scratch_shapes=[pltpu.VMEM((EXPERT_BUFFERS, MOE_BLOCK, half), jnp.uint32),
                        pltpu.VMEM((EXPERT_BUFFERS, MOE_BLOCK, half), jnp.uint32),
                        pltpu.VMEM((st.d, EXPERT_DIM), BF16), pltpu.VMEM((st.d, EXPERT_DIM), BF16),
                        pltpu.VMEM((EXPERT_DIM, st.d), BF16),
                        pltpu.SemaphoreType.DMA((EXPERT_BUFFERS,)), pltpu.SemaphoreType.DMA((EXPERT_BUFFERS,))],
    )
    return pl.pallas_call(
        functools.partial(_experts_kernel, n_blocks=n_blocks),
        grid_spec=grid_spec,
        out_shape=jax.ShapeDtypeStruct(xs.shape, jnp.uint32),
        compiler_params=_params("arbitrary"),
        name="moe_experts",
    )(first_block, block_count, n_used, xs, w_gate, w_up, w_down)


def _ffn_ln_kernel(plan_ref, plan_next_ref, rec_ref, h2_ref, x1_ref, sg_ref, su_ref, sd_ref, mod_ref, modn_ref, g_ref,
                   b_ref, ys_ref, x2_ref, hn_ref, loc_ref, weight_ref, sem, *, alpha, n_tiles):
    i = pl.program_id(0)
    slot = i % 2

    def piece(s):
        def copy(local, src, size):
            return pltpu.make_async_copy(ys_ref.at[pl.ds(src, size)], loc_ref.at[s, pl.ds(local, size)], sem.at[s])
        return copy

    @pl.when(i == 0)
    def _():
        loc_ref[...] = jnp.zeros_like(loc_ref)
        _for_each_group(plan_ref, lambda local, src: piece(0)(local, src, SEG_ALIGN).start())

    @pl.when(i + 1 < n_tiles)
    def _():
        _for_each_group(plan_next_ref, lambda local, src: piece(1 - slot)(local, src, SEG_ALIGN).start())

    h2 = h2_ref[...]
    a = _silu(_bdot(h2, sg_ref[...])) * _bdot(h2, su_ref[...])
    shared = _bdot(a.astype(BF16), sd_ref[...])

    _wait_tile_rows(plan_ref, lambda size: piece(slot)(0, 0, size))
    rec_t = rec_ref[...].T
    shares = [jnp.broadcast_to(rec_t[TOP_K + k:TOP_K + k + 1, :].astype(BF16), (ONEHOT_STRIP, TOKEN_TILE))
              for k in range(TOP_K)]
    half = loc_ref.shape[2]
    routed_lo = jnp.zeros((TOKEN_TILE, half), F32)
    routed_hi = jnp.zeros((TOKEN_TILE, half), F32)
    for c0 in range(0, LOCAL_ROWS, ROW_CHUNK):
        wanted = [jnp.broadcast_to(jnp.clip(rec_t[k:k + 1, :] - c0, -1.0, float(ROW_CHUNK)).astype(BF16),
                                   (ONEHOT_STRIP, TOKEN_TILE)) for k in range(TOP_K)]
        for r0 in range(0, ROW_CHUNK, ONEHOT_STRIP):
            row = (lax.broadcasted_iota(jnp.int32, (ONEHOT_STRIP, TOKEN_TILE), 0) + r0).astype(F32).astype(BF16)
            weight = jnp.zeros((ONEHOT_STRIP, TOKEN_TILE), BF16)
            for k in range(TOP_K):
                weight = jnp.where(row == wanted[k], shares[k], weight)
            weight_ref[r0:r0 + ONEHOT_STRIP, :] = weight
        lo, hi = _unpack_halves(loc_ref[slot, c0:c0 + ROW_CHUNK, :])
        weight = weight_ref[...]
        routed_lo = routed_lo + _bdot_tn(weight, lo)
        routed_hi = routed_hi + _bdot_tn(weight, hi)
    ff = jnp.concatenate([routed_lo, routed_hi], axis=1) + shared
    z = alpha * x1_ref[...] + mod_ref[5:6, :] * ff
    x2 = _layer_norm(z, g_ref[...], b_ref[...])
    x2_ref[...] = x2
    hn_ref[...] = (x2 * (1.0 + modn_ref[1:2, :]) + modn_ref[0:1, :]).astype(BF16)


def _ffn_ln_call(st, ys, plan, rec, h2, x1, s_gate, s_up, s_down, mod, layer, next_layer, ln_g, ln_b, alpha):
    row = lambda v: v.reshape(1, -1)
    sg, su, sd = s_gate.astype(BF16), s_up.astype(BF16), s_down.astype(BF16)
    last = st.n_tiles - 1
    return pl.pallas_call(
        functools.partial(_ffn_ln_kernel, alpha=alpha, n_tiles=st.n_tiles),
        grid=(st.n_tiles,),
        in_specs=[pl.BlockSpec((None, 1, PLAN_WIDTH), lambda t: (t, 0, 0), memory_space=pltpu.SMEM),
                  pl.BlockSpec((None, 1, PLAN_WIDTH), lambda t: (jnp.minimum(t + 1, last), 0, 0),
                               memory_space=pltpu.SMEM),
                  st.tok_spec(ROUTE_LANES), st.tok_spec(st.d), st.tok_spec(st.d),
                  _full_spec(sg.shape), _full_spec(su.shape), _full_spec(sd.shape),
                  st.mod_spec(layer), st.mod_spec(next_layer), _full_spec((1, st.d)), _full_spec((1, st.d)),
                  pl.BlockSpec(memory_space=pl.ANY)],
        out_specs=[st.tok_spec(st.d), st.tok_spec(st.d)],
        out_shape=[jax.ShapeDtypeStruct((st.t, st.d), F32), jax.ShapeDtypeStruct((st.t, st.d), BF16)],
        scratch_shapes=[pltpu.VMEM((2, LOCAL_ROWS, st.d // 2), jnp.uint32),
                        pltpu.VMEM((TOKEN_TILE, ROW_CHUNK), BF16), pltpu.SemaphoreType.DMA((2,))],
        compiler_params=_params("arbitrary"),
        name="ffn_ln",
    )(plan, plan, rec, h2, x1, sg, su, sd, mod, mod, row(ln_g), row(ln_b), ys)


def _na_in_kernel(h_ref, w_ref, o_ref):
    h = h_ref[...]
    o_ref[:, :NA_WIDTH] = (_bdot(h, w_ref[:, :NA_WIDTH]) * SCORE_SCALE).astype(BF16)
    o_ref[:, NA_WIDTH:] = _bdot(h, w_ref[:, NA_WIDTH:]).astype(BF16)


def _na_in_call(st, h, w_in):
    return pl.pallas_call(
        _na_in_kernel,
        grid=(st.n_tiles,),
        in_specs=[st.tok_spec(st.d), _full_spec(w_in.shape)],
        out_specs=st.tok_spec(3 * NA_WIDTH),
        out_shape=jax.ShapeDtypeStruct((st.t, 3 * NA_WIDTH), BF16),
        compiler_params=_params("arbitrary"),
        name="na_in",
    )(h, w_in)


def _na_geometry(n_lat):
    rows = n_lat // GRID_W
    kh, kw, qr = min(NA_KH, rows), min(NA_KW, GRID_W), NA_Q_ROWS
    nbr = min(qr + kh - 1, rows)
    col = np.arange(GRID_W)
    col_start = np.clip(col - kw // 2, 0, GRID_W - kw)
    in_col = (col[None, :] >= col_start[:, None]) & (col[None, :] < col_start[:, None] + kw)
    dc = np.clip(col[None, :] - col[:, None] + NA_KW - 1, 0, 2 * NA_KW - 2)
    starts, variant_of, variants = [], [], {}
    for i in range(rows // qr):
        qrow = i * qr + np.arange(qr)
        rstart = np.clip(qrow - kh // 2, 0, rows - kh)
        bs = min(int(rstart[0]), rows - nbr)
        krow = bs + np.arange(nbr)
        in_row = (krow[None, :] >= rstart[:, None]) & (krow[None, :] < rstart[:, None] + kh)
        dr = np.clip(krow[None, :] - qrow[:, None] + NA_KH - 1, 0, 2 * NA_KH - 2)
        key = (in_row.tobytes(), dr.tobytes())
        if key not in variants:
            mask = (in_row[:, None, :, None] & in_col[None, :, None, :]).reshape(qr * GRID_W, nbr * GRID_W)
            variants[key] = (len(variants), dr, mask)
        starts.append(bs)
        variant_of.append(variants[key][0])
    ordered = sorted(variants.values(), key=lambda v: v[0])
    return nbr, np.asarray(starts, np.int32), np.asarray(variant_of, np.int32), [(v[1], v[2]) for v in ordered], dc


def _na_bias_kernel(tab_ref, place_ref, mask_ref, o_ref):
    tab = tab_ref[...]
    hi = tab.astype(BF16)
    rest = tab - hi.astype(F32)
    mid = rest.astype(BF16)
    lo = (rest - mid.astype(F32)).astype(BF16)
    for a in range(NA_Q_ROWS):
        place = place_ref[a]
        rows = _bdot(hi, place) + (_bdot(mid, place) + _bdot(lo, place))
        sl = slice(a * GRID_W, (a + 1) * GRID_W)
        o_ref[sl, :] = jnp.where(mask_ref[sl, :] > 0.0, rows * LOG2_E, NEG_INF)


def _na_bias_tables(rpb, n_lat, n_ctx):
    nbr, starts, variant_of, variants, dc = _na_geometry(n_lat)
    n_dr, n_dc = 2 * NA_KH - 1, 2 * NA_KW - 1
    band = nbr * GRID_W
    n_keys = band + n_ctx
    col_sel = jnp.asarray((dc[:, :, None] == np.arange(n_dc)).astype(np.float32))
    tab = jnp.einsum("hrc,wuc->hwru", rpb, col_sel, precision=lax.Precision.HIGHEST)
    tab = tab.reshape(NA_HEADS, GRID_W, n_dr * GRID_W)
    place = np.zeros((len(variants), NA_Q_ROWS, n_dr, GRID_W, n_keys), np.float32)
    masks = np.ones((len(variants), NA_Q_ROWS * GRID_W, n_keys), np.float32)
    u = np.arange(GRID_W)
    for v, (dr, mask) in enumerate(variants):
        masks[v, :, :band] = mask
        for a in range(NA_Q_ROWS):
            for j in range(nbr):
                place[v, a, dr[a, j], u, j * GRID_W + u] = 1.0
    place = jnp.asarray(place.reshape(len(variants), NA_Q_ROWS, n_dr * GRID_W, n_keys), BF16)
    bias = pl.pallas_call(
        _na_bias_kernel,
        grid=(len(variants), NA_HEADS),
        in_specs=[pl.BlockSpec((None, GRID_W, n_dr * GRID_W), lambda v, h: (h, 0, 0)),
                  pl.BlockSpec((None, NA_Q_ROWS, n_dr * GRID_W, n_keys), lambda v, h: (v, 0, 0, 0)),
                  pl.BlockSpec((None, NA_Q_ROWS * GRID_W, n_keys), lambda v, h: (v, 0, 0))],
        out_specs=pl.BlockSpec((None, None, NA_Q_ROWS * GRID_W, n_keys), lambda v, h: (v, h, 0, 0)),
        out_shape=jax.ShapeDtypeStruct((len(variants), NA_HEADS, NA_Q_ROWS * GRID_W, n_keys), F32),
        compiler_params=_params("arbitrary", "arbitrary"),
        name="na_bias",
    )(tab, place, jnp.asarray(masks))
    return nbr, starts, variant_of, bias


NA_Q_TILE = NA_Q_ROWS * GRID_W


def _na_kernel(start_ref, var_ref, q_ref, k_ref, v_ref, bias_ref, o_ref, *, n_lat, band):
    i = pl.program_id(1)
    is_lat = i < n_lat // NA_Q_TILE

    @pl.when(is_lat)
    def _():
        off = pl.multiple_of(start_ref[i] * GRID_W, GRID_W)
        for h in range(NA_HEADS):
            sl = slice(h * HEAD_DIM, (h + 1) * HEAD_DIM)
            keys = jnp.concatenate([k_ref[pl.ds(off, band), sl], k_ref[n_lat:, sl]], axis=0)
            values = jnp.concatenate([v_ref[pl.ds(off, band), sl], v_ref[n_lat:, sl]], axis=0)
            o_ref[:, sl] = _attend(q_ref[:, sl], [(keys, values)], [bias_ref[h]]).astype(BF16)

    @pl.when(jnp.logical_not(is_lat))
    def _():
        for h in range(NA_HEADS):
            sl = slice(h * HEAD_DIM, (h + 1) * HEAD_DIM)
            o_ref[:, sl] = _attend(q_ref[:, sl], [(k_ref[n_lat:, sl], v_ref[n_lat:, sl])], [None]).astype(BF16)


def _na_call(st, qkv, rpb):
    nbr, starts, variant_of, bias = _na_bias_tables(rpb, st.n_lat, st.n_ctx)
    band = nbr * GRID_W
    n_q = st.nt // NA_Q_TILE
    pad = n_q - starts.shape[0]
    starts = jnp.asarray(np.concatenate([starts, np.zeros(pad, np.int32)]))
    variant_of = jnp.asarray(np.concatenate([variant_of, np.zeros(pad, np.int32)]))
    qkv3 = qkv.reshape(st.bsz, st.nt, 3 * NA_WIDTH)
    grid_spec = pltpu.PrefetchScalarGridSpec(
        num_scalar_prefetch=2,
        grid=(st.bsz, n_q),
        in_specs=[pl.BlockSpec((None, NA_Q_TILE, NA_WIDTH), lambda b, i, s, v: (b, i, 0)),
                  pl.BlockSpec((None, st.nt, NA_WIDTH), lambda b, i, s, v: (b, 0, 1)),
                  pl.BlockSpec((None, st.nt, NA_WIDTH), lambda b, i, s, v: (b, 0, 2)),
                  pl.BlockSpec((None, NA_HEADS, NA_Q_TILE, band + st.n_ctx), lambda b, i, s, v: (v[i], 0, 0, 0))],
        out_specs=pl.BlockSpec((None, NA_Q_TILE, NA_WIDTH), lambda b, i, s, v: (b, i, 0)),
    )
    o = pl.pallas_call(
        functools.partial(_na_kernel, n_lat=st.n_lat, band=band),
        grid_spec=grid_spec,
        out_shape=jax.ShapeDtypeStruct((st.bsz, st.nt, NA_WIDTH), BF16),
        compiler_params=_params("arbitrary", "arbitrary"),
        name="na_attn",
    )(starts, variant_of, qkv3, qkv3, qkv3, bias)
    return o.reshape(st.t, NA_WIDTH)


def kernel(x, c, ctx, c_ctx, w_mod, b_mod, ln1_g, ln1_b, ln2_g, ln2_b, ab_w_in, ab_w_fnet, ab_q_norm, ab_k_norm,
           ab_w_out, na_w_in, na_rpb, na_w_out, moe_w_router, moe_bias, moe_w_gate, moe_w_up, moe_w_down,
           sh_w_gate, sh_w_up, sh_w_down):
    bsz, n_lat, d = x.shape
    n_ctx = ctx.shape[1]
    depth = w_mod.shape[0]
    st = _Stream(bsz, n_lat, n_ctx, d)
    alpha = (2 * depth) ** 0.25

    cc = jnp.concatenate([c, c_ctx[None, :], jnp.zeros((MOD_ROWS - bsz - 1, d), F32)], axis=0)
    mod = _mod_call(cc, w_mod, b_mod).reshape(depth, MOD_ROWS, 6, d)
    xs = jnp.concatenate([x, ctx], axis=1).reshape(st.t, d)
    h = _modulate_call(st, xs, mod, 0)
    rope = _rope_tables(n_lat, n_ctx)
    n_blocks = _moe_blocks(st)

    for l in range(depth):
        j = l // 2
        if l % 2 == 0:
            f, q, k, v = _ab_in_call(st, h, ab_w_in[j].astype(BF16), ab_q_norm[j], ab_k_norm[j], rope)
            acts = [_fnet_call(st, f, ab_w_fnet[j]), _gqa_call(st, q, k, v)]
            w_out = ab_w_out[j]
        else:
            qkv = _na_in_call(st, h, na_w_in[j].astype(BF16))
            acts = [_na_call(st, qkv, na_rpb[j])]
            w_out = na_w_out[j]
        x1, h2, rec, tile_counts = _out_ln_call(st, acts, w_out, xs, mod, l, ln1_g[l], ln1_b[l],
                                                moe_w_router[l], moe_bias[l], alpha)
        plan, first_block, block_count, n_used, fill = _slot_plan(tile_counts, n_blocks)
        rows = _dispatch_call(st, h2, rec, plan, fill, n_blocks)
        ys = _experts_call(st, rows, first_block, block_count, n_used, moe_w_gate, moe_w_up, moe_w_down, l, n_blocks)
        xs, h = _ffn_ln_call(st, ys, plan, rec, h2, x1, sh_w_gate[l], sh_w_up[l], sh_w_down[l], mod, l,
                             min(l + 1, depth - 1), ln2_g[l], ln2_b[l], alpha)
    return xs.reshape(bsz, st.nt, d)[:, :n_lat]
```

```python
import functools
import math

import numpy as np
import jax
import jax.numpy as jnp
from jax import lax
from jax.experimental import pallas as pl
from jax.experimental.pallas import tpu as pltpu

GRID_W = 64
HEAD_DIM = 128
FNET_GROUPS = 4
FNET_GROUP_DIM = 64
FNET_WIDTH = FNET_GROUPS * FNET_GROUP_DIM
GQA_Q_HEADS = 6
GQA_KV_HEADS = 2
GQA_GROUP = GQA_Q_HEADS // GQA_KV_HEADS
ROPE_THETA = 10000.0
NA_HEADS = 8
NA_WIDTH = NA_HEADS * HEAD_DIM
NA_KH = 8
NA_KW = 16
NA_Q_ROWS = 2
NEG_INF = -1e30
N_EXPERTS = 64
TOP_K = 8
EXPERT_DIM = 256
ROUTE_SCALE = 2.5
LN_EPS = 1e-6
RMS_EPS = 1e-6
ATTN_SCALE = HEAD_DIM ** -0.5
LOG2_E = math.log2(math.e)
SCORE_SCALE = ATTN_SCALE * LOG2_E

V7X_LANES = 128
V7X_SUBLANES = 8
V7X_VMEM_LIMIT_BYTES = 56 * 1024 * 1024

TOKEN_TILE = 256
MOD_ROWS = 8

F32 = jnp.float32
BF16 = jnp.bfloat16


def _params(*sem):
    return pltpu.CompilerParams(dimension_semantics=sem, vmem_limit_bytes=V7X_VMEM_LIMIT_BYTES)


def _bdot(a, b):
    return jnp.dot(a, b, preferred_element_type=F32)


def _bdot_t(a, b):
    return lax.dot_general(a, b, (((1,), (1,)), ((), ())), preferred_element_type=F32)


def _bdot_tn(a, b):
    return lax.dot_general(a, b, (((0,), (0,)), ((), ())), preferred_element_type=F32)


def _split(x):
    hi = x.astype(BF16)
    lo = (x - hi.astype(F32)).astype(BF16)
    return hi, lo


def _dot3(a, b):
    ah, al = _split(a)
    bh, bl = _split(b)
    return _bdot(ah, bh) + (_bdot(ah, bl) + _bdot(al, bh))


def _silu(x):
    return x * jax.nn.sigmoid(x)


def _layer_norm(z, g, b):
    mu = jnp.mean(z, axis=-1, keepdims=True)
    zc = z - mu
    var = jnp.mean(zc * zc, axis=-1, keepdims=True)
    return zc * lax.rsqrt(var + LN_EPS) * g + b


class _Stream:
    def __init__(self, bsz, n_lat, n_ctx, d):
        assert n_lat % TOKEN_TILE == 0 and n_ctx % TOKEN_TILE == 0
        assert bsz < MOD_ROWS
        self.bsz, self.n_lat, self.n_ctx, self.d = bsz, n_lat, n_ctx, d
        self.nt = n_lat + n_ctx
        self.t = bsz * self.nt
        self.tiles_per_sample = self.nt // TOKEN_TILE
        self.lat_tiles = n_lat // TOKEN_TILE
        self.n_tiles = self.t // TOKEN_TILE

    def mod_row(self, tile):
        return jnp.where(tile % self.tiles_per_sample < self.lat_tiles, tile // self.tiles_per_sample, self.bsz)

    def mod_spec(self, layer):
        return pl.BlockSpec((None, None, 6, self.d), lambda t: (layer, self.mod_row(t), 0, 0))

    def tok_spec(self, width):
        return pl.BlockSpec((TOKEN_TILE, width), lambda t: (t, 0))


def _full_spec(shape):
    nd = len(shape)
    return pl.BlockSpec(shape, lambda *_: (0,) * nd)


def _mod_kernel(cc_ref, w_ref, b_ref, o_ref):
    o_ref[...] = _dot3(_silu(cc_ref[...]), w_ref[...]) + b_ref[...]


def _mod_call(cc, w_mod, b_mod):
    depth, d, n = w_mod.shape
    tn = n // 4
    return pl.pallas_call(
        _mod_kernel,
        grid=(depth, n // tn),
        in_specs=[pl.BlockSpec((MOD_ROWS, d), lambda l, j: (0, 0)),
                  pl.BlockSpec((None, d, tn), lambda l, j: (l, 0, j)),
                  pl.BlockSpec((None, 1, tn), lambda l, j: (l, 0, j))],
        out_specs=pl.BlockSpec((None, MOD_ROWS, tn), lambda l, j: (l, 0, j)),
        out_shape=jax.ShapeDtypeStruct((depth, MOD_ROWS, n), F32),
        compiler_params=_params("arbitrary", "arbitrary"),
        name="mod",
    )(cc, w_mod, b_mod.reshape(depth, 1, n))


def _modulate_kernel(x_ref, mod_ref, h_ref):
    h_ref[...] = (x_ref[...] * (1.0 + mod_ref[1:2, :]) + mod_ref[0:1, :]).astype(BF16)


def _modulate_call(st, x, mod, layer):
    return pl.pallas_call(
        _modulate_kernel,
        grid=(st.n_tiles,),
        in_specs=[st.tok_spec(st.d), st.mod_spec(layer)],
        out_specs=st.tok_spec(st.d),
        out_shape=jax.ShapeDtypeStruct((st.t, st.d), BF16),
        compiler_params=_params("arbitrary"),
        name="modulate",
    )(x, mod)


def _rope_tables(n_lat, n_ctx):
    half = HEAD_DIM // 2
    nf = half // 2
    t = np.arange(n_lat)
    inv = ROPE_THETA ** (-(2.0 / half) * np.arange(nf, dtype=np.float64))
    ang_r = (t // GRID_W)[:, None] * inv
    ang_c = (t % GRID_W)[:, None] * inv
    zeros = np.zeros_like(ang_r)
    cos = np.concatenate([np.cos(ang_r), np.cos(ang_r), np.cos(ang_c), np.cos(ang_c)], axis=1)
    sin_fwd = np.concatenate([-np.sin(ang_r), zeros, -np.sin(ang_c), zeros], axis=1)
    sin_bwd = np.concatenate([zeros, np.sin(ang_r), zeros, np.sin(ang_c)], axis=1)
    pad = lambda a, v: np.concatenate([a, np.full((n_ctx, HEAD_DIM), v)], axis=0).astype(np.float32)
    return jnp.asarray(pad(cos, 1.0)), jnp.asarray(pad(sin_fwd, 0.0)), jnp.asarray(pad(sin_bwd, 0.0))


def _ab_in_kernel(h_ref, w_ref, qg_ref, kg_ref, cos_ref, sf_ref, sb_ref, f_ref, q_ref, k_ref, v_ref):
    acc = _bdot(h_ref[...], w_ref[...])
    cos, sf, sb = cos_ref[...], sf_ref[...], sb_ref[...]
    nf = HEAD_DIM // 4

    def norm_rope(xh, gain):
        ms = jnp.mean(xh * xh, axis=-1, keepdims=True)
        y = xh * lax.rsqrt(ms + RMS_EPS) * gain
        return y * cos + pltpu.roll(y, HEAD_DIM - nf, 1) * sf + pltpu.roll(y, nf, 1) * sb

    f_ref[...] = acc[:, :FNET_WIDTH]
    q0 = FNET_WIDTH
    k0 = q0 + GQA_Q_HEADS * HEAD_DIM
    v0 = k0 + GQA_KV_HEADS * HEAD_DIM
    for h in range(GQA_Q_HEADS):
        xh = acc[:, q0 + h * HEAD_DIM:q0 + (h + 1) * HEAD_DIM]
        q_ref[:, h * HEAD_DIM:(h + 1) * HEAD_DIM] = (norm_rope(xh, qg_ref[...]) * SCORE_SCALE).astype(BF16)
    for h in range(GQA_KV_HEADS):
        xh = acc[:, k0 + h * HEAD_DIM:k0 + (h + 1) * HEAD_DIM]
        k_ref[:, h * HEAD_DIM:(h + 1) * HEAD_DIM] = norm_rope(xh, kg_ref[...]).astype(BF16)
    v_ref[...] = acc[:, v0:].astype(BF16)


def _ab_in_call(st, h, w_in, q_gain, k_gain, rope):
    nq = GQA_Q_HEADS * HEAD_DIM
    nkv = GQA_KV_HEADS * HEAD_DIM
    pos_spec = pl.BlockSpec((TOKEN_TILE, HEAD_DIM), lambda t: (t % st.tiles_per_sample, 0))
    return pl.pallas_call(
        _ab_in_kernel,
        grid=(st.n_tiles,),
        in_specs=[st.tok_spec(st.d), _full_spec(w_in.shape), _full_spec((1, HEAD_DIM)), _full_spec((1, HEAD_DIM)),
                  pos_spec, pos_spec, pos_spec],
        out_specs=[st.tok_spec(FNET_WIDTH), st.tok_spec(nq), st.tok_spec(nkv), st.tok_spec(nkv)],
        out_shape=[jax.ShapeDtypeStruct((st.t, FNET_WIDTH), F32), jax.ShapeDtypeStruct((st.t, nq), BF16),
                   jax.ShapeDtypeStruct((st.t, nkv), BF16), jax.ShapeDtypeStruct((st.t, nkv), BF16)],
        compiler_params=_params("arbitrary"),
        name="ab_in",
    )(h, w_in, q_gain.reshape(1, HEAD_DIM), k_gain.reshape(1, HEAD_DIM), *rope)


def _attend(q, keys_values, biases):
    scores = []
    for (k, _), bias in zip(keys_values, biases):
        s = _bdot_t(q, k)
        scores.append(s if bias is None else s + bias)
    m = scores[0].max(axis=-1, keepdims=True)
    for s in scores[1:]:
        m = jnp.maximum(m, s.max(axis=-1, keepdims=True))
    num = None
    den = None
    for s, (_, v) in zip(scores, keys_values):
        p = jnp.exp2(s - m)
        pv = _bdot(p.astype(BF16), v)
        ps = p.sum(axis=-1, keepdims=True)
        num = pv if num is None else num + pv
        den = ps if den is None else den + ps
    return num / den


def _gqa_kernel(q_ref, k_ref, v_ref, o_ref, *, n_lat, lat_tiles):
    def run(k, v):
        for h in range(GQA_GROUP):
            sl = slice(h * HEAD_DIM, (h + 1) * HEAD_DIM)
            o_ref[:, sl] = _attend(q_ref[:, sl], [(k, v)], [None]).astype(BF16)

    is_lat = pl.program_id(2) < lat_tiles

    @pl.when(is_lat)
    def _():
        run(k_ref[...], v_ref[...])

    @pl.when(jnp.logical_not(is_lat))
    def _():
        run(k_ref[n_lat:, :], v_ref[n_lat:, :])


def _gqa_call(st, q, k, v):
    gw = GQA_GROUP * HEAD_DIM
    q3 = q.reshape(st.bsz, st.nt, GQA_Q_HEADS * HEAD_DIM)
    k3 = k.reshape(st.bsz, st.nt, GQA_KV_HEADS * HEAD_DIM)
    v3 = v.reshape(st.bsz, st.nt, GQA_KV_HEADS * HEAD_DIM)
    q_spec = pl.BlockSpec((None, TOKEN_TILE, gw), lambda b, g, i: (b, i, g))
    kv_spec = pl.BlockSpec((None, st.nt, HEAD_DIM), lambda b, g, i: (b, 0, g))
    o = pl.pallas_call(
        functools.partial(_gqa_kernel, n_lat=st.n_lat, lat_tiles=st.lat_tiles),
        grid=(st.bsz, GQA_KV_HEADS, st.tiles_per_sample),
        in_specs=[q_spec, kv_spec, kv_spec],
        out_specs=q_spec,
        out_shape=jax.ShapeDtypeStruct(q3.shape, BF16),
        compiler_params=_params("arbitrary", "arbitrary", "arbitrary"),
        name="gqa",
    )(q3, k3, v3)
    return o.reshape(st.t, GQA_Q_HEADS * HEAD_DIM)


def _fft_split(n):
    l1 = 1 << ((n.bit_length() - 1 + 1) // 2)
    assert n % l1 == 0 and n == l1 * (n // l1)
    return l1, n // l1


def _fft_tables(n):
    l1, l2 = _fft_split(n)
    a = np.arange(l1, dtype=np.float64)
    ang1 = 2.0 * np.pi * np.outer(a, a) / l1
    stage1 = np.concatenate([np.cos(ang1), -np.sin(ang1)], axis=0)
    b = np.arange(l2, dtype=np.float64)
    ang_t = 2.0 * np.pi * np.outer(b, a) / n
    tw_cos = np.cos(ang_t)[:, :, None]
    tw_sin = np.sin(ang_t)[:, :, None]
    ang2 = 2.0 * np.pi * np.outer(b, b) / l2
    c2, s2 = np.cos(ang2), np.sin(ang2)
    stage2 = np.block([[c2, s2], [-s2, c2]])
    f32 = lambda x: jnp.asarray(x.astype(np.float32))
    return f32(stage1), f32(tw_cos), f32(tw_sin), f32(stage2)


FNET_GROUPS_PER_SLAB = V7X_LANES // FNET_GROUP_DIM
FNET_SLABS = FNET_WIDTH // V7X_LANES
FNET_UNROLL = 4


def _fnet_channel_tables(n_positions):
    c = np.arange(FNET_GROUP_DIM, dtype=np.float64)
    ang = 2.0 * np.pi * np.outer(c, c) / FNET_GROUP_DIM
    eye = np.eye(FNET_GROUPS_PER_SLAB)
    scale = 1.0 / math.sqrt(n_positions * FNET_GROUP_DIM)
    m = np.concatenate([np.kron(eye, np.cos(ang)), np.kron(eye, np.sin(ang))], axis=0) * scale
    return jnp.asarray(m.astype(np.float32))


def _fnet_part(f_ref, o_ref, a_ref, row0, n, s1_ref, tc_ref, ts_ref, s2_ref, ch_ref, wf_ref):
    l1, l2 = _fft_split(n)
    stage1 = s1_ref[...]
    stage2 = s2_ref[...]
    chan = ch_ref[...]
    wf = wf_ref[...]

    def first(j, carry):
        xs = f_ref[pl.ds(row0 + j, l1, stride=l2), :]
        a = _dot3(stage1, xs)
        ar, ai = a[:l1], a[l1:]
        tc, ts = tc_ref[j], ts_ref[j]
        a_ref[0, pl.ds(pl.multiple_of(j * l1, l1), l1), :] = ar * tc + ai * ts
        a_ref[1, pl.ds(pl.multiple_of(j * l1, l1), l1), :] = ai * tc - ar * ts
        return carry

    lax.fori_loop(0, l2, first, 0, unroll=FNET_UNROLL)

    def second(j, carry):
        br = a_ref[0, pl.ds(j, l2, stride=l1), :]
        bi = a_ref[1, pl.ds(j, l2, stride=l1), :]
        p = _dot3(stage2, jnp.concatenate([br, bi], axis=0))
        re = _dot3(jnp.concatenate([p[:l2], p[l2:]], axis=1), chan)
        o_ref[pl.ds(row0 + j, l2, stride=l1), :] = _bdot(re.astype(BF16), wf)
        return carry

    lax.fori_loop(0, l1, second, 0, unroll=FNET_UNROLL)


def _fnet_kernel(f_ref, s1l, tcl, tsl, s2l, chl, s1c, tcc, tsc, s2c, chc, wf_ref, o_ref, a_ref, *, n_lat, n_ctx):
    _fnet_part(f_ref, o_ref, a_ref, 0, n_lat, s1l, tcl, tsl, s2l, chl, wf_ref)
    _fnet_part(f_ref, o_ref, a_ref, n_lat, n_ctx, s1c, tcc, tsc, s2c, chc, wf_ref)


def _fnet_call(st, f, w_fnet):
    gps = FNET_GROUPS_PER_SLAB
    eye = jnp.eye(gps, dtype=F32)
    wg = w_fnet.reshape(FNET_SLABS, gps, FNET_GROUP_DIM, FNET_GROUP_DIM)
    wf = (eye[None, :, None, :, None] * wg[:, :, :, None, :]).reshape(FNET_SLABS, V7X_LANES, V7X_LANES).astype(BF16)
    consts = (*_fft_tables(st.n_lat), _fnet_channel_tables(st.n_lat),
              *_fft_tables(st.n_ctx), _fnet_channel_tables(st.n_ctx))
    f3 = f.reshape(st.bsz, st.nt, FNET_WIDTH)
    blk = pl.BlockSpec((None, st.nt, V7X_LANES), lambda b, s: (b, 0, s))
    o = pl.pallas_call(
        functools.partial(_fnet_kernel, n_lat=st.n_lat, n_ctx=st.n_ctx),
        grid=(st.bsz, FNET_SLABS),
        in_specs=[blk] + [_full_spec(c.shape) for c in consts]
        + [pl.BlockSpec((None, V7X_LANES, V7X_LANES), lambda b, s: (s, 0, 0))],
        out_specs=blk,
        out_shape=jax.ShapeDtypeStruct(f3.shape, F32),
        scratch_shapes=[pltpu.VMEM((2, st.n_lat, V7X_LANES), F32)],
        compiler_params=_params("arbitrary", "arbitrary"),
        name="fnet",
    )(f3, *consts, wf)
    return o.reshape(st.t, FNET_WIDTH)


SEG_ALIGN = V7X_SUBLANES
PAIRS_PER_TILE = TOKEN_TILE * TOP_K
LOCAL_ROWS = PAIRS_PER_TILE + N_EXPERTS * SEG_ALIGN
ROUTE_LANES = V7X_LANES


def _route(h2, wr, e_bias):
    tm = h2.shape[0]
    scores = jax.nn.sigmoid(_dot3(h2, wr))
    sel = scores + e_bias
    lane = lax.broadcasted_iota(jnp.int32, sel.shape, 1).astype(F32)
    hits = []
    for _ in range(TOP_K):
        best = sel.max(axis=-1, keepdims=True)
        first = jnp.where(sel == best, lane, float(N_EXPERTS)).min(axis=-1, keepdims=True)
        hit = lane == first
        hits.append(hit)
        sel = jnp.where(hit, -jnp.inf, sel)
    chosen = hits[0]
    for hit in hits[1:]:
        chosen = jnp.logical_or(chosen, hit)
    chosen_f = jnp.where(chosen, 1.0, 0.0)
    counts = chosen_f.sum(axis=0, keepdims=True)
    seg_len = jnp.ceil(counts * (1.0 / SEG_ALIGN)) * SEG_ALIGN
    er = lax.broadcasted_iota(jnp.int32, (N_EXPERTS, N_EXPERTS), 0)
    ec = lax.broadcasted_iota(jnp.int32, (N_EXPERTS, N_EXPERTS), 1)
    lower_experts = jnp.where(er < ec, 1.0, 0.0).astype(BF16)
    seg_start = _bdot(jnp.broadcast_to(seg_len, (V7X_SUBLANES, N_EXPERTS)).astype(BF16), lower_experts)[0:1]
    r = lax.broadcasted_iota(jnp.int32, (tm, tm), 0)
    c = lax.broadcasted_iota(jnp.int32, (tm, tm), 1)
    earlier = jnp.where(c < r, 1.0, 0.0).astype(BF16)
    row_all = _bdot(earlier, chosen_f.astype(BF16)) + seg_start
    rec_lane = lax.broadcasted_iota(jnp.int32, (tm, ROUTE_LANES), 1)
    rec = jnp.zeros((tm, ROUTE_LANES), F32)
    raw = []
    for k, hit in enumerate(hits):
        rec = jnp.where(rec_lane == k, jnp.where(hit, row_all, 0.0).sum(axis=-1, keepdims=True), rec)
        raw.append(jnp.where(hit, scores, 0.0).sum(axis=-1, keepdims=True))
    total = raw[0]
    for w in raw[1:]:
        total = total + w
    for k, w in enumerate(raw):
        rec = jnp.where(rec_lane == TOP_K + k, w / total * ROUTE_SCALE, rec)
    return rec, counts.astype(jnp.int32)


def _out_ln_kernel(*refs, n_in, alpha):
    a_refs = refs[:n_in]
    w_refs = refs[n_in:2 * n_in]
    x_ref, mod_ref, g_ref, b_ref, wr_ref, eb_ref, x1_ref, h2_ref, rec_ref, cnt_ref = refs[2 * n_in:]
    y = None
    for a_ref, w_ref in zip(a_refs, w_refs):
        part = _bdot(a_ref[...].astype(BF16), w_ref[...])
        y = part if y is None else y + part
    z = alpha * x_ref[...] + mod_ref[2:3, :] * y
    x1 = _layer_norm(z, g_ref[...], b_ref[...])
    x1_ref[...] = x1
    h2 = x1 * (1.0 + mod_ref[4:5, :]) + mod_ref[3:4, :]
    h2_ref[...] = h2.astype(BF16)
    rec, counts = _route(h2, wr_ref[...], eb_ref[...])
    rec_ref[...] = rec
    cnt_ref[...] = counts


def _out_ln_call(st, acts, w_out, x, mod, layer, ln_g, ln_b, w_router, e_bias, alpha):
    ws, r0 = [], 0
    for a in acts:
        ws.append(w_out[r0:r0 + a.shape[1]].astype(BF16))
        r0 += a.shape[1]
    assert r0 == w_out.shape[0]
    row = lambda v: v.reshape(1, -1)
    return pl.pallas_call(
        functools.partial(_out_ln_kernel, n_in=len(acts), alpha=alpha),
        grid=(st.n_tiles,),
        in_specs=[st.tok_spec(a.shape[1]) for a in acts] + [_full_spec(w.shape) for w in ws]
        + [st.tok_spec(st.d), st.mod_spec(layer), _full_spec((1, st.d)), _full_spec((1, st.d)),
           _full_spec(w_router.shape), _full_spec((1, N_EXPERTS))],
        out_specs=[st.tok_spec(st.d), st.tok_spec(st.d), st.tok_spec(ROUTE_LANES),
                   pl.BlockSpec((None, 1, N_EXPERTS), lambda t: (t, 0, 0))],
        out_shape=[jax.ShapeDtypeStruct((st.t, st.d), F32), jax.ShapeDtypeStruct((st.t, st.d), BF16),
                   jax.ShapeDtypeStruct((st.t, ROUTE_LANES), F32),
                   jax.ShapeDtypeStruct((st.n_tiles, 1, N_EXPERTS), jnp.int32)],
        compiler_params=_params("arbitrary"),
        name="out_ln",
    )(*acts, *ws, x, mod, row(ln_g), row(ln_b), w_router, row(e_bias))


MOE_BLOCK = 512
ROW_CHUNK = 256
ONEHOT_STRIP = 32
GROUP_UNROLL = 4
DMA_PRIORITIES = 2
ROW_GROUPS = LOCAL_ROWS // SEG_ALIGN
PLAN_WIDTH = (ROW_GROUPS // V7X_LANES + 1) * V7X_LANES
WAIT_SIZES = tuple(1 << b for b in range((LOCAL_ROWS - 1).bit_length() - 1, SEG_ALIGN.bit_length() - 2, -1))


def _moe_blocks(st):
    return (st.t * TOP_K + st.n_tiles * N_EXPERTS * (SEG_ALIGN - 1)) // MOE_BLOCK + N_EXPERTS


def _slot_plan(tile_counts, n_blocks):
    n_tiles = tile_counts.shape[0]
    cnt = tile_counts.reshape(n_tiles, N_EXPERTS)
    seg = (cnt + SEG_ALIGN - 1) // SEG_ALIGN * SEG_ALIGN
    local_start = jnp.cumsum(seg, axis=1) - seg
    tiles_before = jnp.cumsum(seg, axis=0) - seg
    total = seg.sum(axis=0)
    padded = (total + MOE_BLOCK - 1) // MOE_BLOCK * MOE_BLOCK
    pad_end = jnp.cumsum(padded)
    pad_start = pad_end - padded
    sorted_start = pad_start[None, :] + tiles_before
    group_row = jnp.arange(ROW_GROUPS, dtype=jnp.int32) * SEG_ALIGN
    owner = ((local_start[:, None, :] <= group_row[None, :, None])
             & (group_row[None, :, None] < (local_start + seg)[:, None, :]))
    group_dst = jnp.where(owner, (sorted_start - local_start)[:, None, :], 0).sum(-1) + group_row[None, :]
    tile_rows = jnp.broadcast_to(seg.sum(axis=1, keepdims=True), (n_tiles, PLAN_WIDTH - ROW_GROUPS))
    plan = jnp.concatenate([group_dst, tile_rows], axis=1).astype(jnp.int32).reshape(n_tiles, 1, PLAN_WIDTH)
    n_used = jnp.maximum(pad_end[-1] // MOE_BLOCK, 1)
    first_row = jnp.arange(n_blocks, dtype=jnp.int32) * MOE_BLOCK
    ends_expert = ((first_row[:, None] + MOE_BLOCK == pad_end[None, :]) & (padded[None, :] > 0)).any(-1)
    fill = (ends_expert | (first_row >= pad_end[-1])).astype(jnp.int32)
    first_block = (pad_start // MOE_BLOCK).astype(jnp.int32)
    block_count = (padded // MOE_BLOCK).astype(jnp.int32)
    return plan, first_block, block_count, n_used.astype(jnp.int32).reshape(1), fill


def _for_each_group(plan_ref, fn):
    def one(j, priority):
        fn(pl.multiple_of(j * SEG_ALIGN, SEG_ALIGN), pl.multiple_of(plan_ref[0, j], SEG_ALIGN), priority)

    def several(q, carry):
        for u in range(GROUP_UNROLL):
            one(q * GROUP_UNROLL + u, u % DMA_PRIORITIES)
        return carry

    def single(j, carry):
        one(j, 0)
        return carry

    n = plan_ref[0, ROW_GROUPS] // SEG_ALIGN
    lax.fori_loop(0, n // GROUP_UNROLL, several, 0)
    lax.fori_loop(n // GROUP_UNROLL * GROUP_UNROLL, n, single, 0)


def _wait_tile_rows(plan_ref, make_copy):
    rows = plan_ref[0, ROW_GROUPS]
    for size in WAIT_SIZES:
        @pl.when((rows & size) != 0)
        def _(size=size):
            make_copy(size).wait()


def _pack_halves(x):
    bits = pltpu.bitcast(x, jnp.uint32)
    half = x.shape[1] // 2
    return (bits[:, :half] >> 16) | (bits[:, half:] & jnp.uint32(0xFFFF0000))


def _unpack_halves(w):
    lo = pltpu.bitcast(w << 16, F32).astype(BF16)
    hi = pltpu.bitcast(w & jnp.uint32(0xFFFF0000), F32).astype(BF16)
    return lo, hi


def _dispatch_kernel(fill_ref, plan_ref, plan1_ref, plan2_ref, rec_ref, h_ref, xs_ref, loc_ref, zero_ref, onehot_ref,
                     zsem, sem,
                     *, n_blocks, n_tiles):
    i = pl.program_id(0)
    slot = i % 2

    @pl.when(i == 0)
    def _():
        zero_ref[...] = jnp.zeros_like(zero_ref)

        def fill(b):
            return pltpu.make_async_copy(zero_ref, xs_ref.at[pl.ds(pl.multiple_of(b * MOE_BLOCK, MOE_BLOCK), MOE_BLOCK)],
                                         zsem)

        def start(b, c):
            @pl.when(fill_ref[b] > 0)
            def _():
                fill(b).start()
            return c

        def wait(b, c):
            @pl.when(fill_ref[b] > 0)
            def _():
                fill(b).wait()
            return c

        lax.fori_loop(0, n_blocks, start, 0)
        lax.fori_loop(0, n_blocks, wait, 0)

    def piece(s):
        def copy(local, dst, size):
            return pltpu.make_async_copy(loc_ref.at[s, pl.ds(local, size)], xs_ref.at[pl.ds(dst, size)], sem.at[s])
        return copy

    def rows_done(s):
        return lambda size: piece(s)(0, 0, size)

    @pl.when(i >= 2)
    def _():
        _wait_tile_rows(plan2_ref, rows_done(slot))

    rows_of = rec_ref[...].T
    x = h_ref[...]
    for c0 in range(0, LOCAL_ROWS, ROW_CHUNK):
        wanted = [jnp.broadcast_to(jnp.clip(rows_of[k:k + 1, :] - c0, -1.0, float(ROW_CHUNK)).astype(BF16),
                                   (ONEHOT_STRIP, TOKEN_TILE)) for k in range(TOP_K)]
        for r0 in range(0, ROW_CHUNK, ONEHOT_STRIP):
            row = (lax.broadcasted_iota(jnp.int32, (ONEHOT_STRIP, TOKEN_TILE), 0) + r0).astype(F32).astype(BF16)
            onehot = jnp.zeros((ONEHOT_STRIP, TOKEN_TILE), BF16)
            for k in range(TOP_K):
                onehot = jnp.where(row == wanted[k], jnp.ones_like(onehot), onehot)
            onehot_ref[r0:r0 + ONEHOT_STRIP, :] = onehot
        loc_ref[slot, c0:c0 + ROW_CHUNK, :] = _pack_halves(_bdot(onehot_ref[...], x))

    _for_each_group(plan_ref, lambda local, dst, prio: piece(slot)(local, dst, SEG_ALIGN).start(priority=prio))

    @pl.when(i == n_tiles - 1)
    def _():
        if n_tiles >= 2:
            _wait_tile_rows(plan1_ref, rows_done(1 - slot))
        _wait_tile_rows(plan_ref, rows_done(slot))


def _dispatch_call(st, h2, rec, plan, fill, n_blocks):
    half = st.d // 2
    plan_spec = lambda back: pl.BlockSpec((None, 1, PLAN_WIDTH), lambda t, *_: (jnp.maximum(t - back, 0), 0, 0),
                                          memory_space=pltpu.SMEM)
    grid_spec = pltpu.PrefetchScalarGridSpec(
        num_scalar_prefetch=1,
        grid=(st.n_tiles,),
        in_specs=[plan_spec(0), plan_spec(1), plan_spec(2),
                  pl.BlockSpec((TOKEN_TILE, ROUTE_LANES), lambda t, *_: (t, 0)),
                  pl.BlockSpec((TOKEN_TILE, st.d), lambda t, *_: (t, 0))],
        out_specs=pl.BlockSpec(memory_space=pl.ANY),
        scratch_shapes=[pltpu.VMEM((2, LOCAL_ROWS, half), jnp.uint32), pltpu.VMEM((MOE_BLOCK, half), jnp.uint32),
                        pltpu.VMEM((ROW_CHUNK, TOKEN_TILE), BF16),
                        pltpu.SemaphoreType.DMA, pltpu.SemaphoreType.DMA((2,))],
    )
    return pl.pallas_call(
        functools.partial(_dispatch_kernel, n_blocks=n_blocks, n_tiles=st.n_tiles),
        grid_spec=grid_spec,
        out_shape=jax.ShapeDtypeStruct((n_blocks * MOE_BLOCK, half), jnp.uint32),
        compiler_params=_params("arbitrary"),
        name="moe_dispatch",
    )(fill, plan, plan, plan, rec, h2)


EXPERT_BUFFERS = 3


def _experts_kernel(first_ref, count_ref, nu_ref, xs_ref, wg_ref, wu_ref, wd_ref, ys_ref, xbuf, ybuf, wg_s, wu_s, wd_s,
                    xsem, ysem, *, n_blocks):
    e = pl.program_id(0)
    n_used = nu_ref[0]

    def rows(b):
        return pl.ds(pl.multiple_of(b * MOE_BLOCK, MOE_BLOCK), MOE_BLOCK)

    def fetch(b):
        s = b % EXPERT_BUFFERS
        return pltpu.make_async_copy(xs_ref.at[rows(b)], xbuf.at[s], xsem.at[s])

    def store(b):
        s = b % EXPERT_BUFFERS
        return pltpu.make_async_copy(ybuf.at[s], ys_ref.at[rows(b)], ysem.at[s])

    @pl.when(e == 0)
    def _():
        for b in range(EXPERT_BUFFERS - 1):
            @pl.when(b < n_used)
            def _(b=b):
                fetch(b).start()

    @pl.when(count_ref[e] > 0)
    def _():
        wg_s[...] = wg_ref[...].astype(BF16)
        wu_s[...] = wu_ref[...].astype(BF16)
        wd_s[...] = wd_ref[...].astype(BF16)

    def block(b, carry):
        @pl.when(b + EXPERT_BUFFERS - 1 < n_used)
        def _():
            fetch(b + EXPERT_BUFFERS - 1).start()

        fetch(b).wait()

        @pl.when(b >= EXPERT_BUFFERS)
        def _():
            store(b - EXPERT_BUFFERS).wait()

        slot = b % EXPERT_BUFFERS
        half = wg_s.shape[0] // 2
        lo, hi = _unpack_halves(xbuf[slot])
        gate = _bdot(lo, wg_s[:half, :]) + _bdot(hi, wg_s[half:, :])
        up = _bdot(lo, wu_s[:half, :]) + _bdot(hi, wu_s[half:, :])
        y = _bdot((_silu(gate) * up).astype(BF16), wd_s[...])
        ybuf[slot] = _pack_halves(y.astype(BF16).astype(F32))
        store(b).start()
        return carry

    lax.fori_loop(first_ref[e], first_ref[e] + count_ref[e], block, 0)

    @pl.when(e == N_EXPERTS - 1)
    def _():
        def drain(b, carry):
            store(b).wait()
            return carry

        lax.fori_loop(jnp.maximum(n_used - EXPERT_BUFFERS, 0), n_used, drain, 0)
        ybuf[0] = jnp.zeros(ybuf.shape[1:], ybuf.dtype)

        def zero_tail(b):
            return pltpu.make_async_copy(ybuf.at[0], ys_ref.at[rows(b)], ysem.at[0])

        def start(b, carry):
            zero_tail(b).start()
            return carry

        def wait(b, carry):
            zero_tail(b).wait()
            return carry

        lax.fori_loop(n_used, n_blocks, start, 0)
        lax.fori_loop(n_used, n_blocks, wait, 0)


def _experts_call(st, xs, first_block, block_count, n_used, w_gate, w_up, w_down, layer, n_blocks):
    half = st.d // 2
    w_in_spec = pl.BlockSpec((None, None, st.d, EXPERT_DIM), lambda e, *_: (layer, e, 0, 0))
    grid_spec = pltpu.PrefetchScalarGridSpec(
        num_scalar_prefetch=3,
        grid=(N_EXPERTS,),
        in_specs=[pl.BlockSpec(memory_space=pl.ANY),
                  w_in_spec, w_in_spec,
                  pl.BlockSpec((None, None, EXPERT_DIM, st.d), lambda e, *_: (layer, e, 0, 0))],
        out_specs=pl.BlockSpec(memory_space=pl.ANY),
        scratch_shapes=[pltpu.VMEM((EXPERT_BUFFERS, MOE_BLOCK, half), jnp.uint32),
                        pltpu.VMEM((EXPERT_BUFFERS, MOE_BLOCK, half), jnp.uint32),
                        pltpu.VMEM((st.d, EXPERT_DIM), BF16), pltpu.VMEM((st.d, EXPERT_DIM), BF16),
                        pltpu.VMEM((EXPERT_DIM, st.d), BF16),
                        pltpu.SemaphoreType.DMA((EXPERT_BUFFERS,)), pltpu.SemaphoreType.DMA((EXPERT_BUFFERS,))],
    )
    return pl.pallas_call(
        functools.partial(_experts_kernel, n_blocks=n_blocks),
        grid_spec=grid_spec,
        out_shape=jax.ShapeDtypeStruct(xs.shape, jnp.uint32),
        compiler_params=_params("arbitrary"),
        name="moe_experts",
    )(first_block, block_count, n_used, xs, w_gate, w_up, w_down)


def _ffn_ln_kernel(plan_ref, plan_next_ref, rec_ref, h2_ref, x1_ref, sg_ref, su_ref, sd_ref, mod_ref, modn_ref, g_ref,
                   b_ref, ys_ref, x2_ref, hn_ref, loc_ref, weight_ref, sem, *, alpha, n_tiles):
    i = pl.program_id(0)
    slot = i % 2

    def piece(s):
        def copy(local, src, size):
            return pltpu.make_async_copy(ys_ref.at[pl.ds(src, size)], loc_ref.at[s, pl.ds(local, size)], sem.at[s])
        return copy

    @pl.when(i == 0)
    def _():
        loc_ref[...] = jnp.zeros_like(loc_ref)
        _for_each_group(plan_ref, lambda local, src, prio: piece(0)(local, src, SEG_ALIGN).start(priority=prio))

    @pl.when(i + 1 < n_tiles)
    def _():
        _for_each_group(plan_next_ref,
                        lambda local, src, prio: piece(1 - slot)(local, src, SEG_ALIGN).start(priority=prio))

    h2 = h2_ref[...]
    a = _silu(_bdot(h2, sg_ref[...])) * _bdot(h2, su_ref[...])
    shared = _bdot(a.astype(BF16), sd_ref[...])

    _wait_tile_rows(plan_ref, lambda size: piece(slot)(0, 0, size))
    rec_t = rec_ref[...].T
    shares = [jnp.broadcast_to(rec_t[TOP_K + k:TOP_K + k + 1, :].astype(BF16), (ONEHOT_STRIP, TOKEN_TILE))
              for k in range(TOP_K)]
    half = loc_ref.shape[2]
    routed_lo = jnp.zeros((TOKEN_TILE, half), F32)
    routed_hi = jnp.zeros((TOKEN_TILE, half), F32)
    for c0 in range(0, LOCAL_ROWS, ROW_CHUNK):
        wanted = [jnp.broadcast_to(jnp.clip(rec_t[k:k + 1, :] - c0, -1.0, float(ROW_CHUNK)).astype(BF16),
                                   (ONEHOT_STRIP, TOKEN_TILE)) for k in range(TOP_K)]
        for r0 in range(0, ROW_CHUNK, ONEHOT_STRIP):
            row = (lax.broadcasted_iota(jnp.int32, (ONEHOT_STRIP, TOKEN_TILE), 0) + r0).astype(F32).astype(BF16)
            weight = jnp.zeros((ONEHOT_STRIP, TOKEN_TILE), BF16)
            for k in range(TOP_K):
                weight = jnp.where(row == wanted[k], shares[k], weight)
            weight_ref[r0:r0 + ONEHOT_STRIP, :] = weight
        lo, hi = _unpack_halves(loc_ref[slot, c0:c0 + ROW_CHUNK, :])
        weight = weight_ref[...]
        routed_lo = routed_lo + _bdot_tn(weight, lo)
        routed_hi = routed_hi + _bdot_tn(weight, hi)
    ff = jnp.concatenate([routed_lo, routed_hi], axis=1) + shared
    z = alpha * x1_ref[...] + mod_ref[5:6, :] * ff
    x2 = _layer_norm(z, g_ref[...], b_ref[...])
    x2_ref[...] = x2
    hn_ref[...] = (x2 * (1.0 + modn_ref[1:2, :]) + modn_ref[0:1, :]).astype(BF16)


def _ffn_ln_call(st, ys, plan, rec, h2, x1, s_gate, s_up, s_down, mod, layer, next_layer, ln_g, ln_b, alpha):
    row = lambda v: v.reshape(1, -1)
    sg, su, sd = s_gate.astype(BF16), s_up.astype(BF16), s_down.astype(BF16)
    last = st.n_tiles - 1
    return pl.pallas_call(
        functools.partial(_ffn_ln_kernel, alpha=alpha, n_tiles=st.n_tiles),
        grid=(st.n_tiles,),
        in_specs=[pl.BlockSpec((None, 1, PLAN_WIDTH), lambda t: (t, 0, 0), memory_space=pltpu.SMEM),
                  pl.BlockSpec((None, 1, PLAN_WIDTH), lambda t: (jnp.minimum(t + 1, last), 0, 0),
                               memory_space=pltpu.SMEM),
                  st.tok_spec(ROUTE_LANES), st.tok_spec(st.d), st.tok_spec(st.d),
                  _full_spec(sg.shape), _full_spec(su.shape), _full_spec(sd.shape),
                  st.mod_spec(layer), st.mod_spec(next_layer), _full_spec((1, st.d)), _full_spec((1, st.d)),
                  pl.BlockSpec(memory_space=pl.ANY)],
        out_specs=[st.tok_spec(st.d), st.tok_spec(st.d)],
        out_shape=[jax.ShapeDtypeStruct((st.t, st.d), F32), jax.ShapeDtypeStruct((st.t, st.d), BF16)],
        scratch_shapes=[pltpu.VMEM((2, LOCAL_ROWS, st.d // 2), jnp.uint32),
                        pltpu.VMEM((TOKEN_TILE, ROW_CHUNK), BF16), pltpu.SemaphoreType.DMA((2,))],
        compiler_params=_params("arbitrary"),
        name="ffn_ln",
    )(plan, plan, rec, h2, x1, sg, su, sd, mod, mod, row(ln_g), row(ln_b), ys)


def _na_in_kernel(h_ref, w_ref, o_ref):
    h = h_ref[...]
    o_ref[:, :NA_WIDTH] = (_bdot(h, w_ref[:, :NA_WIDTH]) * SCORE_SCALE).astype(BF16)
    o_ref[:, NA_WIDTH:] = _bdot(h, w_ref[:, NA_WIDTH:]).astype(BF16)


def _na_in_call(st, h, w_in):
    return pl.pallas_call(
        _na_in_kernel,
        grid=(st.n_tiles,),
        in_specs=[st.tok_spec(st.d), _full_spec(w_in.shape)],
        out_specs=st.tok_spec(3 * NA_WIDTH),
        out_shape=jax.ShapeDtypeStruct((st.t, 3 * NA_WIDTH), BF16),
        compiler_params=_params("arbitrary"),
        name="na_in",
    )(h, w_in)


def _na_geometry(n_lat):
    rows = n_lat // GRID_W
    kh, kw, qr = min(NA_KH, rows), min(NA_KW, GRID_W), NA_Q_ROWS
    nbr = min(qr + kh - 1, rows)
    col = np.arange(GRID_W)
    col_start = np.clip(col - kw // 2, 0, GRID_W - kw)
    in_col = (col[None, :] >= col_start[:, None]) & (col[None, :] < col_start[:, None] + kw)
    dc = np.clip(col[None, :] - col[:, None] + NA_KW - 1, 0, 2 * NA_KW - 2)
    starts, variant_of, variants = [], [], {}
    for i in range(rows // qr):
        qrow = i * qr + np.arange(qr)
        rstart = np.clip(qrow - kh // 2, 0, rows - kh)
        bs = min(int(rstart[0]), rows - nbr)
        krow = bs + np.arange(nbr)
        in_row = (krow[None, :] >= rstart[:, None]) & (krow[None, :] < rstart[:, None] + kh)
        dr = np.clip(krow[None, :] - qrow[:, None] + NA_KH - 1, 0, 2 * NA_KH - 2)
        key = (in_row.tobytes(), dr.tobytes())
        if key not in variants:
            mask = (in_row[:, None, :, None] & in_col[None, :, None, :]).reshape(qr * GRID_W, nbr * GRID_W)
            variants[key] = (len(variants), dr, mask)
        starts.append(bs)
        variant_of.append(variants[key][0])
    ordered = sorted(variants.values(), key=lambda v: v[0])
    return nbr, np.asarray(starts, np.int32), np.asarray(variant_of, np.int32), [(v[1], v[2]) for v in ordered], dc


def _na_bias_kernel(tab_ref, place_ref, mask_ref, o_ref):
    tab = tab_ref[...]
    hi = tab.astype(BF16)
    rest = tab - hi.astype(F32)
    mid = rest.astype(BF16)
    lo = (rest - mid.astype(F32)).astype(BF16)
    for a in range(NA_Q_ROWS):
        place = place_ref[a]
        rows = _bdot(hi, place) + (_bdot(mid, place) + _bdot(lo, place))
        sl = slice(a * GRID_W, (a + 1) * GRID_W)
        o_ref[sl, :] = jnp.where(mask_ref[sl, :] > 0.0, rows * LOG2_E, NEG_INF)


def _na_bias_tables(rpb, n_lat, n_ctx):
    nbr, starts, variant_of, variants, dc = _na_geometry(n_lat)
    n_dr, n_dc = 2 * NA_KH - 1, 2 * NA_KW - 1
    band = nbr * GRID_W
    n_keys = band + n_ctx
    col_sel = jnp.asarray((dc[:, :, None] == np.arange(n_dc)).astype(np.float32))
    tab = jnp.einsum("hrc,wuc->hwru", rpb, col_sel, precision=lax.Precision.HIGHEST)
    tab = tab.reshape(NA_HEADS, GRID_W, n_dr * GRID_W)
    place = np.zeros((len(variants), NA_Q_ROWS, n_dr, GRID_W, n_keys), np.float32)
    masks = np.ones((len(variants), NA_Q_ROWS * GRID_W, n_keys), np.float32)
    u = np.arange(GRID_W)
    for v, (dr, mask) in enumerate(variants):
        masks[v, :, :band] = mask
        for a in range(NA_Q_ROWS):
            for j in range(nbr):
                place[v, a, dr[a, j], u, j * GRID_W + u] = 1.0
    place = jnp.asarray(place.reshape(len(variants), NA_Q_ROWS, n_dr * GRID_W, n_keys), BF16)
    bias = pl.pallas_call(
        _na_bias_kernel,
        grid=(len(variants), NA_HEADS),
        in_specs=[pl.BlockSpec((None, GRID_W, n_dr * GRID_W), lambda v, h: (h, 0, 0)),
                  pl.BlockSpec((None, NA_Q_ROWS, n_dr * GRID_W, n_keys), lambda v, h: (v, 0, 0, 0)),
                  pl.BlockSpec((None, NA_Q_ROWS * GRID_W, n_keys), lambda v, h: (v, 0, 0))],
        out_specs=pl.BlockSpec((None, None, NA_Q_ROWS * GRID_W, n_keys), lambda v, h: (v, h, 0, 0)),
        out_shape=jax.ShapeDtypeStruct((len(variants), NA_HEADS, NA_Q_ROWS * GRID_W, n_keys), F32),
        compiler_params=_params("arbitrary", "arbitrary"),
        name="na_bias",
    )(tab, place, jnp.asarray(masks))
    return nbr, starts, variant_of, bias


NA_Q_TILE = NA_Q_ROWS * GRID_W


def _na_kernel(start_ref, var_ref, q_ref, k_ref, v_ref, bias_ref, o_ref, *, n_lat, band):
    i = pl.program_id(1)
    is_lat = i < n_lat // NA_Q_TILE

    @pl.when(is_lat)
    def _():
        off = pl.multiple_of(start_ref[i] * GRID_W, GRID_W)
        for h in range(NA_HEADS):
            sl = slice(h * HEAD_DIM, (h + 1) * HEAD_DIM)
            keys = jnp.concatenate([k_ref[pl.ds(off, band), sl], k_ref[n_lat:, sl]], axis=0)
            values = jnp.concatenate([v_ref[pl.ds(off, band), sl], v_ref[n_lat:, sl]], axis=0)
            o_ref[:, sl] = _attend(q_ref[:, sl], [(keys, values)], [bias_ref[h]]).astype(BF16)

    @pl.when(jnp.logical_not(is_lat))
    def _():
        for h in range(NA_HEADS):
            sl = slice(h * HEAD_DIM, (h + 1) * HEAD_DIM)
            o_ref[:, sl] = _attend(q_ref[:, sl], [(k_ref[n_lat:, sl], v_ref[n_lat:, sl])], [None]).astype(BF16)


def _na_call(st, qkv, rpb):
    nbr, starts, variant_of, bias = _na_bias_tables(rpb, st.n_lat, st.n_ctx)
    band = nbr * GRID_W
    n_q = st.nt // NA_Q_TILE
    pad = n_q - starts.shape[0]
    starts = jnp.asarray(np.concatenate([starts, np.zeros(pad, np.int32)]))
    variant_of = jnp.asarray(np.concatenate([variant_of, np.zeros(pad, np.int32)]))
    qkv3 = qkv.reshape(st.bsz, st.nt, 3 * NA_WIDTH)
    grid_spec = pltpu.PrefetchScalarGridSpec(
        num_scalar_prefetch=2,
        grid=(st.bsz, n_q),
        in_specs=[pl.BlockSpec((None, NA_Q_TILE, NA_WIDTH), lambda b, i, s, v: (b, i, 0)),
                  pl.BlockSpec((None, st.nt, NA_WIDTH), lambda b, i, s, v: (b, 0, 1)),
                  pl.BlockSpec((None, st.nt, NA_WIDTH), lambda b, i, s, v: (b, 0, 2)),
                  pl.BlockSpec((None, NA_HEADS, NA_Q_TILE, band + st.n_ctx), lambda b, i, s, v: (v[i], 0, 0, 0))],
        out_specs=pl.BlockSpec((None, NA_Q_TILE, NA_WIDTH), lambda b, i, s, v: (b, i, 0)),
    )
    o = pl.pallas_call(
        functools.partial(_na_kernel, n_lat=st.n_lat, band=band),
        grid_spec=grid_spec,
        out_shape=jax.ShapeDtypeStruct((st.bsz, st.nt, NA_WIDTH), BF16),
        compiler_params=_params("arbitrary", "arbitrary"),
        name="na_attn",
    )(starts, variant_of, qkv3, qkv3, qkv3, bias)
    return o.reshape(st.t, NA_WIDTH)


def kernel(x, c, ctx, c_ctx, w_mod, b_mod, ln1_g, ln1_b, ln2_g, ln2_b, ab_w_in, ab_w_fnet, ab_q_norm, ab_k_norm,
           ab_w_out, na_w_in, na_rpb, na_w_out, moe_w_router, moe_bias, moe_w_gate, moe_w_up, moe_w_down,
           sh_w_gate, sh_w_up, sh_w_down):
    bsz, n_lat, d = x.shape
    n_ctx = ctx.shape[1]
    depth = w_mod.shape[0]
    st = _Stream(bsz, n_lat, n_ctx, d)
    alpha = (2 * depth) ** 0.25

    cc = jnp.concatenate([c, c_ctx[None, :], jnp.zeros((MOD_ROWS - bsz - 1, d), F32)], axis=0)
    mod = _mod_call(cc, w_mod, b_mod).reshape(depth, MOD_ROWS, 6, d)
    xs = jnp.concatenate([x, ctx], axis=1).reshape(st.t, d)
    h = _modulate_call(st, xs, mod, 0)
    rope = _rope_tables(n_lat, n_ctx)
    n_blocks = _moe_blocks(st)

    for l in range(depth):
        j = l // 2
        if l % 2 == 0:
            f, q, k, v = _ab_in_call(st, h, ab_w_in[j].astype(BF16), ab_q_norm[j], ab_k_norm[j], rope)
            acts = [_fnet_call(st, f, ab_w_fnet[j]), _gqa_call(st, q, k, v)]
            w_out = ab_w_out[j]
        else:
            qkv = _na_in_call(st, h, na_w_in[j].astype(BF16))
            acts = [_na_call(st, qkv, na_rpb[j])]
            w_out = na_w_out[j]
        x1, h2, rec, tile_counts = _out_ln_call(st, acts, w_out, xs, mod, l, ln1_g[l], ln1_b[l],
                                                moe_w_router[l], moe_bias[l], alpha)
        plan, first_block, block_count, n_used, fill = _slot_plan(tile_counts, n_blocks)
        rows = _dispatch_call(st, h2, rec, plan, fill, n_blocks)
        ys = _experts_call(st, rows, first_block, block_count, n_used, moe_w_gate, moe_w_up, moe_w_down, l, n_blocks)
        xs, h = _ffn_ln_call(st, ys, plan, rec, h2, x1, sh_w_gate[l], sh_w_up[l], sh_w_down[l], mod, l,
                             min(l + 1, depth - 1), ln2_g[l], ln2_b[l], alpha)
    return xs.reshape(bsz, st.nt, d)[:, :n_lat]
```
